```python
import math
import jax
import jax.numpy as jnp
from jax import lax
import numpy as np


D_MODEL = 1024
BATCH = 4
SEQ = 8192
DEPTH = 2

D_INNER = 2 * D_MODEL
GROUP_W = D_INNER // 4
SHORT_CONV = 3
RMS_EPS = 1e-6

HY_W = GROUP_W
HY_ORDER = 2
HY_POS_BANDS = 8
HY_POS_DIM = 1 + 2 * HY_POS_BANDS
HY_FILT_HID = 64
HY_FILT_OUT = HY_ORDER * 2 * HY_W

MB_W = GROUP_W
MB_HEADDIM = 64
MB_H = MB_W // MB_HEADDIM
MB_GROUPS = 2
MB_STATE = 128
MB_XBC = MB_W + 2 * MB_GROUPS * MB_STATE
MB_CHUNK = 128

ML_W = GROUP_W
ML_H = 4
ML_DH = ML_W // ML_H
ML_CHUNK = 128

NA_W = GROUP_W
NA_DH = 64
NA_H = NA_W // NA_DH
NA_KR_MAX = 8
NA_KC = 16
GRID_W = 64

IN_SPLITS = (3 * HY_W, HY_W, MB_XBC, MB_W, 2 * MB_H, 2 * ML_W, ML_W, ML_W, ML_W, 4 * ML_H, 3 * NA_W, NA_W)
N_IN = sum(IN_SPLITS)

kernel_name = "hybrid_parallel_heads_encoder"


def _split_cols(h, widths):
    out, start = [], 0
    for w in widths:
        out.append(h[..., start:start + w])
        start += w
    return out


def _rmsnorm(x, w):
    xf = x.astype(jnp.float32)
    y = xf * lax.rsqrt(jnp.mean(xf * xf, axis=-1, keepdims=True) + RMS_EPS)
    return (y * w.astype(jnp.float32)).astype(x.dtype)


def _short_conv(u, w, b):
    pad = w.shape[0] // 2
    y = lax.conv_general_dilated(u, w[:, None, :], window_strides=(1,), padding=[(pad, pad)],
                                 dimension_numbers=('NWC', 'WIO', 'NWC'), feature_group_count=u.shape[-1])
    return y + b


def _hyena_positions(L):
    t = jnp.arange(L, dtype=jnp.float32)
    t_norm = t / L
    bands = jnp.arange(1, HY_POS_BANDS + 1, dtype=jnp.float32)
    ang = (2.0 * math.pi / L) * t[:, None] * bands[None, :]
    pos = jnp.concatenate([t_norm[:, None], jnp.cos(ang), jnp.sin(ang)], axis=-1)
    return pos, t_norm


def _hyena_filters(pos, t_norm, w1, b1, w2, b2, w3, freq, decay):
    hid = jnp.sin(freq * (pos @ w1 + b1))
    hid = jnp.sin(freq * (hid @ w2 + b2))
    filt = (hid @ w3) * jnp.exp(-t_norm[:, None] * decay)
    return filt.reshape(-1, HY_ORDER, 2, HY_W)


def _bidir_fftconv(u, h_fwd, h_bwd, skip):
    L, C = h_fwd.shape
    n = 2 * L
    g = jnp.concatenate([h_fwd, jnp.zeros((1, C), h_fwd.dtype), h_bwd[:0:-1]], axis=0).astype(jnp.float32)
    uf = u.astype(jnp.float32)
    y = jnp.fft.irfft(jnp.fft.rfft(uf, n=n, axis=1) * jnp.fft.rfft(g, n=n, axis=0)[None], n=n, axis=1)[:, :L]
    return (y + uf * skip.astype(jnp.float32)).astype(u.dtype)


def _hyena_branch(u3, gate, conv_w, conv_b, filt, skip):
    u3 = _short_conv(u3, conv_w, conv_b)
    v, x1, x2 = _split_cols(u3, (HY_W, HY_W, HY_W))
    z = x1 * _bidir_fftconv(v, filt[:, 0, 0], filt[:, 0, 1], skip[0])
    y = x2 * _bidir_fftconv(z, filt[:, 1, 0], filt[:, 1, 1], skip[1])
    return y * jax.nn.silu(gate)


def _segsum_exp(a):
    T = a.shape[-1]
    cs = jnp.cumsum(a, axis=-1)
    mask = jnp.tril(jnp.ones((T, T), dtype=bool))
    return jnp.exp(jnp.where(mask, cs[..., :, None] - cs[..., None, :], -jnp.inf))


def _ssd_scan(x, dt, A, Bm, Cm):
    bsz, L, H, P = x.shape
    G, N = Bm.shape[2], Bm.shape[3]
    J = H // G
    Q = MB_CHUNK
    nc = L // Q
    xc = (x * dt[..., None]).reshape(bsz, nc, Q, G, J, P)
    a = (dt * A).reshape(bsz, nc, Q, G, J).transpose(0, 3, 4, 1, 2)
    Bc = Bm.reshape(bsz, nc, Q, G, N)
    Cc = Cm.reshape(bsz, nc, Q, G, N)
    a_cs = jnp.cumsum(a, axis=-1)
    CB = jnp.einsum('bclgn,bcsgn->bcgls', Cc, Bc)
    y_diag = jnp.einsum('bcgls,bgjcls,bcsgjp->bclgjp', CB, _segsum_exp(a), xc)
    decay_states = jnp.exp(a_cs[..., -1:] - a_cs)
    states = jnp.einsum('bcsgn,bgjcs,bcsgjp->bcgjpn', Bc, decay_states, xc)
    states = jnp.concatenate([jnp.zeros_like(states[:, :1]), states], axis=1)
    decay_chunk = _segsum_exp(jnp.pad(a_cs[..., -1], ((0, 0), (0, 0), (0, 0), (1, 0))))
    prev_states = jnp.einsum('bgjzc,bcgjpn->bzgjpn', decay_chunk, states)[:, :-1]
    y_off = jnp.einsum('bclgn,bcgjpn,bgjcl->bclgjp', Cc, prev_states, jnp.exp(a_cs))
    return (y_diag + y_off).reshape(bsz, L, H, P)


def _mamba2_branch(xbc, z, dt_raw, conv_w, conv_b, dt_bias, a_log, d_skip, norm_w):
    bsz, L, _ = xbc.shape
    xbc = jax.nn.silu(_short_conv(xbc, conv_w, conv_b))
    xs, bm, cm = _split_cols(xbc, (MB_W, MB_GROUPS * MB_STATE, MB_GROUPS * MB_STATE))
    xs = xs.reshape(bsz, L, MB_H, MB_HEADDIM).astype(jnp.float32)
    bm = bm.reshape(bsz, L, MB_GROUPS, MB_STATE).astype(jnp.float32)
    cm = cm.reshape(bsz, L, MB_GROUPS, MB_STATE).astype(jnp.float32)
    dt = jax.nn.softplus(dt_raw.reshape(bsz, L, 2, MB_H).astype(jnp.float32) + dt_bias.astype(jnp.float32))
    A = -jnp.exp(a_log.astype(jnp.float32))
    y_f = _ssd_scan(xs, dt[:, :, 0], A[0], bm, cm)
    y_b = jnp.flip(_ssd_scan(jnp.flip(xs, 1), jnp.flip(dt[:, :, 1], 1), A[1], jnp.flip(bm, 1), jnp.flip(cm, 1)), 1)
    y = (y_f + y_b + xs * d_skip.astype(jnp.float32)[:, None]).reshape(bsz, L, MB_W)
    y = y.astype(z.dtype) * jax.nn.silu(z)
    y = _rmsnorm(y.reshape(bsz, L, MB_GROUPS, MB_W // MB_GROUPS), norm_w.reshape(MB_GROUPS, MB_W // MB_GROUPS))
    return y.reshape(bsz, L, MB_W)


def _mlstm_scan(q, k, v, ig, lf):
    bsz, H, L, Dh = q.shape
    Q = ML_CHUNK
    nc = L // Q

    def to_chunks(a):
        return jnp.moveaxis(a.reshape(bsz, H, nc, Q, *a.shape[3:]), 2, 0)

    causal = jnp.tril(jnp.ones((Q, Q), dtype=bool))

    def step(carry, inp):
        Cm, n, m = carry
        qq, kk, vv, ii, ff = inp
        b = jnp.cumsum(ff, axis=-1)
        dmat = jnp.where(causal, b[..., :, None] - b[..., None, :] + ii[..., None, :], -jnp.inf)
        m_inter = b + m[..., None]
        m_t = jnp.maximum(m_inter, jnp.max(dmat, axis=-1))
        w_inter = jnp.exp(m_inter - m_t)
        s = jnp.einsum('bhtd,bhsd->bhts', qq, kk) * jnp.exp(dmat - m_t[..., None])
        num = w_inter[..., None] * jnp.einsum('bhtd,bhde->bhte', qq, Cm) + jnp.einsum('bhts,bhse->bhte', s, vv)
        den = w_inter * jnp.einsum('bhtd,bhd->bht', qq, n) + jnp.sum(s, axis=-1)
        h = num / jnp.maximum(jnp.abs(den), jnp.exp(-m_t))[..., None]
        b_last = b[..., -1]
        g = b_last[..., None] - b + ii
        m_new = jnp.maximum(b_last + m, jnp.max(g, axis=-1))
        wk = jnp.exp(g - m_new[..., None])
        decay = jnp.exp(b_last + m - m_new)
        C_new = decay[..., None, None] * Cm + jnp.einsum('bhs,bhsd,bhse->bhde', wk, kk, vv)
        n_new = decay[..., None] * n + jnp.einsum('bhs,bhsd->bhd', wk, kk)
        return (C_new, n_new, m_new), h

    init = (jnp.zeros((bsz, H, Dh, Dh), jnp.float32), jnp.zeros((bsz, H, Dh), jnp.float32),
            jnp.zeros((bsz, H), jnp.float32))
    _, hs = lax.scan(step, init, (to_chunks(q), to_chunks(k), to_chunks(v), to_chunks(ig), to_chunks(lf)))
    return jnp.moveaxis(hs, 0, 2).reshape(bsz, H, L, Dh)


def _mlstm_branch(qk, v, o, z, gates, conv_w, conv_b, gate_b, norm_w):
    bsz, L, _ = v.shape
    qk = jax.nn.silu(_short_conv(qk, conv_w, conv_b))

    def heads(t):
        return t.reshape(bsz, L, ML_H, ML_DH).transpose(0, 2, 1, 3).astype(jnp.float32)

    q = heads(qk[..., :ML_W])
    k = heads(qk[..., ML_W:]) * (ML_DH ** -0.5)
    vv = heads(v)
    g = gates.reshape(bsz, L, 2, 2, ML_H).astype(jnp.float32) + gate_b.astype(jnp.float32)
    g = g.transpose(2, 3, 0, 4, 1)
    ig = g[:, 0]
    lf = jax.nn.log_sigmoid(g[:, 1])
    h_f = _mlstm_scan(q, k, vv, ig[0], lf[0])
    h_b = jnp.flip(_mlstm_scan(jnp.flip(q, 2), jnp.flip(k, 2), jnp.flip(vv, 2),
                               jnp.flip(ig[1], 2), jnp.flip(lf[1], 2)), 2)
    h = (h_f + h_b).transpose(0, 2, 1, 3).reshape(bsz, L, ML_W)
    h = jax.nn.sigmoid(o.astype(jnp.float32)) * h
    h = _rmsnorm(h.reshape(bsz, L, ML_H, ML_DH), norm_w.reshape(ML_H, ML_DH)).reshape(bsz, L, ML_W)
    return h.astype(z.dtype) * jax.nn.silu(z)


def _neighbourhood_attention(q, k, v, rpb):
    bsz, seq, nh, dh = q.shape
    rows = seq // GRID_W
    kr = min(NA_KR_MAX, rows)
    kc = NA_KC
    qg = q.reshape(bsz, rows, GRID_W, nh, dh)
    kg = k.reshape(bsz, rows, GRID_W, nh, dh)
    vg = v.reshape(bsz, rows, GRID_W, nh, dh)
    row_start = jnp.clip(jnp.arange(rows) - kr // 2, 0, rows - kr)
    cols = jnp.arange(GRID_W)
    col_idx = jnp.clip(cols - kc // 2, 0, GRID_W - kc)[:, None] + jnp.arange(kc)[None, :]
    col_off = col_idx - cols[:, None] + (kc - 1)
    rpb_cols = rpb[:, :, col_off]

    def one_row(r):
        rs = row_start[r]
        q_row = lax.dynamic_index_in_dim(qg, r, axis=1, keepdims=False)
        k_win = lax.dynamic_slice_in_dim(kg, rs, kr, axis=1)[:, :, col_idx]
        v_win = lax.dynamic_slice_in_dim(vg, rs, kr, axis=1)[:, :, col_idx]
        row_off = rs + jnp.arange(kr) - r + (NA_KR_MAX - 1)
        bias = rpb_cols[:, row_off].transpose(0, 2, 1, 3)
        s = jnp.einsum('bwhd,biwjhd->bhwij', q_row, k_win).astype(jnp.float32) + bias.astype(jnp.float32)
        p = jax.nn.softmax(s.reshape(bsz, nh, GRID_W, kr * kc), axis=-1).reshape(s.shape)
        return jnp.einsum('bhwij,biwjhd->bwhd', p.astype(v.dtype), v_win)

    out = lax.map(one_row, jnp.arange(rows))
    return out.transpose(1, 0, 2, 3, 4).reshape(bsz, seq, nh, dh)


def _na_branch(qkv, gate, qn_w, kn_w, rpb):
    bsz, L, _ = qkv.shape
    q, k, v = [t.reshape(bsz, L, NA_H, NA_DH) for t in _split_cols(qkv, (NA_W, NA_W, NA_W))]
    q = _rmsnorm(q, qn_w) * (NA_DH ** -0.5)
    k = _rmsnorm(k, kn_w)
    o = _neighbourhood_attention(q, k, v, rpb)
    return o.reshape(bsz, L, NA_W) * jax.nn.silu(gate)


def setup_inputs(seed: int = 0) -> dict:
    key = jax.random.key(seed)
    ks = iter(jax.random.split(key, 40))

    def nrm(shape, scale):
        return scale * jax.random.normal(next(ks), shape, jnp.float32)

    def near_one(shape, s=0.02):
        return 1.0 + s * jax.random.normal(next(ks), shape, jnp.float32)

    x = jax.random.normal(next(ks), (BATCH, SEQ, D_MODEL), jnp.float32)
    norm_w = near_one((DEPTH, D_MODEL))
    w_in = nrm((DEPTH, D_MODEL, N_IN), D_MODEL ** -0.5)
    w_out = nrm((DEPTH, D_INNER, D_MODEL), D_INNER ** -0.5)
    hy_conv_w = nrm((DEPTH, SHORT_CONV, 3 * HY_W), SHORT_CONV ** -0.5)
    hy_conv_b = nrm((DEPTH, 3 * HY_W), 0.01)
    hy_w1 = nrm((DEPTH, HY_POS_DIM, HY_FILT_HID), HY_POS_DIM ** -0.5)
    hy_b1 = nrm((DEPTH, HY_FILT_HID), 0.1)
    hy_w2 = nrm((DEPTH, HY_FILT_HID, HY_FILT_HID), HY_FILT_HID ** -0.5)
    hy_b2 = nrm((DEPTH, HY_FILT_HID), 0.1)
    hy_w3 = nrm((DEPTH, HY_FILT_HID, HY_FILT_OUT), 0.02 * HY_FILT_HID ** -0.5)
    hy_freq = near_one((DEPTH, HY_FILT_HID), 0.1)
    base_decay = jnp.abs(jnp.linspace(math.log(1e-2) / 1.5, math.log(1e-2) / 0.3, HY_W, dtype=jnp.float32))
    hy_decay = jnp.tile(base_decay, HY_ORDER * 2)[None, :] * near_one((DEPTH, HY_FILT_OUT), 0.05)
    hy_skip = near_one((DEPTH, HY_ORDER, HY_W), 0.1)
    mb_conv_w = nrm((DEPTH, SHORT_CONV, MB_XBC), SHORT_CONV ** -0.5)
    mb_conv_b = nrm((DEPTH, MB_XBC), 0.01)
    u = jax.random.uniform(next(ks), (DEPTH, 2, MB_H), jnp.float32)
    dt0 = jnp.exp(u * (math.log(0.1) - math.log(1e-3)) + math.log(1e-3))
    mb_dt_bias = dt0 + jnp.log(-jnp.expm1(-dt0))
    mb_a_log = jnp.log(jax.random.uniform(next(ks), (DEPTH, 2, MB_H), jnp.float32, 1.0, 16.0))
    mb_d = near_one((DEPTH, MB_H), 0.1)
    mb_norm_w = near_one((DEPTH, MB_W))
    ml_conv_w = nrm((DEPTH, SHORT_CONV, 2 * ML_W), SHORT_CONV ** -0.5)
    ml_conv_b = nrm((DEPTH, 2 * ML_W), 0.01)
    ig_b = nrm((DEPTH, 2, 1, ML_H), 0.1)
    fg_b = jnp.linspace(3.0, 6.0, ML_H, dtype=jnp.float32) + nrm((DEPTH, 2, 1, ML_H), 0.1)
    ml_gate_b = jnp.concatenate([ig_b, fg_b], axis=2)
    ml_norm_w = near_one((DEPTH, ML_W))
    na_qnorm_w = near_one((DEPTH, NA_DH))
    na_knorm_w = near_one((DEPTH, NA_DH))
    na_rpb = nrm((DEPTH, NA_H, 2 * NA_KR_MAX - 1, 2 * NA_KC - 1), 0.02)
    return {"x": x, "norm_w": norm_w, "w_in": w_in, "w_out": w_out,
            "hy_conv_w": hy_conv_w, "hy_conv_b": hy_conv_b, "hy_w1": hy_w1, "hy_b1": hy_b1,
            "hy_w2": hy_w2, "hy_b2": hy_b2, "hy_w3": hy_w3, "hy_freq": hy_freq, "hy_decay": hy_decay,
            "hy_skip": hy_skip, "mb_conv_w": mb_conv_w, "mb_conv_b": mb_conv_b, "mb_dt_bias": mb_dt_bias,
            "mb_a_log": mb_a_log, "mb_d": mb_d, "mb_norm_w": mb_norm_w, "ml_conv_w": ml_conv_w,
            "ml_conv_b": ml_conv_b, "ml_gate_b": ml_gate_b, "ml_norm_w": ml_norm_w,
            "na_qnorm_w": na_qnorm_w, "na_knorm_w": na_knorm_w, "na_rpb": na_rpb}


def reference(x, norm_w, w_in, w_out, hy_conv_w, hy_conv_b, hy_w1, hy_b1, hy_w2, hy_b2, hy_w3, hy_freq,
              hy_decay, hy_skip, mb_conv_w, mb_conv_b, mb_dt_bias, mb_a_log, mb_d, mb_norm_w, ml_conv_w,
              ml_conv_b, ml_gate_b, ml_norm_w, na_qnorm_w, na_knorm_w, na_rpb):
    L = x.shape[1]
    pos, t_norm = _hyena_positions(L)
    for l in range(DEPTH):
        h = _rmsnorm(x, norm_w[l])
        proj = h @ w_in[l]
        (hy_u, hy_g, mb_xbc, mb_z, mb_dt, ml_qk, ml_v, ml_o, ml_z, ml_gates, na_qkv, na_g) = _split_cols(proj, IN_SPLITS)
        filt = _hyena_filters(pos, t_norm, hy_w1[l], hy_b1[l], hy_w2[l], hy_b2[l], hy_w3[l], hy_freq[l], hy_decay[l])
        y_hy = _hyena_branch(hy_u, hy_g, hy_conv_w[l], hy_conv_b[l], filt, hy_skip[l])
        y_mb = _mamba2_branch(mb_xbc, mb_z, mb_dt, mb_conv_w[l], mb_conv_b[l], mb_dt_bias[l], mb_a_log[l],
                              mb_d[l], mb_norm_w[l])
        y_ml = _mlstm_branch(ml_qk, ml_v, ml_o, ml_z, ml_gates, ml_conv_w[l], ml_conv_b[l], ml_gate_b[l], ml_norm_w[l])
        y_na = _na_branch(na_qkv, na_g, na_qnorm_w[l], na_knorm_w[l], na_rpb[l])
        y = jnp.concatenate([y_hy, y_mb, y_ml, y_na], axis=-1)
        x = x + y @ w_out[l]
    return x
```

```python
import functools
import math

import jax
import jax.numpy as jnp
from jax import lax
from jax.experimental import pallas as pl
from jax.experimental.pallas import tpu as pltpu

RMS_EPS = 1e-6
GROUP_W = 512
TILE_N = 512
SMALL_W = 128
HALO = 16
N_CONV_TILES = 7
N_SILU_FROM = 3
VMEM_LIMIT = 56 * 1024 * 1024

T_HY_V, T_HY_X1, T_HY_X2 = 0, 1, 2
T_MB_X, T_MB_BC = 3, 4
T_ML_Q, T_ML_K = 5, 6
T_HY_G, T_MB_Z, T_ML_V, T_ML_O, T_ML_Z = 7, 8, 9, 10, 11
T_NA_Q, T_NA_K, T_NA_V, T_NA_G = 12, 13, 14, 15
N_TILES = 16

_HI = lax.Precision.HIGHEST


def _silu(x):
    return x * jax.nn.sigmoid(x)


def _inproj_kernel(x_ref, xp_ref, xn_ref, nw_ref, w_ref, ws_ref, taps_ref, cb_ref, cs_ref, gm_ref,
                   o_ref, os_ref, h_ref, acc_ref, *, tm):
    i = pl.program_id(1)
    j = pl.program_id(2)
    ni = pl.num_programs(1)

    def norm(xv):
        ms = jnp.mean(xv * xv, axis=-1, keepdims=True)
        return (xv * lax.rsqrt(ms + RMS_EPS) * nw_ref[...]).astype(jnp.bfloat16)

    @pl.when(j == 0)
    def _():
        h_ref[pl.ds(HALO, tm), :] = norm(x_ref[0])
        hp = norm(xp_ref[0])
        hn = norm(xn_ref[0])
        h_ref[pl.ds(0, HALO), :] = jnp.where(i == 0, jnp.zeros_like(hp), hp)
        h_ref[pl.ds(HALO + tm, HALO), :] = jnp.where(i == ni - 1, jnp.zeros_like(hn), hn)
        os_ref[0] = jnp.dot(h_ref[pl.ds(HALO, tm), :], ws_ref[...], preferred_element_type=jnp.float32)

    acc_ref[...] = jnp.dot(h_ref[...], w_ref[...], preferred_element_type=jnp.float32)

    def conv():
        t = taps_ref[...]
        return (acc_ref[pl.ds(HALO - 1, tm), :] * t[0:1] + acc_ref[pl.ds(HALO, tm), :] * t[1:2]
                + acc_ref[pl.ds(HALO + 1, tm), :] * t[2:3] + cb_ref[...])

    @pl.when(j < N_SILU_FROM)
    def _():
        o_ref[0] = conv().astype(o_ref.dtype)

    @pl.when((j >= N_SILU_FROM) & (j < N_CONV_TILES))
    def _():
        o_ref[0] = _silu(conv()).astype(o_ref.dtype)

    @pl.when((j >= N_CONV_TILES) & (j != T_NA_Q) & (j != T_NA_K))
    def _():
        o_ref[0] = acc_ref[pl.ds(HALO, tm), :].astype(o_ref.dtype)

    @pl.when((j == T_NA_Q) | (j == T_NA_K))
    def _():
        a = acc_ref[pl.ds(HALO, tm), :]
        ms = jnp.dot((a * a).astype(jnp.bfloat16), gm_ref[...], preferred_element_type=jnp.float32)
        o_ref[0] = (a * lax.rsqrt(ms + RMS_EPS) * cs_ref[...]).astype(o_ref.dtype)


def _inproj(x, nw, w, ws, taps, cbias, cscale, gmean):
    B, L, D = x.shape
    tm = min(1024, L)
    ni = L // tm
    hb = tm // HALO
    nlast = L // HALO - 1
    grid = (B, ni, N_TILES)
    return pl.pallas_call(
        functools.partial(_inproj_kernel, tm=tm),
        grid=grid,
        in_specs=[
            pl.BlockSpec((1, tm, D), lambda b, i, j: (b, i, 0)),
            pl.BlockSpec((1, HALO, D), lambda b, i, j: (b, jnp.maximum(i * hb - 1, 0), 0)),
            pl.BlockSpec((1, HALO, D), lambda b, i, j: (b, jnp.minimum((i + 1) * hb, nlast), 0)),
            pl.BlockSpec((1, D), lambda b, i, j: (0, 0)),
            pl.BlockSpec((D, TILE_N), lambda b, i, j: (0, j)),
            pl.BlockSpec((D, SMALL_W), lambda b, i, j: (0, 0)),
            pl.BlockSpec((3, TILE_N), lambda b, i, j: (0, j)),
            pl.BlockSpec((1, TILE_N), lambda b, i, j: (0, j)),
            pl.BlockSpec((1, TILE_N), lambda b, i, j: (0, j)),
            pl.BlockSpec((TILE_N, TILE_N), lambda b, i, j: (0, 0)),
        ],
        out_specs=[
            pl.BlockSpec((1, tm, TILE_N), lambda b, i, j: (b, i, j)),
            pl.BlockSpec((1, tm, SMALL_W), lambda b, i, j: (b, i, 0)),
        ],
        out_shape=[
            jax.ShapeDtypeStruct((B, L, N_TILES * TILE_N), jnp.bfloat16),
            jax.ShapeDtypeStruct((B, L, SMALL_W), jnp.float32),
        ],
        scratch_shapes=[
            pltpu.VMEM((tm + 2 * HALO, D), jnp.bfloat16),
            pltpu.VMEM((tm + 2 * HALO, TILE_N), jnp.float32),
        ],
        compiler_params=pltpu.CompilerParams(
            dimension_semantics=("arbitrary", "arbitrary", "arbitrary"), vmem_limit_bytes=VMEM_LIMIT),
        name="inproj",
    )(x, x, x, nw, w, ws, taps, cbias, cscale, gmean)


def _outproj_kernel(x_ref, yh_ref, g_ref, ym_ref, yl_ref, yn_ref, w_ref, o_ref):
    yh = (yh_ref[0].astype(jnp.float32) * _silu(g_ref[0].astype(jnp.float32))).astype(jnp.bfloat16)
    acc = jnp.dot(yh, w_ref[0], preferred_element_type=jnp.float32)
    acc += jnp.dot(ym_ref[0], w_ref[1], preferred_element_type=jnp.float32)
    acc += jnp.dot(yl_ref[0], w_ref[2], preferred_element_type=jnp.float32)
    acc += jnp.dot(yn_ref[0], w_ref[3], preferred_element_type=jnp.float32)
    o_ref[0] = x_ref[0] + acc


def _outproj(x, y_hy, proj, y_mb, y_ml, y_na, w_out):
    B, L, D = x.shape
    tm = min(512, L)
    tok = lambda b, i: (b, i, 0)
    return pl.pallas_call(
        _outproj_kernel,
        grid=(B, L // tm),
        in_specs=[
            pl.BlockSpec((1, tm, D), tok),
            pl.BlockSpec((1, tm, GROUP_W), tok),
            pl.BlockSpec((1, tm, GROUP_W), lambda b, i: (b, i, T_HY_G)),
            pl.BlockSpec((1, tm, GROUP_W), tok),
            pl.BlockSpec((1, tm, GROUP_W), tok),
            pl.BlockSpec((1, tm, GROUP_W), tok),
            pl.BlockSpec((4, GROUP_W, D), lambda b, i: (0, 0, 0)),
        ],
        out_specs=pl.BlockSpec((1, tm, D), tok),
        out_shape=jax.ShapeDtypeStruct((B, L, D), jnp.float32),
        compiler_params=pltpu.CompilerParams(
            dimension_semantics=("arbitrary", "arbitrary"), vmem_limit_bytes=VMEM_LIMIT),
        name="outproj",
    )(x, y_hy, proj, y_mb, y_ml, y_na, w_out)


CHUNK = 128
NEG = -1e30


def _dot_nt(a, b):
    return lax.dot_general(a, b, (((1,), (1,)), ((), ())), preferred_element_type=jnp.float32)


def _dot_tn(a, b):
    return lax.dot_general(a, b, (((0,), (0,)), ((), ())), preferred_element_type=jnp.float32)


def _chunk_masks(q, reverse):
    t = lax.broadcasted_iota(jnp.int32, (q, q), 0)
    s = lax.broadcasted_iota(jnp.int32, (q, q), 1)
    mask = (s >= t) if reverse else (s <= t)
    return mask, mask.astype(jnp.float32)


MB_H = 8
MB_P = 64
MB_N = 128
MB_GW = 256


def _ssd_kernel(*refs, reverse, finalize, dcol):
    if finalize:
        (xs_ref, bc_ref, sm_ref, e_ref, bias_ref, a_ref, yb_ref, z_ref, dsk_ref, nw_ref, o_ref, s_ref) = refs
    else:
        (xs_ref, bc_ref, sm_ref, e_ref, bias_ref, a_ref, o_ref, s_ref) = refs
    q = CHUNK

    @pl.when(pl.program_id(1) == 0)
    def _():
        s_ref[...] = jnp.zeros_like(s_ref)

    mask, tri = _chunk_masks(q, reverse)
    dt_s = jax.nn.softplus(sm_ref[0] + bias_ref[...])
    a_s = dt_s * a_ref[...]
    c_s = jnp.dot(tri, a_s, precision=_HI, preferred_element_type=jnp.float32)
    c_t = c_s.T
    c_full = jnp.dot(c_s, e_ref[...], precision=_HI, preferred_element_type=jnp.float32)
    dt_full = jnp.dot(dt_s, e_ref[...], precision=_HI, preferred_element_type=jnp.float32)
    far = 0 if reverse else q - 1
    tot_full = c_full[far:far + 1, :]
    x = xs_ref[0].astype(jnp.float32)
    dtx = dt_full * x
    lane = lax.broadcasted_iota(jnp.int32, (1, MB_GW), 1)
    ys = []
    for g in range(2):
        bg = bc_ref[0, :, g * MB_N:(g + 1) * MB_N]
        cg = bc_ref[0, :, MB_GW + g * MB_N:MB_GW + (g + 1) * MB_N]
        gs = slice(g * MB_GW, (g + 1) * MB_GW)
        gram = _dot_nt(cg, bg)
        dtx_g = dtx[:, gs]
        yg = jnp.zeros((q, MB_GW), jnp.float32)
        for j in range(4):
            col = dcol + g * 4 + j
            decay = jnp.exp(jnp.where(mask, c_s[:, col:col + 1] - c_t[col:col + 1, :], NEG))
            m = (gram * decay).astype(jnp.bfloat16)
            xm = jnp.where((lane >= j * MB_P) & (lane < (j + 1) * MB_P), dtx_g, 0.0).astype(jnp.bfloat16)
            yg = yg + jnp.dot(m, xm, preferred_element_type=jnp.float32)
        s_g = s_ref[:, gs]
        yg = yg + jnp.exp(c_full[:, gs]) * jnp.dot(cg, s_g.astype(jnp.bfloat16), preferred_element_type=jnp.float32)
        w = jnp.exp(tot_full[:, gs] - c_full[:, gs])
        s_ref[:, gs] = jnp.exp(tot_full[:, gs]) * s_g + _dot_tn(bg, (w * dtx_g).astype(jnp.bfloat16))
        ys.append(yg)
    y = jnp.concatenate(ys, axis=-1)
    if not finalize:
        o_ref[0] = y
        return
    y = y + yb_ref[0] + x * dsk_ref[...]
    y = y * _silu(z_ref[0].astype(jnp.float32))
    outs = []
    for g in range(2):
        yg = y[:, g * MB_GW:(g + 1) * MB_GW]
        ms = jnp.mean(yg * yg, axis=-1, keepdims=True)
        outs.append(yg * lax.rsqrt(ms + RMS_EPS))
    o_ref[0] = (jnp.concatenate(outs, axis=-1) * nw_ref[...]).astype(o_ref.dtype)


def _ssd(proj, small, e, bias, a_row, *, reverse, y_b=None, dskip=None, nw=None):
    B, L, _ = proj.shape
    nc = L // CHUNK
    finalize = y_b is not None
    if reverse:
        cm = lambda col: (lambda b, i: (b, nc - 1 - i, col))
    else:
        cm = lambda col: (lambda b, i: (b, i, col))
    const = lambda b, i: (0, 0)
    in_specs = [
        pl.BlockSpec((1, CHUNK, GROUP_W), cm(T_MB_X)),
        pl.BlockSpec((1, CHUNK, GROUP_W), cm(T_MB_BC)),
        pl.BlockSpec((1, CHUNK, SMALL_W), cm(0)),
        pl.BlockSpec((SMALL_W, GROUP_W), const),
        pl.BlockSpec((1, SMALL_W), const),
        pl.BlockSpec((1, SMALL_W), const),
    ]
    args = [proj, proj, small, e, bias, a_row]
    if finalize:
        in_specs += [
            pl.BlockSpec((1, CHUNK, GROUP_W), cm(0)),
            pl.BlockSpec((1, CHUNK, GROUP_W), cm(T_MB_Z)),
            pl.BlockSpec((1, GROUP_W), const),
            pl.BlockSpec((1, GROUP_W), const),
        ]
        args += [y_b, proj, dskip, nw]
    return pl.pallas_call(
        functools.partial(_ssd_kernel, reverse=reverse, finalize=finalize, dcol=8 if reverse else 0),
        grid=(B, nc),
        in_specs=in_specs,
        out_specs=pl.BlockSpec((1, CHUNK, GROUP_W), cm(0)),
        out_shape=jax.ShapeDtypeStruct((B, L, GROUP_W), jnp.bfloat16 if finalize else jnp.float32),
        scratch_shapes=[pltpu.VMEM((MB_N, GROUP_W), jnp.float32)],
        compiler_params=pltpu.CompilerParams(
            dimension_semantics=("arbitrary", "arbitrary"), vmem_limit_bytes=VMEM_LIMIT),
        name="ssd_bwd" if reverse else "ssd_fwd",
    )(*args)


ML_H = 4
ML_DH = 128
ML_GCOL = 16


def _mlstm_kernel(*refs, reverse, finalize, dcol):
    if finalize:
        (q_ref, k_ref, v_ref, sm_ref, gb_ref, hb_ref, og_ref, z_ref, nw_ref, o_ref, c_ref, n_ref, m_ref) = refs
    else:
        (q_ref, k_ref, v_ref, sm_ref, gb_ref, o_ref, c_ref, n_ref, m_ref) = refs
    q = CHUNK

    @pl.when(pl.program_id(1) == 0)
    def _():
        c_ref[...] = jnp.zeros_like(c_ref)
        n_ref[...] = jnp.zeros_like(n_ref)
        m_ref[...] = jnp.zeros_like(m_ref)

    mask, tri = _chunk_masks(q, reverse)
    graw = sm_ref[0] + gb_ref[...]
    lf = jax.nn.log_sigmoid(graw)
    b_s = jnp.dot(tri, lf, precision=_HI, preferred_element_type=jnp.float32)
    b_t = b_s.T
    g_t = graw.T
    far = 0 if reverse else q - 1
    hs = []
    for h in range(ML_H):
        ci = dcol + h
        cf = dcol + ML_H + h
        hsl = slice(h * ML_DH, (h + 1) * ML_DH)
        b_col = b_s[:, cf:cf + 1]
        i_col = graw[:, ci:ci + 1]
        dmat = jnp.where(mask, b_col - b_t[cf:cf + 1, :] + g_t[ci:ci + 1, :], NEG)
        m_prev = m_ref[h:h + 1, 0:1]
        m_inter = b_col + m_prev
        m_t = jnp.maximum(m_inter, jnp.max(dmat, axis=-1, keepdims=True))
        w_inter = jnp.exp(m_inter - m_t)
        p = jnp.exp(dmat - m_t)
        qh = q_ref[0, :, hsl]
        khf = k_ref[0, :, hsl].astype(jnp.float32) * (ML_DH ** -0.5)
        kh = khf.astype(jnp.bfloat16)
        vh = v_ref[0, :, hsl]
        s = _dot_nt(qh, kh) * p
        c_h = c_ref[h]
        n_h = n_ref[h:h + 1, :]
        num = w_inter * jnp.dot(qh, c_h.astype(jnp.bfloat16), preferred_element_type=jnp.float32)
        num = num + jnp.dot(s.astype(jnp.bfloat16), vh, preferred_element_type=jnp.float32)
        den = w_inter * jnp.sum(qh.astype(jnp.float32) * n_h, axis=-1, keepdims=True)
        den = den + jnp.sum(s, axis=-1, keepdims=True)
        hs.append(num / jnp.maximum(jnp.abs(den), jnp.exp(-m_t)))
        b_tot = b_col[far:far + 1, :]
        g_col = b_tot - b_col + i_col
        m_new = jnp.maximum(b_tot + m_prev, jnp.max(g_col, axis=0, keepdims=True))
        kw = khf * jnp.exp(g_col - m_new)
        decay = jnp.exp(b_tot + m_prev - m_new)
        c_ref[h] = decay * c_h + _dot_tn(kw.astype(jnp.bfloat16), vh)
        n_ref[h:h + 1, :] = decay * n_h + jnp.sum(kw, axis=0, keepdims=True)
        m_ref[h:h + 1, :] = jnp.broadcast_to(m_new, (1, ML_DH))
    hcat = jnp.concatenate(hs, axis=-1)
    if not finalize:
        o_ref[0] = hcat
        return
    hcat = (hcat + hb_ref[0]) * jax.nn.sigmoid(og_ref[0].astype(jnp.float32))
    outs = []
    for h in range(ML_H):
        hh = hcat[:, h * ML_DH:(h + 1) * ML_DH]
        ms = jnp.mean(hh * hh, axis=-1, keepdims=True)
        outs.append(hh * lax.rsqrt(ms + RMS_EPS))
    y = jnp.concatenate(outs, axis=-1) * nw_ref[...]
    o_ref[0] = (y * _silu(z_ref[0].astype(jnp.float32))).astype(o_ref.dtype)


def _mlstm(proj, small, gbias, *, reverse, h_b=None, nw=None):
    B, L, _ = proj.shape
    nc = L // CHUNK
    finalize = h_b is not None
    if reverse:
        cm = lambda col: (lambda b, i: (b, nc - 1 - i, col))
    else:
        cm = lambda col: (lambda b, i: (b, i, col))
    const = lambda b, i: (0, 0)
    in_specs = [
        pl.BlockSpec((1, CHUNK, GROUP_W), cm(T_ML_Q)),
        pl.BlockSpec((1, CHUNK, GROUP_W), cm(T_ML_K)),
        pl.BlockSpec((1, CHUNK, GROUP_W), cm(T_ML_V)),
        pl.BlockSpec((1, CHUNK, SMALL_W), cm(0)),
        pl.BlockSpec((1, SMALL_W), const),
    ]
    args = [proj, proj, proj, small, gbias]
    if finalize:
        in_specs += [
            pl.BlockSpec((1, CHUNK, GROUP_W), cm(0)),
            pl.BlockSpec((1, CHUNK, GROUP_W), cm(T_ML_O)),
            pl.BlockSpec((1, CHUNK, GROUP_W), cm(T_ML_Z)),
            pl.BlockSpec((1, GROUP_W), const),
        ]
        args += [h_b, proj, proj, nw]
    return pl.pallas_call(
        functools.partial(_mlstm_kernel, reverse=reverse, finalize=finalize,
                          dcol=ML_GCOL + (8 if reverse else 0)),
        grid=(B, nc),
        in_specs=in_specs,
        out_specs=pl.BlockSpec((1, CHUNK, GROUP_W), cm(0)),
        out_shape=jax.ShapeDtypeStruct((B, L, GROUP_W), jnp.bfloat16 if finalize else jnp.float32),
        scratch_shapes=[
            pltpu.VMEM((ML_H, ML_DH, ML_DH), jnp.float32),
            pltpu.VMEM((8, ML_DH), jnp.float32),
            pltpu.VMEM((8, ML_DH), jnp.float32),
        ],
        compiler_params=pltpu.CompilerParams(
            dimension_semantics=("arbitrary", "arbitrary"), vmem_limit_bytes=VMEM_LIMIT),
        name="mlstm_bwd" if reverse else "mlstm_fwd",
    )(*args)


GRID_W = 64
NA_KR = 8
NA_KC = 16
NA_H = 8
NA_DH = 64
ROWS_PER_BLOCK = 8
BLOCK_TOK = ROWS_PER_BLOCK * GRID_W


def _na_kernel(q_ref, kp_ref, kc_ref, kn_ref, vp_ref, vc_ref, vn_ref, g_ref, tbl_ref, o_ref, k_s, v_s, *, rows):
    mblk = pl.program_id(1)
    k_s[pl.ds(0, BLOCK_TOK), :] = kp_ref[0]
    k_s[pl.ds(BLOCK_TOK, BLOCK_TOK), :] = kc_ref[0]
    k_s[pl.ds(2 * BLOCK_TOK, BLOCK_TOK), :] = kn_ref[0]
    v_s[pl.ds(0, BLOCK_TOK), :] = vp_ref[0]
    v_s[pl.ds(BLOCK_TOK, BLOCK_TOK), :] = vc_ref[0]
    v_s[pl.ds(2 * BLOCK_TOK, BLOCK_TOK), :] = vn_ref[0]
    lane = lax.broadcasted_iota(jnp.int32, (GRID_W, 2 * NA_DH), 1)
    first = lane < NA_DH
    win = NA_KR * GRID_W
    for j in range(ROWS_PER_BLOCK):
        r = mblk * ROWS_PER_BLOCK + j
        rs = jnp.clip(r - NA_KR // 2, 0, rows - NA_KR)
        didx = r - rs
        off = pl.multiple_of((rs - (mblk - 1) * ROWS_PER_BLOCK) * GRID_W, GRID_W)
        outs = []
        for hp in range(NA_H // 2):
            ls = slice(hp * 2 * NA_DH, (hp + 1) * 2 * NA_DH)
            qp = q_ref[0, j * GRID_W:(j + 1) * GRID_W, ls]
            zero = jnp.zeros_like(qp)
            q2 = jnp.concatenate([jnp.where(first, qp, zero), jnp.where(first, zero, qp)], axis=0)
            kw = k_s[pl.ds(off, win), ls]
            vw = v_s[pl.ds(off, win), ls]
            s = _dot_nt(q2, kw) + tbl_ref[didx, hp].astype(jnp.float32)
            e = jnp.exp(s - jnp.max(s, axis=-1, keepdims=True))
            p = (e / jnp.sum(e, axis=-1, keepdims=True)).astype(jnp.bfloat16)
            o2 = jnp.dot(p, vw, preferred_element_type=jnp.float32)
            outs.append(jnp.where(first, o2[:GRID_W], o2[GRID_W:]))
        o = jnp.concatenate(outs, axis=-1)
        gate = _silu(g_ref[0, j * GRID_W:(j + 1) * GRID_W, :].astype(jnp.float32))
        o_ref[0, j * GRID_W:(j + 1) * GRID_W, :] = (o * gate).astype(o_ref.dtype)


def _na(proj, tbl):
    B, L, _ = proj.shape
    rows = L // GRID_W
    nb = rows // ROWS_PER_BLOCK
    prev = lambda col: (lambda b, m: (b, jnp.maximum(m - 1, 0), col))
    cur = lambda col: (lambda b, m: (b, m, col))
    nxt = lambda col: (lambda b, m: (b, jnp.minimum(m + 1, nb - 1), col))
    blk = (1, BLOCK_TOK, GROUP_W)
    return pl.pallas_call(
        functools.partial(_na_kernel, rows=rows),
        grid=(B, nb),
        in_specs=[
            pl.BlockSpec(blk, cur(T_NA_Q)),
            pl.BlockSpec(blk, prev(T_NA_K)), pl.BlockSpec(blk, cur(T_NA_K)), pl.BlockSpec(blk, nxt(T_NA_K)),
            pl.BlockSpec(blk, prev(T_NA_V)), pl.BlockSpec(blk, cur(T_NA_V)), pl.BlockSpec(blk, nxt(T_NA_V)),
            pl.BlockSpec(blk, cur(T_NA_G)),
            pl.BlockSpec(tbl.shape, lambda b, m: (0, 0, 0, 0)),
        ],
        out_specs=pl.BlockSpec(blk, cur(0)),
        out_shape=jax.ShapeDtypeStruct((B, L, GROUP_W), jnp.bfloat16),
        scratch_shapes=[
            pltpu.VMEM((3 * BLOCK_TOK, GROUP_W), jnp.bfloat16),
            pltpu.VMEM((3 * BLOCK_TOK, GROUP_W), jnp.bfloat16),
        ],
        compiler_params=pltpu.CompilerParams(
            dimension_semantics=("arbitrary", "arbitrary"), vmem_limit_bytes=VMEM_LIMIT),
        name="nbr_attn",
    )(proj, proj, proj, proj, proj, proj, proj, proj, tbl)


def _na_bias_table(rpb):
    w = jnp.arange(GRID_W)
    c = jnp.arange(GRID_W)
    cstart = jnp.clip(w - NA_KC // 2, 0, GRID_W - NA_KC)
    inwin = (c[None, :] >= cstart[:, None]) & (c[None, :] < cstart[:, None] + NA_KC)
    coff = jnp.clip(c[None, :] - w[:, None] + NA_KC - 1, 0, 2 * NA_KC - 2)
    d = jnp.arange(NA_KR)
    i = jnp.arange(NA_KR)
    roff = i[None, :] + (NA_KR - 1) - d[:, None]
    t = rpb[:, roff][:, :, :, coff]
    t = jnp.where(inwin[None, None, None], t, NEG)
    t = t.transpose(1, 0, 3, 2, 4).reshape(NA_KR, NA_H // 2, 2 * GRID_W, NA_KR * GRID_W)
    return t.astype(jnp.bfloat16)


HY_LANES = 128
F2N = 128
TW_B = 8
PAD_ROWS = 8
HY_POS_PAD = 128


def _cmul(ar, ai, br, bi):
    return ar * br - ai * bi, ar * bi + ai * br


def _stage_a_store(buf_ref, a, n2, tr, ti, n1f, rs):
    ar, ai = _cmul(a[:n1f], a[n1f:], tr, ti)
    base = pl.multiple_of(n2 * rs, 8)
    buf_ref[pl.ds(base, n1f), :] = ar
    buf_ref[pl.ds(base + n1f, n1f), :] = ai


def _stage_b_load(buf_ref, k1, n1f, rs):
    yr = buf_ref[pl.ds(k1, F2N, stride=rs), :]
    yi = buf_ref[pl.ds(n1f + k1, F2N, stride=rs), :]
    return jnp.concatenate([yr, yi], axis=0).astype(jnp.bfloat16)


def _hy_spec_kernel(ff_ref, fb_ref, mg_ref, tbr_ref, tbi_ref, f2_ref, o_ref, buf_ref, *, n1f, h1, rs, ns):
    s = pl.program_id(2)

    @pl.when(s < ns)
    def _():
        for b in range(TW_B):
            n2 = s * TW_B + b
            hf = ff_ref[pl.ds(n2, h1, stride=F2N), :]
            hb = fb_ref[pl.ds(F2N - n2, h1, stride=F2N), :]
            xs = jnp.concatenate([hf, hb], axis=0).astype(jnp.bfloat16)
            a = jnp.dot(mg_ref[0], xs, preferred_element_type=jnp.float32)
            _stage_a_store(buf_ref, a, n2, tbr_ref[b], tbi_ref[b], n1f, rs)

    @pl.when(s >= ns)
    def _():
        for kk in range(TW_B):
            k1 = (s - ns) * TW_B + kk
            o_ref[0, kk] = jnp.dot(f2_ref[0], _stage_b_load(buf_ref, k1, n1f, rs), preferred_element_type=jnp.float32)


def _hy_spectrum(filt, mg, tbr, tbi, f2, L):
    n1f = 2 * L // F2N
    h1 = n1f // 2
    rs = 2 * n1f + PAD_ROWS
    ns = F2N // TW_B
    nk = n1f // TW_B
    ncb = GROUP_W // HY_LANES
    a_idx = lambda o, c, s: (jnp.minimum(s, ns - 1), 0, 0)
    return pl.pallas_call(
        functools.partial(_hy_spec_kernel, n1f=n1f, h1=h1, rs=rs, ns=ns),
        grid=(2, ncb, ns + nk),
        in_specs=[
            pl.BlockSpec((L + F2N, HY_LANES), lambda o, c, s: (0, o * 2 * ncb + c)),
            pl.BlockSpec((L + F2N, HY_LANES), lambda o, c, s: (0, o * 2 * ncb + ncb + c)),
            pl.BlockSpec((1, 2 * n1f, n1f), a_idx),
            pl.BlockSpec(tbr.shape, lambda o, c, s: (0, 0, 0)),
            pl.BlockSpec(tbi.shape, lambda o, c, s: (0, 0, 0)),
            pl.BlockSpec((1, 2 * F2N, 2 * F2N), lambda o, c, s: (0, 0, 0)),
        ],
        out_specs=pl.BlockSpec((1, TW_B, 2 * F2N, HY_LANES), lambda o, c, s: (o, jnp.maximum(s - ns, 0), 0, c)),
        out_shape=jax.ShapeDtypeStruct((2, n1f, 2 * F2N, GROUP_W), jnp.float32),
        scratch_shapes=[pltpu.VMEM((F2N * rs, HY_LANES), jnp.float32)],
        compiler_params=pltpu.CompilerParams(
            dimension_semantics=("arbitrary", "arbitrary", "arbitrary"), vmem_limit_bytes=VMEM_LIMIT),
        name="hyena_spectrum",
    )(filt, filt, mg, tbr, tbi, f2)


def _hy_conv_kernel(v_ref, x1_ref, x2_ref, g_ref, m1_ref, m3_ref, tbr_ref, tbi_ref, f2_ref, skip_ref,
                    o_ref, buf_ref, z_ref, *, n1f, h1, rs, ns, nk):
    s = pl.program_id(2)
    p1 = ns
    p2 = p1 + nk
    p3 = p2 + ns
    p4 = p3 + nk

    def stage_a(xs, n2, b):
        a = jnp.dot(m1_ref[0], xs, preferred_element_type=jnp.float32)
        _stage_a_store(buf_ref, a, n2, tbr_ref[b], tbi_ref[b], n1f, rs)

    def stage_b(order, kb):
        for kk in range(TW_B):
            k1 = kb * TW_B + kk
            y = jnp.dot(f2_ref[0], _stage_b_load(buf_ref, k1, n1f, rs), preferred_element_type=jnp.float32)
            g = g_ref[0, kk]
            yr, yi = _cmul(y[:F2N], y[F2N:], g[:F2N], g[F2N:])
            ys = jnp.concatenate([yr, yi], axis=0).astype(jnp.bfloat16)
            z = jnp.dot(f2_ref[1], ys, preferred_element_type=jnp.float32)
            buf_ref[pl.ds(k1, F2N, stride=rs), :] = z[:F2N]
            buf_ref[pl.ds(n1f + k1, F2N, stride=rs), :] = z[F2N:]

    def stage_c(n2, b):
        base = pl.multiple_of(n2 * rs, 8)
        zr = buf_ref[pl.ds(base, n1f), :]
        zi = buf_ref[pl.ds(base + n1f, n1f), :]
        wr, wi = _cmul(zr, zi, tbr_ref[b], -tbi_ref[b])
        ws = jnp.concatenate([wr, wi], axis=0).astype(jnp.bfloat16)
        return jnp.dot(m3_ref[0], ws, preferred_element_type=jnp.float32)

    @pl.when(s < p1)
    def _():
        for b in range(TW_B):
            xs = jnp.concatenate([v_ref[0, b], v_ref[1, b]], axis=0)
            stage_a(xs, s * TW_B + b, b)

    @pl.when((s >= p1) & (s < p2))
    def _():
        stage_b(0, s - p1)

    @pl.when((s >= p2) & (s < p3))
    def _():
        for b in range(TW_B):
            n2 = (s - p2) * TW_B + b
            c = stage_c(n2, b)
            zs = []
            for r in range(2):
                vv = v_ref[r, b].astype(jnp.float32)
                z = x1_ref[r, b].astype(jnp.float32) * (c[r * h1:(r + 1) * h1] + skip_ref[0:1, :] * vv)
                z_ref[n2, r] = z
                zs.append(z)
            stage_a(jnp.concatenate(zs, axis=0).astype(jnp.bfloat16), n2, b)

    @pl.when((s >= p3) & (s < p4))
    def _():
        stage_b(1, s - p3)

    @pl.when(s >= p4)
    def _():
        for b in range(TW_B):
            n2 = (s - p4) * TW_B + b
            c = stage_c(n2, b)
            for r in range(2):
                y = x2_ref[r, b].astype(jnp.float32) * (c[r * h1:(r + 1) * h1] + skip_ref[1:2, :] * z_ref[n2, r])
                o_ref[r, b] = y.astype(o_ref.dtype)


def _hy_conv(hyp, spec, m1, m3, tbr, tbi, f2, skip, L):
    B = hyp.shape[0]
    n1f = 2 * L // F2N
    h1 = n1f // 2
    rs = 2 * n1f + PAD_ROWS
    ns = F2N // TW_B
    nk = n1f // TW_B
    ncb = GROUP_W // HY_LANES
    p1, p2, p3, p4 = ns, ns + nk, 2 * ns + nk, 2 * ns + 2 * nk
    clip = lambda v, hi: jnp.clip(v, 0, hi)
    tblk = (2, TW_B, h1, HY_LANES)
    v_idx = lambda c, p, s: (p, jnp.where(s < p1, s, clip(s - p2, ns - 1)), 0, c)
    x1_idx = lambda c, p, s: (p, clip(s - p2, ns - 1), 0, ncb + c)
    x2_idx = lambda c, p, s: (p, clip(s - p4, ns - 1), 0, 2 * ncb + c)
    g_idx = lambda c, p, s: (jnp.where(s < p3, 0, 1), jnp.where(s < p3, clip(s - p1, nk - 1), clip(s - p3, nk - 1)), 0, c)
    m1_idx = lambda c, p, s: (jnp.where(s < p1, s, clip(s - p2, ns - 1)), 0, 0)
    m3_idx = lambda c, p, s: (jnp.where(s < p4, clip(s - p2, ns - 1), s - p4), 0, 0)
    return pl.pallas_call(
        functools.partial(_hy_conv_kernel, n1f=n1f, h1=h1, rs=rs, ns=ns, nk=nk),
        grid=(ncb, B // 2, p4 + ns),
        in_specs=[
            pl.BlockSpec(tblk, v_idx),
            pl.BlockSpec(tblk, x1_idx),
            pl.BlockSpec(tblk, x2_idx),
            pl.BlockSpec((1, TW_B, 2 * F2N, HY_LANES), g_idx),
            pl.BlockSpec((1, 2 * n1f, 2 * h1), m1_idx),
            pl.BlockSpec((1, 2 * h1, 2 * n1f), m3_idx),
            pl.BlockSpec(tbr.shape, lambda c, p, s: (0, 0, 0)),
            pl.BlockSpec(tbi.shape, lambda c, p, s: (0, 0, 0)),
            pl.BlockSpec((2, 2 * F2N, 2 * F2N), lambda c, p, s: (0, 0, 0)),
            pl.BlockSpec((2, HY_LANES), lambda c, p, s: (0, c)),
        ],
        out_specs=pl.BlockSpec(tblk, lambda c, p, s: (p, clip(s - p4, ns - 1), 0, c)),
        out_shape=jax.ShapeDtypeStruct((B, F2N, h1, GROUP_W), jnp.bfloat16),
        scratch_shapes=[
            pltpu.VMEM((F2N * rs, HY_LANES), jnp.float32),
            pltpu.VMEM((F2N, 2, h1, HY_LANES), jnp.float32),
        ],
        compiler_params=pltpu.CompilerParams(
            dimension_semantics=("arbitrary", "arbitrary", "arbitrary"), vmem_limit_bytes=VMEM_LIMIT),
        name="hyena_conv",
    )(hyp, hyp, hyp, spec, m1, m3, tbr, tbi, f2, skip)


def _hy_filter_kernel(pos_ref, w1_ref, b1_ref, w2_ref, b2_ref, w3_ref, fr_ref, dec_ref, o_ref, *, nblk):
    i = pl.program_id(0)
    pos = pos_ref[...]
    fr = fr_ref[...]
    hid = jnp.sin(fr * (jnp.dot(pos, w1_ref[...], precision=_HI, preferred_element_type=jnp.float32) + b1_ref[...]))
    hid = jnp.sin(fr * (jnp.dot(hid, w2_ref[...], precision=_HI, preferred_element_type=jnp.float32) + b2_ref[...]))
    filt = jnp.dot(hid, w3_ref[...], precision=_HI, preferred_element_type=jnp.float32)
    filt = filt * jnp.exp(-pos[:, 0:1] * dec_ref[...])
    o_ref[...] = jnp.where(i < nblk, filt, 0.0)


def _hy_filters(pos, w1, b1, w2, b2, w3, freq, decay, L):
    nblk = L // F2N
    nh = w2.shape[0]
    no = w3.shape[1]
    const = lambda i: (0, 0)
    return pl.pallas_call(
        functools.partial(_hy_filter_kernel, nblk=nblk),
        grid=(nblk + 1,),
        in_specs=[
            pl.BlockSpec((F2N, HY_POS_PAD), lambda i: (jnp.minimum(i, nblk - 1), 0)),
            pl.BlockSpec((HY_POS_PAD, nh), const), pl.BlockSpec((1, nh), const),
            pl.BlockSpec((nh, nh), const), pl.BlockSpec((1, nh), const),
            pl.BlockSpec((nh, no), const), pl.BlockSpec((1, nh), const), pl.BlockSpec((1, no), const),
        ],
        out_specs=pl.BlockSpec((F2N, no), lambda i: (i, 0)),
        out_shape=jax.ShapeDtypeStruct((L + F2N, no), jnp.float32),
        compiler_params=pltpu.CompilerParams(dimension_semantics=("arbitrary",), vmem_limit_bytes=VMEM_LIMIT),
        name="hyena_filters",
    )(pos, w1, b1, w2, b2, w3, freq, decay)


def _hy_constants(L):
    n = 2 * L
    n1f = n // F2N
    h1 = n1f // 2
    na = F2N // TW_B

    def cis(num, den):
        ang = (-2.0 * math.pi / den) * (num % den).astype(jnp.float32)
        return jnp.cos(ang), jnp.sin(ang)

    k1 = jnp.arange(n1f)
    n1 = jnp.arange(n1f)
    a = jnp.arange(na)
    f1r, f1i = cis(k1[:, None] * n1[None, :], n1f)
    tar, tai = cis(k1[None, :] * (TW_B * a)[:, None], n)
    mr, mi = _cmul(f1r[None], f1i[None], tar[:, :, None], tai[:, :, None])

    def blockform(r, i):
        return jnp.concatenate([jnp.concatenate([r, -i], axis=-1), jnp.concatenate([i, r], axis=-1)], axis=-2)

    m1 = blockform(mr[:, :, :h1], mi[:, :, :h1]).astype(jnp.bfloat16)
    m3r = jnp.swapaxes(mr[:, :, :h1], 1, 2) / n
    m3i = -jnp.swapaxes(mi[:, :, :h1], 1, 2) / n
    m3 = blockform(m3r, m3i).astype(jnp.bfloat16)
    mgr = jnp.concatenate([mr[:, :, :h1], mr[:, :, h1:][:, :, ::-1]], axis=-1)
    mgi = jnp.concatenate([mi[:, :, :h1], mi[:, :, h1:][:, :, ::-1]], axis=-1)
    mg = jnp.concatenate([mgr, mgi], axis=1).astype(jnp.bfloat16)
    b = jnp.arange(TW_B)
    tbr, tbi = cis(k1[None, :] * b[:, None], n)
    tbr = jnp.broadcast_to(tbr[:, :, None], (TW_B, n1f, HY_LANES))
    tbi = jnp.broadcast_to(tbi[:, :, None], (TW_B, n1f, HY_LANES))
    k2 = jnp.arange(F2N)
    f2r, f2i = cis(k2[:, None] * k2[None, :], F2N)
    f2 = jnp.stack([blockform(f2r, f2i), blockform(f2r, -f2i)]).astype(jnp.bfloat16)
    return m1, m3, mg, tbr, tbi, f2


def _hy_positions(L):
    t = jnp.arange(L, dtype=jnp.float32)
    bands = jnp.arange(1, 9, dtype=jnp.float32)
    ang = (2.0 * math.pi / L) * t[:, None] * bands[None, :]
    pos = jnp.concatenate([(t / L)[:, None], jnp.cos(ang), jnp.sin(ang)], axis=-1)
    return jnp.pad(pos, ((0, 0), (0, HY_POS_PAD - pos.shape[1])))


def _col_perm():
    r = lambda a, b: list(range(a, b))
    main = (r(0, 1536) + r(2048, 3072) + r(3600, 4624) + r(1536, 2048) + r(3072, 3584)
            + r(4624, 6160) + r(6176, 8224))
    small = r(3584, 3600) + r(6160, 6176)
    return jnp.array(main, jnp.int32), jnp.array(small, jnp.int32)


def _pad_cols(a, width):
    return jnp.pad(a, [(0, 0)] * (a.ndim - 1) + [(0, width - a.shape[-1])])


def kernel(x, norm_w, w_in, w_out, hy_conv_w, hy_conv_b, hy_w1, hy_b1, hy_w2, hy_b2, hy_w3, hy_freq, hy_decay,
           hy_skip, mb_conv_w, mb_conv_b, mb_dt_bias, mb_a_log, mb_d, mb_norm_w, ml_conv_w, ml_conv_b, ml_gate_b,
           ml_norm_w, na_qnorm_w, na_knorm_w, na_rpb):
    B, L, D = x.shape
    depth = w_in.shape[0]
    ncols = N_TILES * TILE_N
    main_idx, small_idx = _col_perm()
    pos = _hy_positions(L)
    m1, m3, mg, tbr, tbi, f2 = _hy_constants(L)
    gmean = jnp.kron(jnp.eye(NA_H, dtype=jnp.float32), jnp.full((NA_DH, NA_DH), 1.0 / NA_DH)).astype(jnp.bfloat16)
    head_cols = jnp.arange(MB_H * MB_P) // MB_P
    f32 = jnp.float32
    for l in range(depth):
        w = w_in[l][:, main_idx].astype(jnp.bfloat16)
        ws = _pad_cols(w_in[l][:, small_idx], SMALL_W).astype(jnp.bfloat16)
        taps = _pad_cols(jnp.concatenate([hy_conv_w[l], mb_conv_w[l], ml_conv_w[l]], axis=-1), ncols)
        cbias = _pad_cols(jnp.concatenate([hy_conv_b[l], mb_conv_b[l], ml_conv_b[l]])[None], ncols)
        cscale = jnp.zeros((1, ncols), f32)
        cscale = cscale.at[0, T_NA_Q * TILE_N:(T_NA_Q + 1) * TILE_N].set(jnp.tile(na_qnorm_w[l], NA_H) * NA_DH ** -0.5)
        cscale = cscale.at[0, T_NA_K * TILE_N:(T_NA_K + 1) * TILE_N].set(jnp.tile(na_knorm_w[l], NA_H))
        proj, small = _inproj(x, norm_w[l][None], w, ws, taps, cbias, cscale, gmean)

        filt = _hy_filters(pos, _pad_cols(hy_w1[l].T, HY_POS_PAD).T, hy_b1[l][None], hy_w2[l], hy_b2[l][None],
                           hy_w3[l], hy_freq[l][None], hy_decay[l][None], L)
        spec = _hy_spectrum(filt, mg, tbr, tbi, f2[:1], L)
        hyp = proj[:, :, :3 * GROUP_W].reshape(B, L // F2N, F2N, 3 * GROUP_W).transpose(0, 2, 1, 3)
        y_hy = _hy_conv(hyp, spec, m1, m3, tbr, tbi, f2, hy_skip[l], L)
        y_hy = y_hy.transpose(0, 2, 1, 3).reshape(B, L, GROUP_W)

        a_neg = -jnp.exp(mb_a_log[l].astype(f32))
        y_b = None
        for d in (1, 0):
            e = jnp.zeros((SMALL_W, GROUP_W), f32).at[d * MB_H + head_cols, jnp.arange(GROUP_W)].set(1.0)
            bias = jnp.zeros((1, SMALL_W), f32).at[0, d * MB_H:(d + 1) * MB_H].set(mb_dt_bias[l, d])
            a_row = jnp.zeros((1, SMALL_W), f32).at[0, d * MB_H:(d + 1) * MB_H].set(a_neg[d])
            if d == 1:
                y_b = _ssd(proj, small, e, bias, a_row, reverse=True)
            else:
                y_mb = _ssd(proj, small, e, bias, a_row, reverse=False, y_b=y_b,
                            dskip=jnp.repeat(mb_d[l], MB_P)[None], nw=mb_norm_w[l][None])

        gbias = jnp.zeros((1, SMALL_W), f32).at[0, ML_GCOL:ML_GCOL + 4 * ML_H].set(ml_gate_b[l].reshape(-1))
        h_b = _mlstm(proj, small, gbias, reverse=True)
        y_ml = _mlstm(proj, small, gbias, reverse=False, h_b=h_b, nw=ml_norm_w[l][None])

        y_na = _na(proj, _na_bias_table(na_rpb[l]))

        x = _outproj(x, y_hy, proj, y_mb, y_ml, y_na, w_out[l].astype(jnp.bfloat16).reshape(4, GROUP_W, D))
    return x
```

```python
import functools
import math

import jax
import jax.numpy as jnp
from jax import lax
from jax.experimental import pallas as pl
from jax.experimental.pallas import tpu as pltpu

RMS_EPS = 1e-6
GROUP_W = 512
TILE_N = 512
SMALL_W = 128
HALO = 16
SUB_ROWS = 256
VMEM_LIMIT = 56 * 1024 * 1024

N_HY_TILES = 3
N_CONV_TILES = 7
N_TILES = 16
T_MB_X, T_MB_BC = 0, 1
T_ML_Q, T_ML_K = 2, 3
T_HY_G, T_MB_Z, T_ML_V, T_ML_O, T_ML_Z = 4, 5, 6, 7, 8
T_NA_Q, T_NA_K, T_NA_V, T_NA_G = 9, 10, 11, 12
N_MAIN_TILES = N_TILES - N_HY_TILES

_HI = lax.Precision.HIGHEST


def _silu(x):
    return x * jax.nn.sigmoid(x)


def _inproj_kernel(x_ref, xp_ref, xn_ref, nw_ref, w_ref, ws_ref, taps_ref, cb_ref, cs_ref, gm_ref,
                   o_ref, oh_ref, os_ref, h_ref, acc_ref, *, tm):
    i = pl.program_id(1)
    j = pl.program_id(2)
    ni = pl.num_programs(1)
    is_qk = (j == T_NA_Q + N_HY_TILES) | (j == T_NA_K + N_HY_TILES)

    def norm(xv):
        ms = jnp.mean(xv * xv, axis=-1, keepdims=True)
        return (xv * lax.rsqrt(ms + RMS_EPS) * nw_ref[...]).astype(jnp.bfloat16)

    @pl.when(j == 0)
    def _():
        h_ref[pl.ds(HALO, tm), :] = norm(x_ref[0])
        hp = norm(xp_ref[0])
        hn = norm(xn_ref[0])
        h_ref[pl.ds(0, HALO), :] = jnp.where(i == 0, jnp.zeros_like(hp), hp)
        h_ref[pl.ds(HALO + tm, HALO), :] = jnp.where(i == ni - 1, jnp.zeros_like(hn), hn)
        os_ref[0] = jnp.dot(h_ref[pl.ds(HALO, tm), :], ws_ref[...], preferred_element_type=jnp.float32)

    nsub = tm // SUB_ROWS

    def conv_tile(dst_ref, act):
        t = taps_ref[...]
        for c in range(nsub):
            lo = 0 if c == 0 else 2 * HALO + c * SUB_ROWS
            hi = 2 * HALO + (c + 1) * SUB_ROWS
            acc_ref[pl.ds(lo, hi - lo), :] = jnp.dot(h_ref[pl.ds(lo, hi - lo), :], w_ref[...],
                                                      preferred_element_type=jnp.float32)
            r0 = c * SUB_ROWS + HALO
            y = (acc_ref[pl.ds(r0 - 1, SUB_ROWS), :] * t[0:1] + acc_ref[pl.ds(r0, SUB_ROWS), :] * t[1:2]
                 + acc_ref[pl.ds(r0 + 1, SUB_ROWS), :] * t[2:3] + cb_ref[...])
            dst_ref[0, pl.ds(c * SUB_ROWS, SUB_ROWS), :] = act(y).astype(dst_ref.dtype)

    def plain_tile(post):
        for c in range(nsub):
            a = jnp.dot(h_ref[pl.ds(HALO + c * SUB_ROWS, SUB_ROWS), :], w_ref[...], preferred_element_type=jnp.float32)
            o_ref[0, pl.ds(c * SUB_ROWS, SUB_ROWS), :] = post(a).astype(o_ref.dtype)

    def qk_norm(a):
        ms = jnp.dot((a * a).astype(jnp.bfloat16), gm_ref[...], preferred_element_type=jnp.float32)
        return a * lax.rsqrt(ms + RMS_EPS) * cs_ref[...]

    @pl.when(j < N_HY_TILES)
    def _():
        conv_tile(oh_ref, lambda y: y)

    @pl.when((j >= N_HY_TILES) & (j < N_CONV_TILES))
    def _():
        conv_tile(o_ref, _silu)

    @pl.when((j >= N_CONV_TILES) & jnp.logical_not(is_qk))
    def _():
        plain_tile(lambda a: a)

    @pl.when(is_qk)
    def _():
        plain_tile(qk_norm)


def _inproj(x, nw, w, ws, taps, cbias, cscale, gmean):
    B, L, D = x.shape
    tm = min(1024, L)
    ni = L // tm
    hb = tm // HALO
    nlast = L // HALO - 1
    grid = (B, ni, N_TILES)
    main = lambda b, i, j: (b, i, jnp.maximum(j - N_HY_TILES, 0))
    return pl.pallas_call(
        functools.partial(_inproj_kernel, tm=tm),
        grid=grid,
        in_specs=[
            pl.BlockSpec((1, tm, D), lambda b, i, j: (b, i, 0)),
            pl.BlockSpec((1, HALO, D), lambda b, i, j: (b, jnp.maximum(i * hb - 1, 0), 0)),
            pl.BlockSpec((1, HALO, D), lambda b, i, j: (b, jnp.minimum((i + 1) * hb, nlast), 0)),
            pl.BlockSpec((1, D), lambda b, i, j: (0, 0)),
            pl.BlockSpec((D, TILE_N), lambda b, i, j: (0, j)),
            pl.BlockSpec((D, SMALL_W), lambda b, i, j: (0, 0)),
            pl.BlockSpec((3, TILE_N), lambda b, i, j: (0, j)),
            pl.BlockSpec((1, TILE_N), lambda b, i, j: (0, j)),
            pl.BlockSpec((1, TILE_N), lambda b, i, j: (0, j)),
            pl.BlockSpec((TILE_N, TILE_N), lambda b, i, j: (0, 0)),
        ],
        out_specs=[
            pl.BlockSpec((1, tm, TILE_N), main),
            pl.BlockSpec((1, tm, TILE_N), lambda b, i, j: (b, i, jnp.minimum(j, N_HY_TILES - 1))),
            pl.BlockSpec((1, tm, SMALL_W), lambda b, i, j: (b, i, 0)),
        ],
        out_shape=[
            jax.ShapeDtypeStruct((B, L, N_MAIN_TILES * TILE_N), jnp.bfloat16),
            jax.ShapeDtypeStruct((B, L, N_HY_TILES * TILE_N), jnp.bfloat16),
            jax.ShapeDtypeStruct((B, L, SMALL_W), jnp.float32),
        ],
        scratch_shapes=[
            pltpu.VMEM((tm + 2 * HALO, D), jnp.bfloat16),
            pltpu.VMEM((tm + 2 * HALO, TILE_N), jnp.float32),
        ],
        compiler_params=pltpu.CompilerParams(
            dimension_semantics=("arbitrary", "arbitrary", "arbitrary"), vmem_limit_bytes=VMEM_LIMIT),
        name="inproj",
    )(x, x, x, nw, w, ws, taps, cbias, cscale, gmean)


def _outproj_kernel(x_ref, yh_ref, g_ref, ym_ref, yl_ref, yn_ref, w_ref, o_ref):
    yh = (yh_ref[0].astype(jnp.float32) * _silu(g_ref[0].astype(jnp.float32))).astype(jnp.bfloat16)
    acc = jnp.dot(yh, w_ref[0], preferred_element_type=jnp.float32)
    acc += jnp.dot(ym_ref[0], w_ref[1], preferred_element_type=jnp.float32)
    acc += jnp.dot(yl_ref[0], w_ref[2], preferred_element_type=jnp.float32)
    acc += jnp.dot(yn_ref[0], w_ref[3], preferred_element_type=jnp.float32)
    o_ref[0] = x_ref[0] + acc


def _outproj(x, y_hy, proj, y_mb, y_ml, y_na, w_out):
    B, L, D = x.shape
    tm = min(512, L)
    tok = lambda b, i: (b, i, 0)
    return pl.pallas_call(
        _outproj_kernel,
        grid=(B, L // tm),
        in_specs=[
            pl.BlockSpec((1, tm, D), tok),
            pl.BlockSpec((1, tm, GROUP_W), tok),
            pl.BlockSpec((1, tm, GROUP_W), lambda b, i: (b, i, T_HY_G)),
            pl.BlockSpec((1, tm, GROUP_W), tok),
            pl.BlockSpec((1, tm, GROUP_W), tok),
            pl.BlockSpec((1, tm, GROUP_W), tok),
            pl.BlockSpec((4, GROUP_W, D), lambda b, i: (0, 0, 0)),
        ],
        out_specs=pl.BlockSpec((1, tm, D), tok),
        out_shape=jax.ShapeDtypeStruct((B, L, D), jnp.float32),
        compiler_params=pltpu.CompilerParams(
            dimension_semantics=("arbitrary", "arbitrary"), vmem_limit_bytes=VMEM_LIMIT),
        name="outproj",
    )(x, y_hy, proj, y_mb, y_ml, y_na, w_out)


CHUNK = 128
NEG = -1e30


def _dot_nt(a, b):
    return lax.dot_general(a, b, (((1,), (1,)), ((), ())), preferred_element_type=jnp.float32)


def _dot_tn(a, b):
    return lax.dot_general(a, b, (((0,), (0,)), ((), ())), preferred_element_type=jnp.float32)


def _chunk_masks(q, reverse):
    t = lax.broadcasted_iota(jnp.int32, (q, q), 0)
    s = lax.broadcasted_iota(jnp.int32, (q, q), 1)
    mask = (s >= t) if reverse else (s <= t)
    return mask, mask.astype(jnp.float32)


MB_H = 8
MB_P = 64
MB_N = 128
MB_GW = 256


def _ssd_kernel(*refs, reverse, finalize, dcol):
    if finalize:
        (xs_ref, bc_ref, sm_ref, e_ref, bias_ref, a_ref, yb_ref, z_ref, dsk_ref, nw_ref, o_ref, s_ref) = refs
    else:
        (xs_ref, bc_ref, sm_ref, e_ref, bias_ref, a_ref, o_ref, s_ref) = refs
    q = CHUNK

    @pl.when(pl.program_id(1) == 0)
    def _():
        s_ref[...] = jnp.zeros_like(s_ref)

    mask, tri = _chunk_masks(q, reverse)
    dt_s = jax.nn.softplus(sm_ref[0] + bias_ref[...])
    a_s = dt_s * a_ref[...]
    c_s = jnp.dot(tri, a_s, precision=_HI, preferred_element_type=jnp.float32)
    c_t = c_s.T
    c_full = jnp.dot(c_s, e_ref[...], precision=_HI, preferred_element_type=jnp.float32)
    dt_full = jnp.dot(dt_s, e_ref[...], precision=_HI, preferred_element_type=jnp.float32)
    far = 0 if reverse else q - 1
    tot_full = c_full[far:far + 1, :]
    x = xs_ref[0].astype(jnp.float32)
    dtx = dt_full * x
    lane = lax.broadcasted_iota(jnp.int32, (1, MB_GW), 1)
    ys = []
    for g in range(2):
        bg = bc_ref[0, :, g * MB_N:(g + 1) * MB_N]
        cg = bc_ref[0, :, MB_GW + g * MB_N:MB_GW + (g + 1) * MB_N]
        gs = slice(g * MB_GW, (g + 1) * MB_GW)
        gram = _dot_nt(cg, bg)
        dtx_g = dtx[:, gs]
        yg = jnp.zeros((q, MB_GW), jnp.float32)
        for j in range(4):
            col = dcol + g * 4 + j
            decay = jnp.exp(jnp.where(mask, c_s[:, col:col + 1] - c_t[col:col + 1, :], NEG))
            m = (gram * decay).astype(jnp.bfloat16)
            xm = jnp.where((lane >= j * MB_P) & (lane < (j + 1) * MB_P), dtx_g, 0.0).astype(jnp.bfloat16)
            yg = yg + jnp.dot(m, xm, preferred_element_type=jnp.float32)
        s_g = s_ref[:, gs]
        yg = yg + jnp.exp(c_full[:, gs]) * jnp.dot(cg, s_g.astype(jnp.bfloat16), preferred_element_type=jnp.float32)
        w = jnp.exp(tot_full[:, gs] - c_full[:, gs])
        s_ref[:, gs] = jnp.exp(tot_full[:, gs]) * s_g + _dot_tn(bg, (w * dtx_g).astype(jnp.bfloat16))
        ys.append(yg)
    y = jnp.concatenate(ys, axis=-1)
    if not finalize:
        o_ref[0] = y
        return
    y = y + yb_ref[0] + x * dsk_ref[...]
    y = y * _silu(z_ref[0].astype(jnp.float32))
    outs = []
    for g in range(2):
        yg = y[:, g * MB_GW:(g + 1) * MB_GW]
        ms = jnp.mean(yg * yg, axis=-1, keepdims=True)
        outs.append(yg * lax.rsqrt(ms + RMS_EPS))
    o_ref[0] = (jnp.concatenate(outs, axis=-1) * nw_ref[...]).astype(o_ref.dtype)


def _ssd(proj, small, e, bias, a_row, *, reverse, y_b=None, dskip=None, nw=None):
    B, L, _ = proj.shape
    nc = L // CHUNK
    finalize = y_b is not None
    if reverse:
        cm = lambda col: (lambda b, i: (b, nc - 1 - i, col))
    else:
        cm = lambda col: (lambda b, i: (b, i, col))
    const = lambda b, i: (0, 0)
    in_specs = [
        pl.BlockSpec((1, CHUNK, GROUP_W), cm(T_MB_X)),
        pl.BlockSpec((1, CHUNK, GROUP_W), cm(T_MB_BC)),
        pl.BlockSpec((1, CHUNK, SMALL_W), cm(0)),
        pl.BlockSpec((SMALL_W, GROUP_W), const),
        pl.BlockSpec((1, SMALL_W), const),
        pl.BlockSpec((1, SMALL_W), const),
    ]
    args = [proj, proj, small, e, bias, a_row]
    if finalize:
        in_specs += [
            pl.BlockSpec((1, CHUNK, GROUP_W), cm(0)),
            pl.BlockSpec((1, CHUNK, GROUP_W), cm(T_MB_Z)),
            pl.BlockSpec((1, GROUP_W), const),
            pl.BlockSpec((1, GROUP_W), const),
        ]
        args += [y_b, proj, dskip, nw]
    return pl.pallas_call(
        functools.partial(_ssd_kernel, reverse=reverse, finalize=finalize, dcol=8 if reverse else 0),
        grid=(B, nc),
        in_specs=in_specs,
        out_specs=pl.BlockSpec((1, CHUNK, GROUP_W), cm(0)),
        out_shape=jax.ShapeDtypeStruct((B, L, GROUP_W), jnp.bfloat16 if finalize else jnp.float32),
        scratch_shapes=[pltpu.VMEM((MB_N, GROUP_W), jnp.float32)],
        compiler_params=pltpu.CompilerParams(
            dimension_semantics=("arbitrary", "arbitrary"), vmem_limit_bytes=VMEM_LIMIT),
        name="ssd_bwd" if reverse else "ssd_fwd",
    )(*args)


ML_H = 4
ML_DH = 128
ML_GCOL = 16


def _mlstm_kernel(*refs, reverse, finalize, dcol):
    if finalize:
        (q_ref, k_ref, v_ref, sm_ref, gb_ref, hb_ref, og_ref, z_ref, nw_ref, o_ref, c_ref, n_ref, m_ref) = refs
    else:
        (q_ref, k_ref, v_ref, sm_ref, gb_ref, o_ref, c_ref, n_ref, m_ref) = refs
    q = CHUNK

    @pl.when(pl.program_id(1) == 0)
    def _():
        c_ref[...] = jnp.zeros_like(c_ref)
        n_ref[...] = jnp.zeros_like(n_ref)
        m_ref[...] = jnp.zeros_like(m_ref)

    mask, tri = _chunk_masks(q, reverse)
    graw = sm_ref[0] + gb_ref[...]
    lf = jax.nn.log_sigmoid(graw)
    b_s = jnp.dot(tri, lf, precision=_HI, preferred_element_type=jnp.float32)
    b_t = b_s.T
    g_t = graw.T
    far = 0 if reverse else q - 1
    hs = []
    for h in range(ML_H):
        ci = dcol + h
        cf = dcol + ML_H + h
        hsl = slice(h * ML_DH, (h + 1) * ML_DH)
        b_col = b_s[:, cf:cf + 1]
        i_col = graw[:, ci:ci + 1]
        dmat = jnp.where(mask, b_col - b_t[cf:cf + 1, :] + g_t[ci:ci + 1, :], NEG)
        m_prev = m_ref[h:h + 1, 0:1]
        m_inter = b_col + m_prev
        m_t = jnp.maximum(m_inter, jnp.max(dmat, axis=-1, keepdims=True))
        w_inter = jnp.exp(m_inter - m_t)
        p = jnp.exp(dmat - m_t)
        qh = q_ref[0, :, hsl]
        khf = k_ref[0, :, hsl].astype(jnp.float32) * (ML_DH ** -0.5)
        kh = khf.astype(jnp.bfloat16)
        vh = v_ref[0, :, hsl]
        s = _dot_nt(qh, kh) * p
        c_h = c_ref[h]
        n_h = n_ref[h:h + 1, :]
        num = w_inter * jnp.dot(qh, c_h.astype(jnp.bfloat16), preferred_element_type=jnp.float32)
        num = num + jnp.dot(s.astype(jnp.bfloat16), vh, preferred_element_type=jnp.float32)
        den = w_inter * jnp.sum(qh.astype(jnp.float32) * n_h, axis=-1, keepdims=True)
        den = den + jnp.sum(s, axis=-1, keepdims=True)
        hs.append(num / jnp.maximum(jnp.abs(den), jnp.exp(-m_t)))
        b_tot = b_col[far:far + 1, :]
        g_col = b_tot - b_col + i_col
        m_new = jnp.maximum(b_tot + m_prev, jnp.max(g_col, axis=0, keepdims=True))
        kw = khf * jnp.exp(g_col - m_new)
        decay = jnp.exp(b_tot + m_prev - m_new)
        c_ref[h] = decay * c_h + _dot_tn(kw.astype(jnp.bfloat16), vh)
        n_ref[h:h + 1, :] = decay * n_h + jnp.sum(kw, axis=0, keepdims=True)
        m_ref[h:h + 1, :] = jnp.broadcast_to(m_new, (1, ML_DH))
    hcat = jnp.concatenate(hs, axis=-1)
    if not finalize:
        o_ref[0] = hcat
        return
    hcat = (hcat + hb_ref[0]) * jax.nn.sigmoid(og_ref[0].astype(jnp.float32))
    outs = []
    for h in range(ML_H):
        hh = hcat[:, h * ML_DH:(h + 1) * ML_DH]
        ms = jnp.mean(hh * hh, axis=-1, keepdims=True)
        outs.append(hh * lax.rsqrt(ms + RMS_EPS))
    y = jnp.concatenate(outs, axis=-1) * nw_ref[...]
    o_ref[0] = (y * _silu(z_ref[0].astype(jnp.float32))).astype(o_ref.dtype)


def _mlstm(proj, small, gbias, *, reverse, h_b=None, nw=None):
    B, L, _ = proj.shape
    nc = L // CHUNK
    finalize = h_b is not None
    if reverse:
        cm = lambda col: (lambda b, i: (b, nc - 1 - i, col))
    else:
        cm = lambda col: (lambda b, i: (b, i, col))
    const = lambda b, i: (0, 0)
    in_specs = [
        pl.BlockSpec((1, CHUNK, GROUP_W), cm(T_ML_Q)),
        pl.BlockSpec((1, CHUNK, GROUP_W), cm(T_ML_K)),
        pl.BlockSpec((1, CHUNK, GROUP_W), cm(T_ML_V)),
        pl.BlockSpec((1, CHUNK, SMALL_W), cm(0)),
        pl.BlockSpec((1, SMALL_W), const),
    ]
    args = [proj, proj, proj, small, gbias]
    if finalize:
        in_specs += [
            pl.BlockSpec((1, CHUNK, GROUP_W), cm(0)),
            pl.BlockSpec((1, CHUNK, GROUP_W), cm(T_ML_O)),
            pl.BlockSpec((1, CHUNK, GROUP_W), cm(T_ML_Z)),
            pl.BlockSpec((1, GROUP_W), const),
        ]
        args += [h_b, proj, proj, nw]
    return pl.pallas_call(
        functools.partial(_mlstm_kernel, reverse=reverse, finalize=finalize,
                          dcol=ML_GCOL + (8 if reverse else 0)),
        grid=(B, nc),
        in_specs=in_specs,
        out_specs=pl.BlockSpec((1, CHUNK, GROUP_W), cm(0)),
        out_shape=jax.ShapeDtypeStruct((B, L, GROUP_W), jnp.bfloat16 if finalize else jnp.float32),
        scratch_shapes=[
            pltpu.VMEM((ML_H, ML_DH, ML_DH), jnp.float32),
            pltpu.VMEM((8, ML_DH), jnp.float32),
            pltpu.VMEM((8, ML_DH), jnp.float32),
        ],
        compiler_params=pltpu.CompilerParams(
            dimension_semantics=("arbitrary", "arbitrary"), vmem_limit_bytes=VMEM_LIMIT),
        name="mlstm_bwd" if reverse else "mlstm_fwd",
    )(*args)


GRID_W = 64
NA_KR = 8
NA_KC = 16
NA_H = 8
NA_DH = 64
ROWS_PER_BLOCK = 8
BLOCK_TOK = ROWS_PER_BLOCK * GRID_W


def _na_kernel(q_ref, kp_ref, kc_ref, kn_ref, vp_ref, vc_ref, vn_ref, g_ref, tbl_ref, o_ref, k_s, v_s, *, rows):
    mblk = pl.program_id(1)
    npair = NA_H // 2
    pw = 2 * NA_DH
    ones = jnp.ones((BLOCK_TOK, pw), jnp.bfloat16)
    for t, (kr, vr) in enumerate(((kp_ref, vp_ref), (kc_ref, vc_ref), (kn_ref, vn_ref))):
        k_s[pl.ds(t * BLOCK_TOK, BLOCK_TOK), :] = kr[0]
        for hp in range(npair):
            v_s[pl.ds(t * BLOCK_TOK, BLOCK_TOK), 2 * hp * pw:(2 * hp + 1) * pw] = vr[0, :, hp * pw:(hp + 1) * pw]
            v_s[pl.ds(t * BLOCK_TOK, BLOCK_TOK), (2 * hp + 1) * pw:(2 * hp + 2) * pw] = ones
    lane = lax.broadcasted_iota(jnp.int32, (GRID_W, pw), 1)
    first = lane < NA_DH
    win = NA_KR * GRID_W
    for j in range(ROWS_PER_BLOCK):
        r = mblk * ROWS_PER_BLOCK + j
        rs = jnp.clip(r - NA_KR // 2, 0, rows - NA_KR)
        didx = r - rs
        off = pl.multiple_of((rs - (mblk - 1) * ROWS_PER_BLOCK) * GRID_W, GRID_W)
        ss = []
        for hp in range(npair):
            ls = slice(hp * pw, (hp + 1) * pw)
            qp = q_ref[0, j * GRID_W:(j + 1) * GRID_W, ls]
            zero = jnp.zeros_like(qp)
            q2 = jnp.concatenate([jnp.where(first, qp, zero), jnp.where(first, zero, qp)], axis=0)
            ss.append(_dot_nt(q2, k_s[pl.ds(off, win), ls]))
        s = jnp.concatenate(ss, axis=0) + tbl_ref[didx].astype(jnp.float32)
        e = jnp.exp2(s - jnp.max(s, axis=-1, keepdims=True)).astype(jnp.bfloat16)
        outs = []
        for hp in range(npair):
            ov = jnp.dot(e[hp * pw:(hp + 1) * pw], v_s[pl.ds(off, win), 2 * hp * pw:(2 * hp + 2) * pw],
                         preferred_element_type=jnp.float32)
            o2 = ov[:, :pw] / ov[:, pw:]
            outs.append(jnp.where(first, o2[:GRID_W], o2[GRID_W:]))
        o = jnp.concatenate(outs, axis=-1)
        gate = _silu(g_ref[0, j * GRID_W:(j + 1) * GRID_W, :].astype(jnp.float32))
        o_ref[0, j * GRID_W:(j + 1) * GRID_W, :] = (o * gate).astype(o_ref.dtype)


def _na(proj, tbl):
    B, L, _ = proj.shape
    rows = L // GRID_W
    nb = rows // ROWS_PER_BLOCK
    prev = lambda col: (lambda b, m: (b, jnp.maximum(m - 1, 0), col))
    cur = lambda col: (lambda b, m: (b, m, col))
    nxt = lambda col: (lambda b, m: (b, jnp.minimum(m + 1, nb - 1), col))
    blk = (1, BLOCK_TOK, GROUP_W)
    return pl.pallas_call(
        functools.partial(_na_kernel, rows=rows),
        grid=(B, nb),
        in_specs=[
            pl.BlockSpec(blk, cur(T_NA_Q)),
            pl.BlockSpec(blk, prev(T_NA_K)), pl.BlockSpec(blk, cur(T_NA_K)), pl.BlockSpec(blk, nxt(T_NA_K)),
            pl.BlockSpec(blk, prev(T_NA_V)), pl.BlockSpec(blk, cur(T_NA_V)), pl.BlockSpec(blk, nxt(T_NA_V)),
            pl.BlockSpec(blk, cur(T_NA_G)),
            pl.BlockSpec(tbl.shape, lambda b, m: (0, 0, 0)),
        ],
        out_specs=pl.BlockSpec(blk, cur(0)),
        out_shape=jax.ShapeDtypeStruct((B, L, GROUP_W), jnp.bfloat16),
        scratch_shapes=[
            pltpu.VMEM((3 * BLOCK_TOK, GROUP_W), jnp.bfloat16),
            pltpu.VMEM((3 * BLOCK_TOK, 2 * GROUP_W), jnp.bfloat16),
        ],
        compiler_params=pltpu.CompilerParams(
            dimension_semantics=("arbitrary", "arbitrary"), vmem_limit_bytes=VMEM_LIMIT),
        name="nbr_attn",
    )(proj, proj, proj, proj, proj, proj, proj, proj, tbl)


def _na_bias_table(rpb):
    nco = 2 * NA_KC - 1
    rows_d = jnp.stack([rpb[:, NA_KR - 1 - d:2 * NA_KR - 1 - d, :] for d in range(NA_KR)])
    lpad = GRID_W - NA_KC
    ext = jnp.pad(rows_d * math.log2(math.e), ((0, 0), (0, 0), (0, 0), (lpad, 2 * GRID_W - lpad - nco)))
    lead = ext.shape[:3]
    skew = jnp.broadcast_to(ext[..., None, :], lead + (GRID_W, 2 * GRID_W)).reshape(lead + (2 * GRID_W * GRID_W,))
    skew = skew[..., :GRID_W * (2 * GRID_W - 1)].reshape(lead + (GRID_W, 2 * GRID_W - 1))[..., GRID_W - 1:]
    w = jnp.arange(GRID_W)
    c = jnp.arange(GRID_W)
    cstart = jnp.clip(w - NA_KC // 2, 0, GRID_W - NA_KC)
    inwin = (c[None, :] >= cstart[:, None]) & (c[None, :] < cstart[:, None] + NA_KC)
    t = jnp.where(inwin, skew, NEG)
    t = t.transpose(0, 1, 3, 2, 4).reshape(NA_KR, NA_H * GRID_W, NA_KR * GRID_W)
    return t.astype(jnp.bfloat16)


HY_LANES = 128
F2N = 128
TW_B = 8
PAD_ROWS = 8
HY_POS_PAD = 128


def _cmul(ar, ai, br, bi):
    return ar * br - ai * bi, ar * bi + ai * br


def _stage_a_store(buf_ref, a, n2, tr, ti, n1f, rs):
    ar, ai = _cmul(a[:n1f], a[n1f:], tr, ti)
    base = pl.multiple_of(n2 * rs, 8)
    buf_ref[pl.ds(base, n1f), :] = ar
    buf_ref[pl.ds(base + n1f, n1f), :] = ai


def _stage_b_load(buf_ref, k1, n1f, rs):
    yr = buf_ref[pl.ds(k1, F2N, stride=rs), :]
    yi = buf_ref[pl.ds(n1f + k1, F2N, stride=rs), :]
    return jnp.concatenate([yr, yi], axis=0).astype(jnp.bfloat16)


def _hy_spec_kernel(ff_ref, fb_ref, mg_ref, tbr_ref, tbi_ref, f2_ref, o_ref, buf_ref, *, n1f, h1, rs, ns):
    s = pl.program_id(2)

    @pl.when(s < ns)
    def _():
        for b in range(TW_B):
            n2 = s * TW_B + b
            hf = ff_ref[pl.ds(n2, h1, stride=F2N), :]
            hb = fb_ref[pl.ds(F2N - n2, h1, stride=F2N), :]
            xs = jnp.concatenate([hf, hb], axis=0).astype(jnp.bfloat16)
            a = jnp.dot(mg_ref[0], xs, preferred_element_type=jnp.float32)
            _stage_a_store(buf_ref, a, n2, tbr_ref[b], tbi_ref[b], n1f, rs)

    @pl.when(s >= ns)
    def _():
        for kk in range(TW_B):
            k1 = (s - ns) * TW_B + kk
            o_ref[0, kk] = jnp.dot(f2_ref[0], _stage_b_load(buf_ref, k1, n1f, rs), preferred_element_type=jnp.float32)


def _hy_spectrum(filt, mg, tbr, tbi, f2, L):
    n1f = 2 * L // F2N
    h1 = n1f // 2
    rs = 2 * n1f + PAD_ROWS
    ns = F2N // TW_B
    nk = n1f // TW_B
    ncb = GROUP_W // HY_LANES
    a_idx = lambda o, c, s: (jnp.minimum(s, ns - 1), 0, 0)
    return pl.pallas_call(
        functools.partial(_hy_spec_kernel, n1f=n1f, h1=h1, rs=rs, ns=ns),
        grid=(2, ncb, ns + nk),
        in_specs=[
            pl.BlockSpec((L + F2N, HY_LANES), lambda o, c, s: (0, o * 2 * ncb + c)),
            pl.BlockSpec((L + F2N, HY_LANES), lambda o, c, s: (0, o * 2 * ncb + ncb + c)),
            pl.BlockSpec((1, 2 * n1f, n1f), a_idx),
            pl.BlockSpec(tbr.shape, lambda o, c, s: (0, 0, 0)),
            pl.BlockSpec(tbi.shape, lambda o, c, s: (0, 0, 0)),
            pl.BlockSpec((1, 2 * F2N, 2 * F2N), lambda o, c, s: (0, 0, 0)),
        ],
        out_specs=pl.BlockSpec((1, TW_B, 2 * F2N, HY_LANES), lambda o, c, s: (o, jnp.maximum(s - ns, 0), 0, c)),
        out_shape=jax.ShapeDtypeStruct((2, n1f, 2 * F2N, GROUP_W), jnp.float32),
        scratch_shapes=[pltpu.VMEM((F2N * rs, HY_LANES), jnp.float32)],
        compiler_params=pltpu.CompilerParams(
            dimension_semantics=("arbitrary", "arbitrary", "arbitrary"), vmem_limit_bytes=VMEM_LIMIT),
        name="hyena_spectrum",
    )(filt, filt, mg, tbr, tbi, f2)


def _hy_conv_kernel(v_ref, x1_ref, x2_ref, g_ref, m1_ref, m3_ref, tbr_ref, tbi_ref, f2_ref, skip_ref,
                    o_ref, buf_ref, z_ref, *, n1f, h1, rs, ns, nk):
    s = pl.program_id(2)
    p1 = ns
    p2 = p1 + nk
    p3 = p2 + ns
    p4 = p3 + nk

    def stage_a(xs, n2, b):
        a = jnp.dot(m1_ref[0], xs, preferred_element_type=jnp.float32)
        _stage_a_store(buf_ref, a, n2, tbr_ref[b], tbi_ref[b], n1f, rs)

    def stage_b(order, kb):
        for kk in range(TW_B):
            k1 = kb * TW_B + kk
            y = jnp.dot(f2_ref[0], _stage_b_load(buf_ref, k1, n1f, rs), preferred_element_type=jnp.float32)
            g = g_ref[0, kk]
            yr, yi = _cmul(y[:F2N], y[F2N:], g[:F2N], g[F2N:])
            ys = jnp.concatenate([yr, yi], axis=0).astype(jnp.bfloat16)
            z = jnp.dot(f2_ref[1], ys, preferred_element_type=jnp.float32)
            buf_ref[pl.ds(k1, F2N, stride=rs), :] = z[:F2N]
            buf_ref[pl.ds(n1f + k1, F2N, stride=rs), :] = z[F2N:]

    def stage_c(n2, b):
        base = pl.multiple_of(n2 * rs, 8)
        zr = buf_ref[pl.ds(base, n1f), :]
        zi = buf_ref[pl.ds(base + n1f, n1f), :]
        wr, wi = _cmul(zr, zi, tbr_ref[b], -tbi_ref[b])
        ws = jnp.concatenate([wr, wi], axis=0).astype(jnp.bfloat16)
        return jnp.dot(m3_ref[0], ws, preferred_element_type=jnp.float32)

    @pl.when(s < p1)
    def _():
        for b in range(TW_B):
            xs = jnp.concatenate([v_ref[0, b], v_ref[1, b]], axis=0)
            stage_a(xs, s * TW_B + b, b)

    @pl.when((s >= p1) & (s < p2))
    def _():
        stage_b(0, s - p1)

    @pl.when((s >= p2) & (s < p3))
    def _():
        for b in range(TW_B):
            n2 = (s - p2) * TW_B + b
            c = stage_c(n2, b)
            zs = []
            for r in range(2):
                vv = v_ref[r, b].astype(jnp.float32)
                z = x1_ref[r, b].astype(jnp.float32) * (c[r * h1:(r + 1) * h1] + skip_ref[0:1, :] * vv)
                z_ref[n2, r] = z
                zs.append(z)
            stage_a(jnp.concatenate(zs, axis=0).astype(jnp.bfloat16), n2, b)

    @pl.when((s >= p3) & (s < p4))
    def _():
        stage_b(1, s - p3)

    @pl.when(s >= p4)
    def _():
        for b in range(TW_B):
            n2 = (s - p4) * TW_B + b
            c = stage_c(n2, b)
            for r in range(2):
                y = x2_ref[r, b].astype(jnp.float32) * (c[r * h1:(r + 1) * h1] + skip_ref[1:2, :] * z_ref[n2, r])
                o_ref[r, b] = y.astype(o_ref.dtype)


def _hy_conv(hyp, spec, m1, m3, tbr, tbi, f2, skip, L):
    B = hyp.shape[0]
    n1f = 2 * L // F2N
    h1 = n1f // 2
    rs = 2 * n1f + PAD_ROWS
    ns = F2N // TW_B
    nk = n1f // TW_B
    ncb = GROUP_W // HY_LANES
    p1, p2, p3, p4 = ns, ns + nk, 2 * ns + nk, 2 * ns + 2 * nk
    clip = lambda v, hi: jnp.clip(v, 0, hi)
    tblk = (2, TW_B, h1, HY_LANES)
    v_idx = lambda c, p, s: (p, jnp.where(s < p1, s, clip(s - p2, ns - 1)), 0, c)
    x1_idx = lambda c, p, s: (p, clip(s - p2, ns - 1), 0, ncb + c)
    x2_idx = lambda c, p, s: (p, clip(s - p4, ns - 1), 0, 2 * ncb + c)
    g_idx = lambda c, p, s: (jnp.where(s < p3, 0, 1), jnp.where(s < p3, clip(s - p1, nk - 1), clip(s - p3, nk - 1)), 0, c)
    m1_idx = lambda c, p, s: (jnp.where(s < p1, s, clip(s - p2, ns - 1)), 0, 0)
    m3_idx = lambda c, p, s: (jnp.where(s < p4, clip(s - p2, ns - 1), s - p4), 0, 0)
    return pl.pallas_call(
        functools.partial(_hy_conv_kernel, n1f=n1f, h1=h1, rs=rs, ns=ns, nk=nk),
        grid=(ncb, B // 2, p4 + ns),
        in_specs=[
            pl.BlockSpec(tblk, v_idx),
            pl.BlockSpec(tblk, x1_idx),
            pl.BlockSpec(tblk, x2_idx),
            pl.BlockSpec((1, TW_B, 2 * F2N, HY_LANES), g_idx),
            pl.BlockSpec((1, 2 * n1f, 2 * h1), m1_idx),
            pl.BlockSpec((1, 2 * h1, 2 * n1f), m3_idx),
            pl.BlockSpec(tbr.shape, lambda c, p, s: (0, 0, 0)),
            pl.BlockSpec(tbi.shape, lambda c, p, s: (0, 0, 0)),
            pl.BlockSpec((2, 2 * F2N, 2 * F2N), lambda c, p, s: (0, 0, 0)),
            pl.BlockSpec((2, HY_LANES), lambda c, p, s: (0, c)),
        ],
        out_specs=pl.BlockSpec(tblk, lambda c, p, s: (p, clip(s - p4, ns - 1), 0, c)),
        out_shape=jax.ShapeDtypeStruct((B, F2N, h1, GROUP_W), jnp.bfloat16),
        scratch_shapes=[
            pltpu.VMEM((F2N * rs, HY_LANES), jnp.float32),
            pltpu.VMEM((F2N, 2, h1, HY_LANES), jnp.float32),
        ],
        compiler_params=pltpu.CompilerParams(
            dimension_semantics=("arbitrary", "arbitrary", "arbitrary"), vmem_limit_bytes=VMEM_LIMIT),
        name="hyena_conv",
    )(hyp, hyp, hyp, spec, m1, m3, tbr, tbi, f2, skip)


def _hy_filter_kernel(pos_ref, w1_ref, b1_ref, w2_ref, b2_ref, w3_ref, fr_ref, dec_ref, o_ref, *, nblk):
    i = pl.program_id(0)
    pos = pos_ref[...]
    fr = fr_ref[...]
    hid = jnp.sin(fr * (jnp.dot(pos, w1_ref[...], precision=_HI, preferred_element_type=jnp.float32) + b1_ref[...]))
    hid = jnp.sin(fr * (jnp.dot(hid, w2_ref[...], precision=_HI, preferred_element_type=jnp.float32) + b2_ref[...]))
    filt = jnp.dot(hid, w3_ref[...], precision=_HI, preferred_element_type=jnp.float32)
    filt = filt * jnp.exp(-pos[:, 0:1] * dec_ref[...])
    o_ref[...] = jnp.where(i < nblk, filt, 0.0)


def _hy_filters(pos, w1, b1, w2, b2, w3, freq, decay, L):
    nblk = L // F2N
    nh = w2.shape[0]
    no = w3.shape[1]
    const = lambda i: (0, 0)
    return pl.pallas_call(
        functools.partial(_hy_filter_kernel, nblk=nblk),
        grid=(nblk + 1,),
        in_specs=[
            pl.BlockSpec((F2N, HY_POS_PAD), lambda i: (jnp.minimum(i, nblk - 1), 0)),
            pl.BlockSpec((HY_POS_PAD, nh), const), pl.BlockSpec((1, nh), const),
            pl.BlockSpec((nh, nh), const), pl.BlockSpec((1, nh), const),
            pl.BlockSpec((nh, no), const), pl.BlockSpec((1, nh), const), pl.BlockSpec((1, no), const),
        ],
        out_specs=pl.BlockSpec((F2N, no), lambda i: (i, 0)),
        out_shape=jax.ShapeDtypeStruct((L + F2N, no), jnp.float32),
        compiler_params=pltpu.CompilerParams(dimension_semantics=("arbitrary",), vmem_limit_bytes=VMEM_LIMIT),
        name="hyena_filters",
    )(pos, w1, b1, w2, b2, w3, freq, decay)


def _hy_constants(L):
    n = 2 * L
    n1f = n // F2N
    h1 = n1f // 2
    na = F2N // TW_B

    def cis(num, den):
        ang = (-2.0 * math.pi / den) * (num % den).astype(jnp.float32)
        return jnp.cos(ang), jnp.sin(ang)

    k1 = jnp.arange(n1f)
    n1 = jnp.arange(n1f)
    a = jnp.arange(na)
    f1r, f1i = cis(k1[:, None] * n1[None, :], n1f)
    tar, tai = cis(k1[None, :] * (TW_B * a)[:, None], n)
    mr, mi = _cmul(f1r[None], f1i[None], tar[:, :, None], tai[:, :, None])

    def blockform(r, i):
        return jnp.concatenate([jnp.concatenate([r, -i], axis=-1), jnp.concatenate([i, r], axis=-1)], axis=-2)

    m1 = blockform(mr[:, :, :h1], mi[:, :, :h1]).astype(jnp.bfloat16)
    m3r = jnp.swapaxes(mr[:, :, :h1], 1, 2) / n
    m3i = -jnp.swapaxes(mi[:, :, :h1], 1, 2) / n
    m3 = blockform(m3r, m3i).astype(jnp.bfloat16)
    mgr = jnp.concatenate([mr[:, :, :h1], mr[:, :, h1:][:, :, ::-1]], axis=-1)
    mgi = jnp.concatenate([mi[:, :, :h1], mi[:, :, h1:][:, :, ::-1]], axis=-1)
    mg = jnp.concatenate([mgr, mgi], axis=1).astype(jnp.bfloat16)
    b = jnp.arange(TW_B)
    tbr, tbi = cis(k1[None, :] * b[:, None], n)
    tbr = jnp.broadcast_to(tbr[:, :, None], (TW_B, n1f, HY_LANES))
    tbi = jnp.broadcast_to(tbi[:, :, None], (TW_B, n1f, HY_LANES))
    k2 = jnp.arange(F2N)
    f2r, f2i = cis(k2[:, None] * k2[None, :], F2N)
    f2 = jnp.stack([blockform(f2r, f2i), blockform(f2r, -f2i)]).astype(jnp.bfloat16)
    return m1, m3, mg, tbr, tbi, f2


def _hy_positions(L):
    t = jnp.arange(L, dtype=jnp.float32)
    bands = jnp.arange(1, 9, dtype=jnp.float32)
    ang = (2.0 * math.pi / L) * t[:, None] * bands[None, :]
    pos = jnp.concatenate([(t / L)[:, None], jnp.cos(ang), jnp.sin(ang)], axis=-1)
    return jnp.pad(pos, ((0, 0), (0, HY_POS_PAD - pos.shape[1])))


MAIN_COL_RANGES = ((0, 1536), (2048, 3072), (3600, 4624), (1536, 2048), (3072, 3584), (4624, 6160), (6176, 8224))
SMALL_COL_RANGES = ((3584, 3600), (6160, 6176))


def _take_cols(a, ranges):
    return jnp.concatenate([a[..., lo:hi] for lo, hi in ranges], axis=-1)


def _pad_cols(a, width, left=0):
    return jnp.pad(a, [(0, 0)] * (a.ndim - 1) + [(left, width - left - a.shape[-1])])


def kernel(x, norm_w, w_in, w_out, hy_conv_w, hy_conv_b, hy_w1, hy_b1, hy_w2, hy_b2, hy_w3, hy_freq, hy_decay,
           hy_skip, mb_conv_w, mb_conv_b, mb_dt_bias, mb_a_log, mb_d, mb_norm_w, ml_conv_w, ml_conv_b, ml_gate_b,
           ml_norm_w, na_qnorm_w, na_knorm_w, na_rpb):
    B, L, D = x.shape
    depth = w_in.shape[0]
    ncols = N_TILES * TILE_N
    f32 = jnp.float32
    pos = _hy_positions(L)
    m1, m3, mg, tbr, tbi, f2 = _hy_constants(L)
    gmean = jnp.kron(jnp.eye(NA_H, dtype=f32), jnp.full((NA_DH, NA_DH), 1.0 / NA_DH)).astype(jnp.bfloat16)
    head_cols = jnp.arange(MB_H * MB_P) // MB_P
    expand = [(jnp.arange(SMALL_W)[:, None] == (d * MB_H + head_cols)[None, :]).astype(f32) for d in range(2)]
    for l in range(depth):
        w = _take_cols(w_in[l], MAIN_COL_RANGES).astype(jnp.bfloat16)
        ws = _pad_cols(_take_cols(w_in[l], SMALL_COL_RANGES), SMALL_W).astype(jnp.bfloat16)
        taps = _pad_cols(jnp.concatenate([hy_conv_w[l], mb_conv_w[l], ml_conv_w[l]], axis=-1), ncols)
        cbias = _pad_cols(jnp.concatenate([hy_conv_b[l], mb_conv_b[l], ml_conv_b[l]])[None], ncols)
        qk_scale = jnp.concatenate([jnp.tile(na_qnorm_w[l], NA_H) * (NA_DH ** -0.5 * math.log2(math.e)),
                                    jnp.tile(na_knorm_w[l], NA_H)])[None]
        cscale = _pad_cols(qk_scale, ncols, left=(T_NA_Q + N_HY_TILES) * TILE_N)
        proj, hy, small = _inproj(x, norm_w[l][None], w, ws, taps, cbias, cscale, gmean)

        filt = _hy_filters(pos, _pad_cols(hy_w1[l].T, HY_POS_PAD).T, hy_b1[l][None], hy_w2[l], hy_b2[l][None],
                           hy_w3[l], hy_freq[l][None], hy_decay[l][None], L)
        spec = _hy_spectrum(filt, mg, tbr, tbi, f2[:1], L)
        hyp = hy.reshape(B, L // F2N, F2N, N_HY_TILES * GROUP_W).transpose(0, 2, 1, 3)
        y_hy = _hy_conv(hyp, spec, m1, m3, tbr, tbi, f2, hy_skip[l], L)
        y_hy = y_hy.transpose(0, 2, 1, 3).reshape(B, L, GROUP_W)

        a_neg = -jnp.exp(mb_a_log[l].astype(f32))
        y_b = None
        for d in (1, 0):
            bias = _pad_cols(mb_dt_bias[l, d][None], SMALL_W, left=d * MB_H)
            a_row = _pad_cols(a_neg[d][None], SMALL_W, left=d * MB_H)
            if d == 1:
                y_b = _ssd(proj, small, expand[d], bias, a_row, reverse=True)
            else:
                y_mb = _ssd(proj, small, expand[d], bias, a_row, reverse=False, y_b=y_b,
                            dskip=jnp.repeat(mb_d[l], MB_P)[None], nw=mb_norm_w[l][None])

        gbias = _pad_cols(ml_gate_b[l].reshape(1, -1), SMALL_W, left=ML_GCOL)
        h_b = _mlstm(proj, small, gbias, reverse=True)
        y_ml = _mlstm(proj, small, gbias, reverse=False, h_b=h_b, nw=ml_norm_w[l][None])

        y_na = _na(proj, _na_bias_table(na_rpb[l]))

        x = _outproj(x, y_hy, proj, y_mb, y_ml, y_na, w_out[l].astype(jnp.bfloat16).reshape(4, GROUP_W, D))
    return x
```

```python
import functools
import math

import jax
import jax.numpy as jnp
from jax import lax
from jax.experimental import pallas as pl
from jax.experimental.pallas import tpu as pltpu

RMS_EPS = 1e-6
GROUP_W = 512
TILE_N = 512
SMALL_W = 128
SMALL_OUT = 2 * SMALL_W
HALO = 16
SUB_ROWS = 256
VMEM_LIMIT = 56 * 1024 * 1024

N_HY_TILES = 3
N_CONV_TILES = 7
N_TILES = 16
T_MB_X, T_MB_BC = 0, 1
T_ML_Q, T_ML_K = 2, 3
T_HY_G, T_MB_Z, T_ML_V, T_ML_O, T_ML_Z = 4, 5, 6, 7, 8
T_NA_Q, T_NA_K, T_NA_V, T_NA_G = 9, 10, 11, 12
N_MAIN_TILES = N_TILES - N_HY_TILES

_HI = lax.Precision.HIGHEST


def _silu(x):
    return x * jax.nn.sigmoid(x)


def _inproj_kernel(x_ref, xp_ref, xn_ref, nw_ref, w_ref, ws_ref, taps_ref, cb_ref, cs_ref, gm_ref,
                   o_ref, oh_ref, os_ref, h_ref, acc_ref, *, tm):
    i = pl.program_id(1)
    j = pl.program_id(2)
    ni = pl.num_programs(1)
    is_qk = (j == T_NA_Q + N_HY_TILES) | (j == T_NA_K + N_HY_TILES)

    def norm(xv):
        ms = jnp.mean(xv * xv, axis=-1, keepdims=True)
        return (xv * lax.rsqrt(ms + RMS_EPS) * nw_ref[...]).astype(jnp.bfloat16)

    @pl.when(j == 0)
    def _():
        h_ref[pl.ds(HALO, tm), :] = norm(x_ref[0])
        hp = norm(xp_ref[0])
        hn = norm(xn_ref[0])
        h_ref[pl.ds(0, HALO), :] = jnp.where(i == 0, jnp.zeros_like(hp), hp)
        h_ref[pl.ds(HALO + tm, HALO), :] = jnp.where(i == ni - 1, jnp.zeros_like(hn), hn)
        os_ref[0] = jnp.dot(h_ref[pl.ds(HALO, tm), :], ws_ref[...], preferred_element_type=jnp.float32)

    nsub = tm // SUB_ROWS

    def conv_tile(dst_ref, act):
        t = taps_ref[...]
        for c in range(nsub):
            lo = 0 if c == 0 else 2 * HALO + c * SUB_ROWS
            hi = 2 * HALO + (c + 1) * SUB_ROWS
            acc_ref[pl.ds(lo, hi - lo), :] = jnp.dot(h_ref[pl.ds(lo, hi - lo), :], w_ref[...],
                                                      preferred_element_type=jnp.float32)
            r0 = c * SUB_ROWS + HALO
            y = (acc_ref[pl.ds(r0 - 1, SUB_ROWS), :] * t[0:1] + acc_ref[pl.ds(r0, SUB_ROWS), :] * t[1:2]
                 + acc_ref[pl.ds(r0 + 1, SUB_ROWS), :] * t[2:3] + cb_ref[...])
            dst_ref[0, pl.ds(c * SUB_ROWS, SUB_ROWS), :] = act(y).astype(dst_ref.dtype)

    def plain_tile(post):
        for c in range(nsub):
            a = jnp.dot(h_ref[pl.ds(HALO + c * SUB_ROWS, SUB_ROWS), :], w_ref[...], preferred_element_type=jnp.float32)
            o_ref[0, pl.ds(c * SUB_ROWS, SUB_ROWS), :] = post(a).astype(o_ref.dtype)

    def qk_norm(a):
        ms = jnp.dot((a * a).astype(jnp.bfloat16), gm_ref[...], preferred_element_type=jnp.float32)
        return a * lax.rsqrt(ms + RMS_EPS) * cs_ref[...]

    @pl.when(j < N_HY_TILES)
    def _():
        conv_tile(oh_ref, lambda y: y)

    @pl.when((j >= N_HY_TILES) & (j < N_CONV_TILES))
    def _():
        conv_tile(o_ref, _silu)

    @pl.when((j >= N_CONV_TILES) & jnp.logical_not(is_qk))
    def _():
        plain_tile(lambda a: a)

    @pl.when(is_qk)
    def _():
        plain_tile(qk_norm)


def _inproj(x, nw, w, ws, taps, cbias, cscale, gmean):
    B, L, D = x.shape
    tm = min(1024, L)
    ni = L // tm
    hb = tm // HALO
    nlast = L // HALO - 1
    grid = (B, ni, N_TILES)
    main = lambda b, i, j: (b, i, jnp.maximum(j - N_HY_TILES, 0))
    return pl.pallas_call(
        functools.partial(_inproj_kernel, tm=tm),
        grid=grid,
        in_specs=[
            pl.BlockSpec((1, tm, D), lambda b, i, j: (b, i, 0)),
            pl.BlockSpec((1, HALO, D), lambda b, i, j: (b, jnp.maximum(i * hb - 1, 0), 0)),
            pl.BlockSpec((1, HALO, D), lambda b, i, j: (b, jnp.minimum((i + 1) * hb, nlast), 0)),
            pl.BlockSpec((1, D), lambda b, i, j: (0, 0)),
            pl.BlockSpec((D, TILE_N), lambda b, i, j: (0, j)),
            pl.BlockSpec((D, SMALL_OUT), lambda b, i, j: (0, 0)),
            pl.BlockSpec((3, TILE_N), lambda b, i, j: (0, j)),
            pl.BlockSpec((1, TILE_N), lambda b, i, j: (0, j)),
            pl.BlockSpec((1, TILE_N), lambda b, i, j: (0, j)),
            pl.BlockSpec((TILE_N, TILE_N), lambda b, i, j: (0, 0)),
        ],
        out_specs=[
            pl.BlockSpec((1, tm, TILE_N), main),
            pl.BlockSpec((1, tm, TILE_N), lambda b, i, j: (b, i, jnp.minimum(j, N_HY_TILES - 1))),
            pl.BlockSpec((1, tm, SMALL_OUT), lambda b, i, j: (b, i, 0)),
        ],
        out_shape=[
            jax.ShapeDtypeStruct((B, L, N_MAIN_TILES * TILE_N), jnp.bfloat16),
            jax.ShapeDtypeStruct((B, L, N_HY_TILES * TILE_N), jnp.bfloat16),
            jax.ShapeDtypeStruct((B, L, SMALL_OUT), jnp.float32),
        ],
        scratch_shapes=[
            pltpu.VMEM((tm + 2 * HALO, D), jnp.bfloat16),
            pltpu.VMEM((tm + 2 * HALO, TILE_N), jnp.float32),
        ],
        compiler_params=pltpu.CompilerParams(
            dimension_semantics=("arbitrary", "arbitrary", "arbitrary"), vmem_limit_bytes=VMEM_LIMIT),
        name="inproj",
    )(x, x, x, nw, w, ws, taps, cbias, cscale, gmean)


def _outproj_kernel(x_ref, yh_ref, g_ref, ym_ref, yl_ref, yn_ref, w_ref, o_ref):
    yh = (yh_ref[0].astype(jnp.float32) * _silu(g_ref[0].astype(jnp.float32))).astype(jnp.bfloat16)
    acc = jnp.dot(yh, w_ref[0], preferred_element_type=jnp.float32)
    acc += jnp.dot(ym_ref[0], w_ref[1], preferred_element_type=jnp.float32)
    acc += jnp.dot(yl_ref[0], w_ref[2], preferred_element_type=jnp.float32)
    acc += jnp.dot(yn_ref[0], w_ref[3], preferred_element_type=jnp.float32)
    o_ref[0] = x_ref[0] + acc


def _outproj(x, y_hy, proj, y_mb, y_ml, y_na, w_out):
    B, L, D = x.shape
    tm = min(512, L)
    tok = lambda b, i: (b, i, 0)
    return pl.pallas_call(
        _outproj_kernel,
        grid=(B, L // tm),
        in_specs=[
            pl.BlockSpec((1, tm, D), tok),
            pl.BlockSpec((1, tm, GROUP_W), tok),
            pl.BlockSpec((1, tm, GROUP_W), lambda b, i: (b, i, T_HY_G)),
            pl.BlockSpec((1, tm, GROUP_W), tok),
            pl.BlockSpec((1, tm, GROUP_W), tok),
            pl.BlockSpec((1, tm, GROUP_W), tok),
            pl.BlockSpec((4, GROUP_W, D), lambda b, i: (0, 0, 0)),
        ],
        out_specs=pl.BlockSpec((1, tm, D), tok),
        out_shape=jax.ShapeDtypeStruct((B, L, D), jnp.float32),
        compiler_params=pltpu.CompilerParams(
            dimension_semantics=("arbitrary", "arbitrary"), vmem_limit_bytes=VMEM_LIMIT),
        name="outproj",
    )(x, y_hy, proj, y_mb, y_ml, y_na, w_out)


CHUNK = 256
NEG = -1e30


def _dot_nt(a, b):
    return lax.dot_general(a, b, (((1,), (1,)), ((), ())), preferred_element_type=jnp.float32)


def _dot_tn(a, b):
    return lax.dot_general(a, b, (((0,), (0,)), ((), ())), preferred_element_type=jnp.float32)


def _chunk_masks(q, reverse):
    t = lax.broadcasted_iota(jnp.int32, (q, q), 0)
    s = lax.broadcasted_iota(jnp.int32, (q, q), 1)
    mask = (s >= t) if reverse else (s <= t)
    return mask, mask.astype(jnp.float32)


MB_H = 8
MB_P = 64
MB_N = 128
MB_GW = 256


def _ssd_kernel(*refs, reverse, finalize, dcol):
    if finalize:
        (xs_ref, bc_ref, sm_ref, e_ref, bias_ref, a_ref, yb_ref, z_ref, dsk_ref, nw_ref, o_ref, s_ref) = refs
    else:
        (xs_ref, bc_ref, sm_ref, e_ref, bias_ref, a_ref, o_ref, s_ref) = refs
    q = CHUNK

    @pl.when(pl.program_id(1) == 0)
    def _():
        s_ref[...] = jnp.zeros_like(s_ref)

    mask, tri = _chunk_masks(q, reverse)
    dt_s = jax.nn.softplus(sm_ref[0] + bias_ref[...])
    a_s = dt_s * a_ref[...]
    c_s = jnp.dot(tri, a_s, precision=_HI, preferred_element_type=jnp.float32)
    c_t = c_s.T
    c_full = jnp.dot(c_s, e_ref[...], precision=_HI, preferred_element_type=jnp.float32)
    dt_full = jnp.dot(dt_s, e_ref[...], precision=_HI, preferred_element_type=jnp.float32)
    far = 0 if reverse else q - 1
    tot_full = c_full[far:far + 1, :]
    x = xs_ref[0].astype(jnp.float32)
    dtx = dt_full * x
    lane = lax.broadcasted_iota(jnp.int32, (1, MB_GW), 1)
    ys = []
    for g in range(2):
        bg = bc_ref[0, :, g * MB_N:(g + 1) * MB_N]
        cg = bc_ref[0, :, MB_GW + g * MB_N:MB_GW + (g + 1) * MB_N]
        gs = slice(g * MB_GW, (g + 1) * MB_GW)
        gram = _dot_nt(cg, bg)
        dtx_g = dtx[:, gs]
        yg = jnp.zeros((q, MB_GW), jnp.float32)
        for j in range(4):
            col = dcol + g * 4 + j
            decay = jnp.exp(jnp.where(mask, c_s[:, col:col + 1] - c_t[col:col + 1, :], NEG))
            m = (gram * decay).astype(jnp.bfloat16)
            xm = jnp.where((lane >= j * MB_P) & (lane < (j + 1) * MB_P), dtx_g, 0.0).astype(jnp.bfloat16)
            yg = yg + jnp.dot(m, xm, preferred_element_type=jnp.float32)
        s_g = s_ref[:, gs]
        yg = yg + jnp.exp(c_full[:, gs]) * jnp.dot(cg, s_g.astype(jnp.bfloat16), preferred_element_type=jnp.float32)
        w = jnp.exp(tot_full[:, gs] - c_full[:, gs])
        s_ref[:, gs] = jnp.exp(tot_full[:, gs]) * s_g + _dot_tn(bg, (w * dtx_g).astype(jnp.bfloat16))
        ys.append(yg)
    y = jnp.concatenate(ys, axis=-1)
    if not finalize:
        o_ref[0] = y
        return
    y = y + yb_ref[0] + x * dsk_ref[...]
    y = y * _silu(z_ref[0].astype(jnp.float32))
    outs = []
    for g in range(2):
        yg = y[:, g * MB_GW:(g + 1) * MB_GW]
        ms = jnp.mean(yg * yg, axis=-1, keepdims=True)
        outs.append(yg * lax.rsqrt(ms + RMS_EPS))
    o_ref[0] = (jnp.concatenate(outs, axis=-1) * nw_ref[...]).astype(o_ref.dtype)


def _ssd(proj, small, e, bias, a_row, *, reverse, y_b=None, dskip=None, nw=None):
    B, L, _ = proj.shape
    nc = L // CHUNK
    finalize = y_b is not None
    if reverse:
        cm = lambda col: (lambda b, i: (b, nc - 1 - i, col))
    else:
        cm = lambda col: (lambda b, i: (b, i, col))
    const = lambda b, i: (0, 0)
    in_specs = [
        pl.BlockSpec((1, CHUNK, GROUP_W), cm(T_MB_X)),
        pl.BlockSpec((1, CHUNK, GROUP_W), cm(T_MB_BC)),
        pl.BlockSpec((1, CHUNK, SMALL_W), cm(0)),
        pl.BlockSpec((SMALL_W, GROUP_W), const),
        pl.BlockSpec((1, SMALL_W), const),
        pl.BlockSpec((1, SMALL_W), const),
    ]
    args = [proj, proj, small, e, bias, a_row]
    if finalize:
        in_specs += [
            pl.BlockSpec((1, CHUNK, GROUP_W), cm(0)),
            pl.BlockSpec((1, CHUNK, GROUP_W), cm(T_MB_Z)),
            pl.BlockSpec((1, GROUP_W), const),
            pl.BlockSpec((1, GROUP_W), const),
        ]
        args += [y_b, proj, dskip, nw]
    return pl.pallas_call(
        functools.partial(_ssd_kernel, reverse=reverse, finalize=finalize, dcol=8 if reverse else 0),
        grid=(B, nc),
        in_specs=in_specs,
        out_specs=pl.BlockSpec((1, CHUNK, GROUP_W), cm(0)),
        out_shape=jax.ShapeDtypeStruct((B, L, GROUP_W), jnp.bfloat16 if finalize else jnp.float32),
        scratch_shapes=[pltpu.VMEM((MB_N, GROUP_W), jnp.float32)],
        compiler_params=pltpu.CompilerParams(
            dimension_semantics=("arbitrary", "arbitrary"), vmem_limit_bytes=VMEM_LIMIT),
        name="ssd_bwd" if reverse else "ssd_fwd",
    )(*args)


ML_H = 4
ML_DH = 128
ML_GCOL = 16


def _mlstm_kernel(*refs, reverse, finalize, dcol):
    if finalize:
        (q_ref, k_ref, v_ref, sm_ref, gb_ref, hb_ref, og_ref, z_ref, nw_ref, o_ref, c_ref, n_ref, m_ref) = refs
    else:
        (q_ref, k_ref, v_ref, sm_ref, gb_ref, o_ref, c_ref, n_ref, m_ref) = refs
    q = CHUNK

    @pl.when(pl.program_id(1) == 0)
    def _():
        c_ref[...] = jnp.zeros_like(c_ref)
        n_ref[...] = jnp.zeros_like(n_ref)
        m_ref[...] = jnp.zeros_like(m_ref)

    mask, tri = _chunk_masks(q, reverse)
    graw = sm_ref[0] + gb_ref[...]
    lf = jax.nn.log_sigmoid(graw)
    b_s = jnp.dot(tri, lf, precision=_HI, preferred_element_type=jnp.float32)
    b_t = b_s.T
    g_t = graw.T
    far = 0 if reverse else q - 1
    hs = []
    for h in range(ML_H):
        ci = dcol + h
        cf = dcol + ML_H + h
        hsl = slice(h * ML_DH, (h + 1) * ML_DH)
        b_col = b_s[:, cf:cf + 1]
        i_col = graw[:, ci:ci + 1]
        dmat = jnp.where(mask, b_col - b_t[cf:cf + 1, :] + g_t[ci:ci + 1, :], NEG)
        m_prev = m_ref[h:h + 1, 0:1]
        m_inter = b_col + m_prev
        m_t = jnp.maximum(m_inter, jnp.max(dmat, axis=-1, keepdims=True))
        w_inter = jnp.exp(m_inter - m_t)
        p = jnp.exp(dmat - m_t)
        qh = q_ref[0, :, hsl]
        khf = k_ref[0, :, hsl].astype(jnp.float32) * (ML_DH ** -0.5)
        kh = khf.astype(jnp.bfloat16)
        vh = v_ref[0, :, hsl]
        s = _dot_nt(qh, kh) * p
        c_h = c_ref[h]
        n_h = n_ref[h:h + 1, :]
        num = w_inter * jnp.dot(qh, c_h.astype(jnp.bfloat16), preferred_element_type=jnp.float32)
        num = num + jnp.dot(s.astype(jnp.bfloat16), vh, preferred_element_type=jnp.float32)
        den = w_inter * jnp.sum(qh.astype(jnp.float32) * n_h, axis=-1, keepdims=True)
        den = den + jnp.sum(s, axis=-1, keepdims=True)
        hs.append(num / jnp.maximum(jnp.abs(den), jnp.exp(-m_t)))
        b_tot = b_col[far:far + 1, :]
        g_col = b_tot - b_col + i_col
        m_new = jnp.maximum(b_tot + m_prev, jnp.max(g_col, axis=0, keepdims=True))
        kw = khf * jnp.exp(g_col - m_new)
        decay = jnp.exp(b_tot + m_prev - m_new)
        c_ref[h] = decay * c_h + _dot_tn(kw.astype(jnp.bfloat16), vh)
        n_ref[h:h + 1, :] = decay * n_h + jnp.sum(kw, axis=0, keepdims=True)
        m_ref[h:h + 1, :] = jnp.broadcast_to(m_new, (1, ML_DH))
    hcat = jnp.concatenate(hs, axis=-1)
    if not finalize:
        o_ref[0] = hcat
        return
    hcat = (hcat + hb_ref[0]) * jax.nn.sigmoid(og_ref[0].astype(jnp.float32))
    outs = []
    for h in range(ML_H):
        hh = hcat[:, h * ML_DH:(h + 1) * ML_DH]
        ms = jnp.mean(hh * hh, axis=-1, keepdims=True)
        outs.append(hh * lax.rsqrt(ms + RMS_EPS))
    y = jnp.concatenate(outs, axis=-1) * nw_ref[...]
    o_ref[0] = (y * _silu(z_ref[0].astype(jnp.float32))).astype(o_ref.dtype)


def _mlstm(proj, small, gbias, *, reverse, h_b=None, nw=None):
    B, L, _ = proj.shape
    nc = L // CHUNK
    finalize = h_b is not None
    if reverse:
        cm = lambda col: (lambda b, i: (b, nc - 1 - i, col))
    else:
        cm = lambda col: (lambda b, i: (b, i, col))
    const = lambda b, i: (0, 0)
    in_specs = [
        pl.BlockSpec((1, CHUNK, GROUP_W), cm(T_ML_Q)),
        pl.BlockSpec((1, CHUNK, GROUP_W), cm(T_ML_K)),
        pl.BlockSpec((1, CHUNK, GROUP_W), cm(T_ML_V)),
        pl.BlockSpec((1, CHUNK, SMALL_W), cm(0)),
        pl.BlockSpec((1, SMALL_W), const),
    ]
    args = [proj, proj, proj, small, gbias]
    if finalize:
        in_specs += [
            pl.BlockSpec((1, CHUNK, GROUP_W), cm(0)),
            pl.BlockSpec((1, CHUNK, GROUP_W), cm(T_ML_O)),
            pl.BlockSpec((1, CHUNK, GROUP_W), cm(T_ML_Z)),
            pl.BlockSpec((1, GROUP_W), const),
        ]
        args += [h_b, proj, proj, nw]
    return pl.pallas_call(
        functools.partial(_mlstm_kernel, reverse=reverse, finalize=finalize,
                          dcol=ML_GCOL + (8 if reverse else 0)),
        grid=(B, nc),
        in_specs=in_specs,
        out_specs=pl.BlockSpec((1, CHUNK, GROUP_W), cm(0)),
        out_shape=jax.ShapeDtypeStruct((B, L, GROUP_W), jnp.bfloat16 if finalize else jnp.float32),
        scratch_shapes=[
            pltpu.VMEM((ML_H, ML_DH, ML_DH), jnp.float32),
            pltpu.VMEM((8, ML_DH), jnp.float32),
            pltpu.VMEM((8, ML_DH), jnp.float32),
        ],
        compiler_params=pltpu.CompilerParams(
            dimension_semantics=("arbitrary", "arbitrary"), vmem_limit_bytes=VMEM_LIMIT),
        name="mlstm_bwd" if reverse else "mlstm_fwd",
    )(*args)


LOG2E = math.log2(math.e)


def _split_cols(x):
    hi = x.astype(jnp.bfloat16)
    lo = (x - hi.astype(jnp.float32)).astype(jnp.bfloat16)
    return jnp.concatenate([hi, lo], axis=1)


def _split_rows(x):
    hi = x.astype(jnp.bfloat16)
    lo = (x - hi.astype(jnp.float32)).astype(jnp.bfloat16)
    return jnp.concatenate([hi, lo], axis=0)


def _tri2(mask):
    tri = mask.astype(jnp.bfloat16)
    return jnp.concatenate([tri, tri], axis=1)


def _ssd_direction(xs_ref, bc_ref, sm_ref, bias, a_row, ee, ee2, s_ref, o_ref, *, reverse, dcol):
    q = CHUNK
    mask, _ = _chunk_masks(q, reverse)
    dt_s = jax.nn.softplus(sm_ref[0] + bias)
    a_s = dt_s * a_row
    c_s = jnp.dot(_tri2(mask), _split_rows(a_s), preferred_element_type=jnp.float32)
    c_t = c_s.T
    full = jnp.dot(_split_cols(jnp.concatenate([c_s, dt_s], axis=0)), ee, preferred_element_type=jnp.float32)
    c_full, dt_full = full[:q], full[q:]
    cc_all = jnp.dot(_split_cols(c_s), ee2, preferred_element_type=jnp.float32)
    far = 0 if reverse else q - 1
    tot_full = c_full[far:far + 1, :]
    x = xs_ref[0].astype(jnp.float32)
    dtx = dt_full * x
    lane = lax.broadcasted_iota(jnp.int32, (1, MB_GW), 1)
    ys = []
    yield
    for g in range(2):
        bg = bc_ref[0, :, g * MB_N:(g + 1) * MB_N]
        cg = bc_ref[0, :, MB_GW + g * MB_N:MB_GW + (g + 1) * MB_N]
        gs = slice(g * MB_GW, (g + 1) * MB_GW)
        gram = _dot_nt(cg, bg)
        dtx_g = dtx[:, gs]
        ms, xms = [], []
        for j in range(4):
            hd = g * 4 + j
            col = dcol + hd
            decay = jnp.exp2(jnp.where(mask, cc_all[:, hd * q:(hd + 1) * q] - c_t[col:col + 1, :], NEG))
            ms.append((gram * decay).astype(jnp.bfloat16))
            xms.append(jnp.where((lane >= j * MB_P) & (lane < (j + 1) * MB_P), dtx_g, 0.0).astype(jnp.bfloat16))
        yg = jnp.dot(jnp.concatenate(ms, axis=1), jnp.concatenate(xms, axis=0), preferred_element_type=jnp.float32)
        s_g = s_ref[:, gs]
        yg = yg + jnp.exp2(c_full[:, gs]) * jnp.dot(cg, s_g.astype(jnp.bfloat16), preferred_element_type=jnp.float32)
        w = jnp.exp2(tot_full[:, gs] - c_full[:, gs])
        s_ref[:, gs] = jnp.exp2(tot_full[:, gs]) * s_g + _dot_tn(bg, (w * dtx_g).astype(jnp.bfloat16))
        ys.append(yg)
        yield
    o_ref[0] = jnp.concatenate(ys, axis=-1).astype(o_ref.dtype)


def _interleave(*gens):
    live = list(gens)
    while live:
        for g in list(live):
            if next(g, StopIteration) is StopIteration:
                live.remove(g)


def _ssd2_kernel(xf_ref, bcf_ref, smf_ref, xb_ref, bcb_ref, smb_ref, bias_ref, a_ref, ee_ref, ee2_ref,
                 of_ref, ob_ref, s_ref):
    @pl.when(pl.program_id(1) == 0)
    def _():
        s_ref[...] = jnp.zeros_like(s_ref)

    _interleave(
        _ssd_direction(xf_ref, bcf_ref, smf_ref, bias_ref[0:1], a_ref[0:1], ee_ref[0], ee2_ref[0], s_ref.at[0],
                       of_ref, reverse=False, dcol=0),
        _ssd_direction(xb_ref, bcb_ref, smb_ref, bias_ref[1:2], a_ref[1:2], ee_ref[1], ee2_ref[1], s_ref.at[1],
                       ob_ref, reverse=True, dcol=MB_H))


def _ssd2(proj, small, bias, a_row, ee, ee2):
    B, L, _ = proj.shape
    nc = L // CHUNK
    fw = lambda col: (lambda b, i: (b, i, col))
    bw = lambda col: (lambda b, i: (b, nc - 1 - i, col))
    blk = (1, CHUNK, GROUP_W)
    sblk = (1, CHUNK, SMALL_W)
    const2 = lambda b, i: (0, 0)
    const3 = lambda b, i: (0, 0, 0)
    out = jax.ShapeDtypeStruct((B, L, GROUP_W), jnp.bfloat16)
    return pl.pallas_call(
        _ssd2_kernel,
        grid=(B, nc),
        in_specs=[
            pl.BlockSpec(blk, fw(T_MB_X)), pl.BlockSpec(blk, fw(T_MB_BC)), pl.BlockSpec(sblk, fw(0)),
            pl.BlockSpec(blk, bw(T_MB_X)), pl.BlockSpec(blk, bw(T_MB_BC)), pl.BlockSpec(sblk, bw(0)),
            pl.BlockSpec(bias.shape, const2), pl.BlockSpec(a_row.shape, const2),
            pl.BlockSpec(ee.shape, const3), pl.BlockSpec(ee2.shape, const3),
        ],
        out_specs=[pl.BlockSpec(blk, fw(0)), pl.BlockSpec(blk, bw(0))],
        out_shape=[out, out],
        scratch_shapes=[pltpu.VMEM((2, MB_N, GROUP_W), jnp.float32)],
        compiler_params=pltpu.CompilerParams(
            dimension_semantics=("arbitrary", "arbitrary"), vmem_limit_bytes=VMEM_LIMIT),
        name="ssd",
    )(proj, proj, small, proj, proj, small, bias, a_row, ee, ee2)


ML_LANE = 16


def _cummax_rows(u, reverse):
    n = u.shape[0]
    row = lax.broadcasted_iota(jnp.int32, u.shape, 0)
    k = 1
    while k < n:
        if reverse:
            shifted = jnp.where(row < n - k, pltpu.roll(u, n - k, axis=0), NEG)
        else:
            shifted = jnp.where(row >= k, pltpu.roll(u, k, axis=0), NEG)
        u = jnp.maximum(u, shifted)
        k *= 2
    return u


def _mlstm_direction(q_ref, k_ref, v_ref, si_ref, sf_ref, bias_i, bias_f, ee, st_ref, m_ref, o_ref, *, reverse, lane0):
    q = CHUNK
    mask, _ = _chunk_masks(q, reverse)
    gi = si_ref[0] + bias_i
    lf = jax.nn.log_sigmoid(sf_ref[0] + bias_f)
    b = jnp.dot(_tri2(mask), _split_rows(lf), preferred_element_type=jnp.float32)
    u = gi - b
    far = 0 if reverse else q - 1
    m_prev = m_ref[...]
    m_inter = b + m_prev
    m_t = jnp.maximum(m_inter, b + _cummax_rows(u, reverse))
    w_inter = jnp.exp(m_inter - m_t)
    em = jnp.exp(-m_t)
    b_tot = b[far:far + 1, :]
    g = b_tot - b + gi
    m_new = jnp.maximum(b_tot + m_prev, jnp.max(g, axis=0, keepdims=True))
    wk = jnp.exp(g - m_new) * (ML_DH ** -0.5)
    decay = jnp.exp(b_tot + m_prev - m_new)
    m_ref[...] = m_new
    arow = (b - m_t) * LOG2E + math.log2(ML_DH ** -0.5)
    ee_dh, ee_q = ee[:, :GROUP_W], ee[:, GROUP_W:]
    stack = jnp.concatenate([w_inter, em, wk], axis=0)
    full = jnp.dot(_split_cols(stack), ee_dh, preferred_element_type=jnp.float32)
    w_full, em_full, wk_full = full[:q], full[q:2 * q], full[2 * q:]
    a_all = jnp.dot(_split_cols(arow), ee_q, preferred_element_type=jnp.float32)
    dec_full = jnp.dot(_split_cols(jnp.broadcast_to(decay, (8, SMALL_W))), ee_dh,
                       preferred_element_type=jnp.float32)[0:1]
    u_t = (u * LOG2E).T
    ones = jnp.ones((q, ML_DH), jnp.bfloat16)
    hs = []
    yield
    for h in range(ML_H):
        hsl = slice(h * ML_DH, (h + 1) * ML_DH)
        p = jnp.exp2(jnp.where(mask, a_all[:, h * q:(h + 1) * q] + u_t[lane0 + h:lane0 + h + 1, :], NEG))
        qh = q_ref[0, :, hsl]
        kh = k_ref[0, :, hsl]
        v1 = jnp.concatenate([v_ref[0, :, hsl], ones], axis=1)
        s = (_dot_nt(qh, kh) * p).astype(jnp.bfloat16)
        st = st_ref[h]
        inter = jnp.dot(qh, st.astype(jnp.bfloat16), preferred_element_type=jnp.float32)
        both = w_full[:, hsl] * inter[:, :ML_DH], w_full[:, hsl] * inter[:, ML_DH:]
        intra = jnp.dot(s, v1, preferred_element_type=jnp.float32)
        num = both[0] + intra[:, :ML_DH]
        den = both[1] + intra[:, ML_DH:]
        hs.append(num / jnp.maximum(jnp.abs(den), em_full[:, hsl]))
        kw = (kh.astype(jnp.float32) * wk_full[:, hsl]).astype(jnp.bfloat16)
        dh = dec_full[:, hsl]
        st_ref[h] = jnp.concatenate([dh, dh], axis=1) * st + _dot_tn(kw, v1)
        yield
    o_ref[0] = jnp.concatenate(hs, axis=-1).astype(o_ref.dtype)


def _mlstm2_kernel(qf_ref, kf_ref, vf_ref, sif_ref, sff_ref, qb_ref, kb_ref, vb_ref, sib_ref, sfb_ref,
                   bi_ref, bf_ref, ee_ref, of_ref, ob_ref, st_ref, m_ref):
    @pl.when(pl.program_id(1) == 0)
    def _():
        st_ref[...] = jnp.zeros_like(st_ref)
        m_ref[...] = jnp.zeros_like(m_ref)

    _interleave(
        _mlstm_direction(qf_ref, kf_ref, vf_ref, sif_ref, sff_ref, bi_ref[...], bf_ref[...], ee_ref[0], st_ref.at[0],
                         m_ref.at[0], of_ref, reverse=False, lane0=ML_LANE),
        _mlstm_direction(qb_ref, kb_ref, vb_ref, sib_ref, sfb_ref, bi_ref[...], bf_ref[...], ee_ref[1], st_ref.at[1],
                         m_ref.at[1], ob_ref, reverse=True, lane0=ML_LANE + ML_H))


def _mlstm2(proj, small, bias_i, bias_f, ee):
    B, L, _ = proj.shape
    nc = L // CHUNK
    fw = lambda col: (lambda b, i: (b, i, col))
    bw = lambda col: (lambda b, i: (b, nc - 1 - i, col))
    blk = (1, CHUNK, GROUP_W)
    sblk = (1, CHUNK, SMALL_W)
    const2 = lambda b, i: (0, 0)
    out = jax.ShapeDtypeStruct((B, L, GROUP_W), jnp.bfloat16)
    specs = lambda m: [pl.BlockSpec(blk, m(T_ML_Q)), pl.BlockSpec(blk, m(T_ML_K)), pl.BlockSpec(blk, m(T_ML_V)),
                       pl.BlockSpec(sblk, m(0)), pl.BlockSpec(sblk, m(1))]
    return pl.pallas_call(
        _mlstm2_kernel,
        grid=(B, nc),
        in_specs=specs(fw) + specs(bw) + [
            pl.BlockSpec(bias_i.shape, const2), pl.BlockSpec(bias_f.shape, const2),
            pl.BlockSpec(ee.shape, lambda b, i: (0, 0, 0)),
        ],
        out_specs=[pl.BlockSpec(blk, fw(0)), pl.BlockSpec(blk, bw(0))],
        out_shape=[out, out],
        scratch_shapes=[
            pltpu.VMEM((2, ML_H, ML_DH, 2 * ML_DH), jnp.float32),
            pltpu.VMEM((2, 1, SMALL_W), jnp.float32),
        ],
        compiler_params=pltpu.CompilerParams(
            dimension_semantics=("arbitrary", "arbitrary"), vmem_limit_bytes=VMEM_LIMIT),
        name="mlstm",
    )(proj, proj, proj, small, small, proj, proj, proj, small, small, bias_i, bias_f, ee)


def _group_rmsnorm(y, width):
    outs = []
    for g in range(y.shape[-1] // width):
        yg = y[:, g * width:(g + 1) * width]
        ms = jnp.mean(yg * yg, axis=-1, keepdims=True)
        outs.append(yg * lax.rsqrt(ms + RMS_EPS))
    return jnp.concatenate(outs, axis=-1)


def _outproj2_kernel(x_ref, yh_ref, hg_ref, mf_ref, mb_ref, mx_ref, mz_ref, lf_ref, lb_ref, lo_ref, lz_ref, yn_ref,
                     dsk_ref, mnw_ref, lnw_ref, w_ref, o_ref):
    f32 = jnp.float32
    up = lambda r: r[0].astype(f32)
    yh = (up(yh_ref) * _silu(up(hg_ref))).astype(jnp.bfloat16)
    acc = jnp.dot(yh, w_ref[0], preferred_element_type=f32)
    ym = (up(mf_ref) + up(mb_ref) + up(mx_ref) * dsk_ref[...]) * _silu(up(mz_ref))
    ym = (_group_rmsnorm(ym, MB_GW) * mnw_ref[...]).astype(jnp.bfloat16)
    acc += jnp.dot(ym, w_ref[1], preferred_element_type=f32)
    yl = (up(lf_ref) + up(lb_ref)) * jax.nn.sigmoid(up(lo_ref))
    yl = (_group_rmsnorm(yl, ML_DH) * lnw_ref[...] * _silu(up(lz_ref))).astype(jnp.bfloat16)
    acc += jnp.dot(yl, w_ref[2], preferred_element_type=f32)
    acc += jnp.dot(yn_ref[0], w_ref[3], preferred_element_type=f32)
    o_ref[0] = x_ref[0] + acc


def _outproj2(x, y_hy, proj, y_mf, y_mb, h_f, h_b, y_na, dskip, mb_nw, ml_nw, w_out):
    B, L, D = x.shape
    tm = min(512, L)
    tok = lambda b, i: (b, i, 0)
    col = lambda c: (lambda b, i: (b, i, c))
    blk = (1, tm, GROUP_W)
    vec = pl.BlockSpec((1, GROUP_W), lambda b, i: (0, 0))
    return pl.pallas_call(
        _outproj2_kernel,
        grid=(B, L // tm),
        in_specs=[
            pl.BlockSpec((1, tm, D), tok),
            pl.BlockSpec(blk, tok), pl.BlockSpec(blk, col(T_HY_G)),
            pl.BlockSpec(blk, tok), pl.BlockSpec(blk, tok), pl.BlockSpec(blk, col(T_MB_X)), pl.BlockSpec(blk, col(T_MB_Z)),
            pl.BlockSpec(blk, tok), pl.BlockSpec(blk, tok), pl.BlockSpec(blk, col(T_ML_O)), pl.BlockSpec(blk, col(T_ML_Z)),
            pl.BlockSpec(blk, tok),
            vec, vec, vec,
            pl.BlockSpec((4, GROUP_W, D), lambda b, i: (0, 0, 0)),
        ],
        out_specs=pl.BlockSpec((1, tm, D), tok),
        out_shape=jax.ShapeDtypeStruct((B, L, D), jnp.float32),
        compiler_params=pltpu.CompilerParams(
            dimension_semantics=("arbitrary", "arbitrary"), vmem_limit_bytes=VMEM_LIMIT),
        name="outproj",
    )(x, y_hy, proj, y_mf, y_mb, proj, proj, h_f, h_b, proj, proj, y_na, dskip, mb_nw, ml_nw, w_out)


GRID_W = 64
NA_KR = 8
NA_KC = 16
NA_H = 8
NA_DH = 64
ROWS_PER_BLOCK = 8
BLOCK_TOK = ROWS_PER_BLOCK * GRID_W


def _na_kernel(q_ref, kp_ref, kc_ref, kn_ref, vp_ref, vc_ref, vn_ref, g_ref, tbl_ref, o_ref, k_s, v_s, *, rows):
    mblk = pl.program_id(1)
    npair = NA_H // 2
    pw = 2 * NA_DH
    ones = jnp.ones((BLOCK_TOK, pw), jnp.bfloat16)
    for t, (kr, vr) in enumerate(((kp_ref, vp_ref), (kc_ref, vc_ref), (kn_ref, vn_ref))):
        k_s[pl.ds(t * BLOCK_TOK, BLOCK_TOK), :] = kr[0]
        for hp in range(npair):
            v_s[pl.ds(t * BLOCK_TOK, BLOCK_TOK), 2 * hp * pw:(2 * hp + 1) * pw] = vr[0, :, hp * pw:(hp + 1) * pw]
            v_s[pl.ds(t * BLOCK_TOK, BLOCK_TOK), (2 * hp + 1) * pw:(2 * hp + 2) * pw] = ones
    lane = lax.broadcasted_iota(jnp.int32, (GRID_W, pw), 1)
    first = lane < NA_DH
    win = NA_KR * GRID_W
    for j in range(ROWS_PER_BLOCK):
        r = mblk * ROWS_PER_BLOCK + j
        rs = jnp.clip(r - NA_KR // 2, 0, rows - NA_KR)
        didx = r - rs
        off = pl.multiple_of((rs - (mblk - 1) * ROWS_PER_BLOCK) * GRID_W, GRID_W)
        ss = []
        for hp in range(npair):
            ls = slice(hp * pw, (hp + 1) * pw)
            qp = q_ref[0, j * GRID_W:(j + 1) * GRID_W, ls]
            zero = jnp.zeros_like(qp)
            q2 = jnp.concatenate([jnp.where(first, qp, zero), jnp.where(first, zero, qp)], axis=0)
            ss.append(_dot_nt(q2, k_s[pl.ds(off, win), ls]))
        s = jnp.concatenate(ss, axis=0) + tbl_ref[didx].astype(jnp.float32)
        e = jnp.exp2(s - jnp.max(s, axis=-1, keepdims=True)).astype(jnp.bfloat16)
        outs = []
        for hp in range(npair):
            ov = jnp.dot(e[hp * pw:(hp + 1) * pw], v_s[pl.ds(off, win), 2 * hp * pw:(2 * hp + 2) * pw],
                         preferred_element_type=jnp.float32)
            o2 = ov[:, :pw] / ov[:, pw:]
            outs.append(jnp.where(first, o2[:GRID_W], o2[GRID_W:]))
        o = jnp.concatenate(outs, axis=-1)
        gate = _silu(g_ref[0, j * GRID_W:(j + 1) * GRID_W, :].astype(jnp.float32))
        o_ref[0, j * GRID_W:(j + 1) * GRID_W, :] = (o * gate).astype(o_ref.dtype)


def _na(proj, tbl):
    B, L, _ = proj.shape
    rows = L // GRID_W
    nb = rows // ROWS_PER_BLOCK
    prev = lambda col: (lambda b, m: (b, jnp.maximum(m - 1, 0), col))
    cur = lambda col: (lambda b, m: (b, m, col))
    nxt = lambda col: (lambda b, m: (b, jnp.minimum(m + 1, nb - 1), col))
    blk = (1, BLOCK_TOK, GROUP_W)
    return pl.pallas_call(
        functools.partial(_na_kernel, rows=rows),
        grid=(B, nb),
        in_specs=[
            pl.BlockSpec(blk, cur(T_NA_Q)),
            pl.BlockSpec(blk, prev(T_NA_K)), pl.BlockSpec(blk, cur(T_NA_K)), pl.BlockSpec(blk, nxt(T_NA_K)),
            pl.BlockSpec(blk, prev(T_NA_V)), pl.BlockSpec(blk, cur(T_NA_V)), pl.BlockSpec(blk, nxt(T_NA_V)),
            pl.BlockSpec(blk, cur(T_NA_G)),
            pl.BlockSpec(tbl.shape, lambda b, m: (0, 0, 0)),
        ],
        out_specs=pl.BlockSpec(blk, cur(0)),
        out_shape=jax.ShapeDtypeStruct((B, L, GROUP_W), jnp.bfloat16),
        scratch_shapes=[
            pltpu.VMEM((3 * BLOCK_TOK, GROUP_W), jnp.bfloat16),
            pltpu.VMEM((3 * BLOCK_TOK, 2 * GROUP_W), jnp.bfloat16),
        ],
        compiler_params=pltpu.CompilerParams(
            dimension_semantics=("arbitrary", "arbitrary"), vmem_limit_bytes=VMEM_LIMIT),
        name="nbr_attn",
    )(proj, proj, proj, proj, proj, proj, proj, proj, tbl)


def _na_bias_table(rpb):
    nco = 2 * NA_KC - 1
    rows_d = jnp.stack([rpb[:, NA_KR - 1 - d:2 * NA_KR - 1 - d, :] for d in range(NA_KR)])
    lpad = GRID_W - NA_KC
    ext = jnp.pad(rows_d * math.log2(math.e), ((0, 0), (0, 0), (0, 0), (lpad, 2 * GRID_W - lpad - nco)))
    lead = ext.shape[:3]
    skew = jnp.broadcast_to(ext[..., None, :], lead + (GRID_W, 2 * GRID_W)).reshape(lead + (2 * GRID_W * GRID_W,))
    skew = skew[..., :GRID_W * (2 * GRID_W - 1)].reshape(lead + (GRID_W, 2 * GRID_W - 1))[..., GRID_W - 1:]
    w = jnp.arange(GRID_W)
    c = jnp.arange(GRID_W)
    cstart = jnp.clip(w - NA_KC // 2, 0, GRID_W - NA_KC)
    inwin = (c[None, :] >= cstart[:, None]) & (c[None, :] < cstart[:, None] + NA_KC)
    t = jnp.where(inwin, skew, NEG)
    t = t.transpose(0, 1, 3, 2, 4).reshape(NA_KR, NA_H * GRID_W, NA_KR * GRID_W)
    return t.astype(jnp.bfloat16)


HY_LANES = 128
F2N = 128
TW_B = 8
PAD_ROWS = 8
HY_POS_PAD = 128


def _cmul(ar, ai, br, bi):
    return ar * br - ai * bi, ar * bi + ai * br


def _stage_a_store(buf_ref, a, n2, tr, ti, n1f, rs):
    ar, ai = _cmul(a[:n1f], a[n1f:], tr, ti)
    base = pl.multiple_of(n2 * rs, 8)
    buf_ref[pl.ds(base, n1f), :] = ar
    buf_ref[pl.ds(base + n1f, n1f), :] = ai


def _stage_b_load(buf_ref, k1, n1f, rs):
    yr = buf_ref[pl.ds(k1, F2N, stride=rs), :]
    yi = buf_ref[pl.ds(n1f + k1, F2N, stride=rs), :]
    return jnp.concatenate([yr, yi], axis=0).astype(jnp.bfloat16)


def _hy_spec_kernel(ff_ref, fb_ref, mg_ref, tbr_ref, tbi_ref, f2_ref, o_ref, buf_ref, *, n1f, h1, rs, ns):
    s = pl.program_id(2)

    @pl.when(s < ns)
    def _():
        for b in range(TW_B):
            n2 = s * TW_B + b
            hf = ff_ref[pl.ds(n2, h1, stride=F2N), :]
            hb = fb_ref[pl.ds(F2N - n2, h1, stride=F2N), :]
            xs = jnp.concatenate([hf, hb], axis=0).astype(jnp.bfloat16)
            a = jnp.dot(mg_ref[0], xs, preferred_element_type=jnp.float32)
            _stage_a_store(buf_ref, a, n2, tbr_ref[b], tbi_ref[b], n1f, rs)

    @pl.when(s >= ns)
    def _():
        for kk in range(TW_B):
            k1 = (s - ns) * TW_B + kk
            o_ref[0, kk] = jnp.dot(f2_ref[0], _stage_b_load(buf_ref, k1, n1f, rs), preferred_element_type=jnp.float32)


def _hy_spectrum(filt, mg, tbr, tbi, f2, L):
    n1f = 2 * L // F2N
    h1 = n1f // 2
    rs = 2 * n1f + PAD_ROWS
    ns = F2N // TW_B
    nk = n1f // TW_B
    ncb = GROUP_W // HY_LANES
    a_idx = lambda o, c, s: (jnp.minimum(s, ns - 1), 0, 0)
    return pl.pallas_call(
        functools.partial(_hy_spec_kernel, n1f=n1f, h1=h1, rs=rs, ns=ns),
        grid=(2, ncb, ns + nk),
        in_specs=[
            pl.BlockSpec((L + F2N, HY_LANES), lambda o, c, s: (0, o * 2 * ncb + c)),
            pl.BlockSpec((L + F2N, HY_LANES), lambda o, c, s: (0, o * 2 * ncb + ncb + c)),
            pl.BlockSpec((1, 2 * n1f, n1f), a_idx),
            pl.BlockSpec(tbr.shape, lambda o, c, s: (0, 0, 0)),
            pl.BlockSpec(tbi.shape, lambda o, c, s: (0, 0, 0)),
            pl.BlockSpec((1, 2 * F2N, 2 * F2N), lambda o, c, s: (0, 0, 0)),
        ],
        out_specs=pl.BlockSpec((1, TW_B, 2 * F2N, HY_LANES), lambda o, c, s: (o, jnp.maximum(s - ns, 0), 0, c)),
        out_shape=jax.ShapeDtypeStruct((2, n1f, 2 * F2N, GROUP_W), jnp.float32),
        scratch_shapes=[pltpu.VMEM((F2N * rs, HY_LANES), jnp.float32)],
        compiler_params=pltpu.CompilerParams(
            dimension_semantics=("arbitrary", "arbitrary", "arbitrary"), vmem_limit_bytes=VMEM_LIMIT),
        name="hyena_spectrum",
    )(filt, filt, mg, tbr, tbi, f2)


def _hy_conv_kernel(v_ref, x1_ref, x2_ref, g_ref, m1_ref, m3_ref, tbr_ref, tbi_ref, f2_ref, skip_ref,
                    o_ref, buf_ref, z_ref, *, n1f, h1, rs, ns, nk):
    s = pl.program_id(2)
    p1 = ns
    p2 = p1 + nk
    p3 = p2 + ns
    p4 = p3 + nk

    def stage_a(xs, n2, b):
        a = jnp.dot(m1_ref[0], xs, preferred_element_type=jnp.float32)
        _stage_a_store(buf_ref, a, n2, tbr_ref[b], tbi_ref[b], n1f, rs)

    def stage_b(order, kb):
        for kk in range(TW_B):
            k1 = kb * TW_B + kk
            y = jnp.dot(f2_ref[0], _stage_b_load(buf_ref, k1, n1f, rs), preferred_element_type=jnp.float32)
            g = g_ref[0, kk]
            yr, yi = _cmul(y[:F2N], y[F2N:], g[:F2N], g[F2N:])
            ys = jnp.concatenate([yr, yi], axis=0).astype(jnp.bfloat16)
            z = jnp.dot(f2_ref[1], ys, preferred_element_type=jnp.float32)
            buf_ref[pl.ds(k1, F2N, stride=rs), :] = z[:F2N]
            buf_ref[pl.ds(n1f + k1, F2N, stride=rs), :] = z[F2N:]

    def stage_c(n2, b):
        base = pl.multiple_of(n2 * rs, 8)
        zr = buf_ref[pl.ds(base, n1f), :]
        zi = buf_ref[pl.ds(base + n1f, n1f), :]
        wr, wi = _cmul(zr, zi, tbr_ref[b], -tbi_ref[b])
        ws = jnp.concatenate([wr, wi], axis=0).astype(jnp.bfloat16)
        return jnp.dot(m3_ref[0], ws, preferred_element_type=jnp.float32)

    @pl.when(s < p1)
    def _():
        for b in range(TW_B):
            xs = jnp.concatenate([v_ref[0, b], v_ref[1, b]], axis=0)
            stage_a(xs, s * TW_B + b, b)

    @pl.when((s >= p1) & (s < p2))
    def _():
        stage_b(0, s - p1)

    @pl.when((s >= p2) & (s < p3))
    def _():
        for b in range(TW_B):
            n2 = (s - p2) * TW_B + b
            c = stage_c(n2, b)
            zs = []
            for r in range(2):
                vv = v_ref[r, b].astype(jnp.float32)
                z = x1_ref[r, b].astype(jnp.float32) * (c[r * h1:(r + 1) * h1] + skip_ref[0:1, :] * vv)
                z_ref[n2, r] = z
                zs.append(z)
            stage_a(jnp.concatenate(zs, axis=0).astype(jnp.bfloat16), n2, b)

    @pl.when((s >= p3) & (s < p4))
    def _():
        stage_b(1, s - p3)

    @pl.when(s >= p4)
    def _():
        for b in range(TW_B):
            n2 = (s - p4) * TW_B + b
            c = stage_c(n2, b)
            for r in range(2):
                y = x2_ref[r, b].astype(jnp.float32) * (c[r * h1:(r + 1) * h1] + skip_ref[1:2, :] * z_ref[n2, r])
                o_ref[r, b] = y.astype(o_ref.dtype)


def _hy_conv(hyp, spec, m1, m3, tbr, tbi, f2, skip, L):
    B = hyp.shape[0]
    n1f = 2 * L // F2N
    h1 = n1f // 2
    rs = 2 * n1f + PAD_ROWS
    ns = F2N // TW_B
    nk = n1f // TW_B
    ncb = GROUP_W // HY_LANES
    p1, p2, p3, p4 = ns, ns + nk, 2 * ns + nk, 2 * ns + 2 * nk
    clip = lambda v, hi: jnp.clip(v, 0, hi)
    tblk = (2, TW_B, h1, HY_LANES)
    v_idx = lambda c, p, s: (p, jnp.where(s < p1, s, clip(s - p2, ns - 1)), 0, c)
    x1_idx = lambda c, p, s: (p, clip(s - p2, ns - 1), 0, ncb + c)
    x2_idx = lambda c, p, s: (p, clip(s - p4, ns - 1), 0, 2 * ncb + c)
    g_idx = lambda c, p, s: (jnp.where(s < p3, 0, 1), jnp.where(s < p3, clip(s - p1, nk - 1), clip(s - p3, nk - 1)), 0, c)
    m1_idx = lambda c, p, s: (jnp.where(s < p1, s, clip(s - p2, ns - 1)), 0, 0)
    m3_idx = lambda c, p, s: (jnp.where(s < p4, clip(s - p2, ns - 1), s - p4), 0, 0)
    return pl.pallas_call(
        functools.partial(_hy_conv_kernel, n1f=n1f, h1=h1, rs=rs, ns=ns, nk=nk),
        grid=(ncb, B // 2, p4 + ns),
        in_specs=[
            pl.BlockSpec(tblk, v_idx),
            pl.BlockSpec(tblk, x1_idx),
            pl.BlockSpec(tblk, x2_idx),
            pl.BlockSpec((1, TW_B, 2 * F2N, HY_LANES), g_idx),
            pl.BlockSpec((1, 2 * n1f, 2 * h1), m1_idx),
            pl.BlockSpec((1, 2 * h1, 2 * n1f), m3_idx),
            pl.BlockSpec(tbr.shape, lambda c, p, s: (0, 0, 0)),
            pl.BlockSpec(tbi.shape, lambda c, p, s: (0, 0, 0)),
            pl.BlockSpec((2, 2 * F2N, 2 * F2N), lambda c, p, s: (0, 0, 0)),
            pl.BlockSpec((2, HY_LANES), lambda c, p, s: (0, c)),
        ],
        out_specs=pl.BlockSpec(tblk, lambda c, p, s: (p, clip(s - p4, ns - 1), 0, c)),
        out_shape=jax.ShapeDtypeStruct((B, F2N, h1, GROUP_W), jnp.bfloat16),
        scratch_shapes=[
            pltpu.VMEM((F2N * rs, HY_LANES), jnp.float32),
            pltpu.VMEM((F2N, 2, h1, HY_LANES), jnp.float32),
        ],
        compiler_params=pltpu.CompilerParams(
            dimension_semantics=("arbitrary", "arbitrary", "arbitrary"), vmem_limit_bytes=VMEM_LIMIT),
        name="hyena_conv",
    )(hyp, hyp, hyp, spec, m1, m3, tbr, tbi, f2, skip)


def _hy_filter_kernel(pos_ref, w1_ref, b1_ref, w2_ref, b2_ref, w3_ref, fr_ref, dec_ref, o_ref, *, nblk):
    i = pl.program_id(0)
    pos = pos_ref[...]
    fr = fr_ref[...]
    hid = jnp.sin(fr * (jnp.dot(pos, w1_ref[...], precision=_HI, preferred_element_type=jnp.float32) + b1_ref[...]))
    hid = jnp.sin(fr * (jnp.dot(hid, w2_ref[...], precision=_HI, preferred_element_type=jnp.float32) + b2_ref[...]))
    filt = jnp.dot(hid, w3_ref[...], precision=_HI, preferred_element_type=jnp.float32)
    filt = filt * jnp.exp(-pos[:, 0:1] * dec_ref[...])
    o_ref[...] = jnp.where(i < nblk, filt, 0.0)


def _hy_filters(pos, w1, b1, w2, b2, w3, freq, decay, L):
    nblk = L // F2N
    nh = w2.shape[0]
    no = w3.shape[1]
    const = lambda i: (0, 0)
    return pl.pallas_call(
        functools.partial(_hy_filter_kernel, nblk=nblk),
        grid=(nblk + 1,),
        in_specs=[
            pl.BlockSpec((F2N, HY_POS_PAD), lambda i: (jnp.minimum(i, nblk - 1), 0)),
            pl.BlockSpec((HY_POS_PAD, nh), const), pl.BlockSpec((1, nh), const),
            pl.BlockSpec((nh, nh), const), pl.BlockSpec((1, nh), const),
            pl.BlockSpec((nh, no), const), pl.BlockSpec((1, nh), const), pl.BlockSpec((1, no), const),
        ],
        out_specs=pl.BlockSpec((F2N, no), lambda i: (i, 0)),
        out_shape=jax.ShapeDtypeStruct((L + F2N, no), jnp.float32),
        compiler_params=pltpu.CompilerParams(dimension_semantics=("arbitrary",), vmem_limit_bytes=VMEM_LIMIT),
        name="hyena_filters",
    )(pos, w1, b1, w2, b2, w3, freq, decay)


def _hy_constants(L):
    n = 2 * L
    n1f = n // F2N
    h1 = n1f // 2
    na = F2N // TW_B

    def cis(num, den):
        ang = (-2.0 * math.pi / den) * (num % den).astype(jnp.float32)
        return jnp.cos(ang), jnp.sin(ang)

    k1 = jnp.arange(n1f)
    n1 = jnp.arange(n1f)
    a = jnp.arange(na)
    f1r, f1i = cis(k1[:, None] * n1[None, :], n1f)
    tar, tai = cis(k1[None, :] * (TW_B * a)[:, None], n)
    mr, mi = _cmul(f1r[None], f1i[None], tar[:, :, None], tai[:, :, None])

    def blockform(r, i):
        return jnp.concatenate([jnp.concatenate([r, -i], axis=-1), jnp.concatenate([i, r], axis=-1)], axis=-2)

    m1 = blockform(mr[:, :, :h1], mi[:, :, :h1]).astype(jnp.bfloat16)
    m3r = jnp.swapaxes(mr[:, :, :h1], 1, 2) / n
    m3i = -jnp.swapaxes(mi[:, :, :h1], 1, 2) / n
    m3 = blockform(m3r, m3i).astype(jnp.bfloat16)
    mgr = jnp.concatenate([mr[:, :, :h1], mr[:, :, h1:][:, :, ::-1]], axis=-1)
    mgi = jnp.concatenate([mi[:, :, :h1], mi[:, :, h1:][:, :, ::-1]], axis=-1)
    mg = jnp.concatenate([mgr, mgi], axis=1).astype(jnp.bfloat16)
    b = jnp.arange(TW_B)
    tbr, tbi = cis(k1[None, :] * b[:, None], n)
    tbr = jnp.broadcast_to(tbr[:, :, None], (TW_B, n1f, HY_LANES))
    tbi = jnp.broadcast_to(tbi[:, :, None], (TW_B, n1f, HY_LANES))
    k2 = jnp.arange(F2N)
    f2r, f2i = cis(k2[:, None] * k2[None, :], F2N)
    f2 = jnp.stack([blockform(f2r, f2i), blockform(f2r, -f2i)]).astype(jnp.bfloat16)
    return m1, m3, mg, tbr, tbi, f2


def _hy_positions(L):
    t = jnp.arange(L, dtype=jnp.float32)
    bands = jnp.arange(1, 9, dtype=jnp.float32)
    ang = (2.0 * math.pi / L) * t[:, None] * bands[None, :]
    pos = jnp.concatenate([(t / L)[:, None], jnp.cos(ang), jnp.sin(ang)], axis=-1)
    return jnp.pad(pos, ((0, 0), (0, HY_POS_PAD - pos.shape[1])))


MAIN_COL_RANGES = ((0, 1536), (2048, 3072), (3600, 4624), (1536, 2048), (3072, 3584), (4624, 6160), (6176, 8224))
SMALL_COL_RANGES = ((3584, 3600), (6160, 6176))


def _take_cols(a, ranges):
    return jnp.concatenate([a[..., lo:hi] for lo, hi in ranges], axis=-1)


def _pad_cols(a, width, left=0):
    return jnp.pad(a, [(0, 0)] * (a.ndim - 1) + [(left, width - left - a.shape[-1])])


def kernel(x, norm_w, w_in, w_out, hy_conv_w, hy_conv_b, hy_w1, hy_b1, hy_w2, hy_b2, hy_w3, hy_freq, hy_decay,
           hy_skip, mb_conv_w, mb_conv_b, mb_dt_bias, mb_a_log, mb_d, mb_norm_w, ml_conv_w, ml_conv_b, ml_gate_b,
           ml_norm_w, na_qnorm_w, na_knorm_w, na_rpb):
    B, L, D = x.shape
    depth = w_in.shape[0]
    ncols = N_TILES * TILE_N
    f32 = jnp.float32
    pos = _hy_positions(L)
    m1, m3, mg, tbr, tbi, f2 = _hy_constants(L)
    gmean = jnp.kron(jnp.eye(NA_H, dtype=f32), jnp.full((NA_DH, NA_DH), 1.0 / NA_DH)).astype(jnp.bfloat16)

    def expansion(first_lane, heads, width):
        tgt = jnp.arange(heads * width) // width
        e = [(jnp.arange(SMALL_W)[:, None] == (first_lane + d * heads + tgt)[None, :]) for d in range(2)]
        e = jnp.stack(e).astype(jnp.bfloat16)
        return jnp.concatenate([e, e], axis=1)

    ee_mb = expansion(0, MB_H, MB_P)
    ee2_mb = expansion(0, MB_H, CHUNK)
    ee_ml = jnp.concatenate([expansion(ML_LANE, ML_H, ML_DH), expansion(ML_LANE, ML_H, CHUNK)], axis=2)
    for l in range(depth):
        w = _take_cols(w_in[l], MAIN_COL_RANGES).astype(jnp.bfloat16)
        wl = w_in[l]
        zeros = lambda n: jnp.zeros((D, n), wl.dtype)
        ws = jnp.concatenate([wl[:, 3584:3600], wl[:, 6160:6164], wl[:, 6168:6172], zeros(SMALL_W - 24),
                              zeros(ML_LANE), wl[:, 6164:6168], wl[:, 6172:6176], zeros(SMALL_W - 24)],
                             axis=1).astype(jnp.bfloat16)
        taps = _pad_cols(jnp.concatenate([hy_conv_w[l], mb_conv_w[l], ml_conv_w[l]], axis=-1), ncols)
        cbias = _pad_cols(jnp.concatenate([hy_conv_b[l], mb_conv_b[l], ml_conv_b[l]])[None], ncols)
        qk_scale = jnp.concatenate([jnp.tile(na_qnorm_w[l], NA_H) * (NA_DH ** -0.5 * math.log2(math.e)),
                                    jnp.tile(na_knorm_w[l], NA_H)])[None]
        cscale = _pad_cols(qk_scale, ncols, left=(T_NA_Q + N_HY_TILES) * TILE_N)
        proj, hy, small = _inproj(x, norm_w[l][None], w, ws, taps, cbias, cscale, gmean)

        filt = _hy_filters(pos, _pad_cols(hy_w1[l].T, HY_POS_PAD).T, hy_b1[l][None], hy_w2[l], hy_b2[l][None],
                           hy_w3[l], hy_freq[l][None], hy_decay[l][None], L)
        spec = _hy_spectrum(filt, mg, tbr, tbi, f2[:1], L)
        hyp = hy.reshape(B, L // F2N, F2N, N_HY_TILES * GROUP_W).transpose(0, 2, 1, 3)
        y_hy = _hy_conv(hyp, spec, m1, m3, tbr, tbi, f2, hy_skip[l], L)
        y_hy = y_hy.transpose(0, 2, 1, 3).reshape(B, L, GROUP_W)

        a_log2 = -jnp.exp(mb_a_log[l].astype(f32)) * LOG2E
        dt_bias = jnp.concatenate([_pad_cols(mb_dt_bias[l, d][None], SMALL_W, left=d * MB_H) for d in range(2)])
        a_rows = jnp.concatenate([_pad_cols(a_log2[d][None], SMALL_W, left=d * MB_H) for d in range(2)])
        y_mf, y_mb = _ssd2(proj, small, dt_bias, a_rows, ee_mb, ee2_mb)

        bias_i = _pad_cols(ml_gate_b[l][:, 0, :].reshape(1, -1), SMALL_W, left=ML_LANE)
        bias_f = _pad_cols(ml_gate_b[l][:, 1, :].reshape(1, -1), SMALL_W, left=ML_LANE)
        h_f, h_b = _mlstm2(proj, small, bias_i, bias_f, ee_ml)

        y_na = _na(proj, _na_bias_table(na_rpb[l]))

        x = _outproj2(x, y_hy, proj, y_mf, y_mb, h_f, h_b, y_na, jnp.repeat(mb_d[l], MB_P)[None], mb_norm_w[l][None],
                      ml_norm_w[l][None], w_out[l].astype(jnp.bfloat16).reshape(4, GROUP_W, D))
    return x
```

```python
import functools
import math

import jax
import jax.numpy as jnp
from jax import lax
from jax.experimental import pallas as pl
from jax.experimental.pallas import tpu as pltpu

RMS_EPS = 1e-6
GROUP_W = 512
TILE_N = 512
SMALL_W = 128
SMALL_OUT = 2 * SMALL_W
HALO = 16
INPROJ_ROWS = 512
VMEM_LIMIT = 56 * 1024 * 1024

N_HY_TILES = 3
N_CONV_TILES = 7
N_TILES = 16
T_MB_X, T_MB_BC = 0, 1
T_ML_Q, T_ML_K = 2, 3
T_HY_G, T_MB_Z, T_ML_V, T_ML_O, T_ML_Z = 4, 5, 6, 7, 8
T_NA_Q, T_NA_K, T_NA_V, T_NA_G = 9, 10, 11, 12
N_MAIN_TILES = N_TILES - N_HY_TILES

_HI = lax.Precision.HIGHEST


def _silu(x):
    return x * jax.nn.sigmoid(x)


def _inproj_kernel(x_ref, xp_ref, xn_ref, nw_ref, w_ref, ws_ref, taps_ref, cb_ref, cs_ref, gm_ref,
                   o_ref, oh_ref, os_ref, h_ref, acc_ref, *, tm):
    i = pl.program_id(1)
    ni = pl.num_programs(1)

    def norm(xv):
        ms = jnp.mean(xv * xv, axis=-1, keepdims=True)
        return (xv * lax.rsqrt(ms + RMS_EPS) * nw_ref[...]).astype(jnp.bfloat16)

    h_ref[pl.ds(HALO, tm), :] = norm(x_ref[0])
    hp = norm(xp_ref[0])
    hn = norm(xn_ref[0])
    h_ref[pl.ds(0, HALO), :] = jnp.where(i == 0, jnp.zeros_like(hp), hp)
    h_ref[pl.ds(HALO + tm, HALO), :] = jnp.where(i == ni - 1, jnp.zeros_like(hn), hn)
    os_ref[0] = jnp.dot(h_ref[pl.ds(HALO, tm), :], ws_ref[...], preferred_element_type=jnp.float32)

    qk_tiles = (T_NA_Q + N_HY_TILES, T_NA_K + N_HY_TILES)

    def matmul(u):
        wt = w_ref[:, u * TILE_N:(u + 1) * TILE_N]
        if u < N_CONV_TILES:
            acc_ref[u % 2] = jnp.dot(h_ref[...], wt, preferred_element_type=jnp.float32)
        else:
            acc_ref[u % 2, pl.ds(0, tm), :] = jnp.dot(h_ref[pl.ds(HALO, tm), :], wt, preferred_element_type=jnp.float32)

    def epilogue(u):
        a = acc_ref.at[u % 2]
        cols = slice(u * TILE_N, (u + 1) * TILE_N)
        if u < N_CONV_TILES:
            t = taps_ref[:, cols]
            y = (a[pl.ds(HALO - 1, tm), :] * t[0:1] + a[pl.ds(HALO, tm), :] * t[1:2]
                 + a[pl.ds(HALO + 1, tm), :] * t[2:3] + cb_ref[:, cols])
            if u < N_HY_TILES:
                oh_ref[0, :, cols] = y.astype(oh_ref.dtype)
                return
            y = _silu(y)
        else:
            y = a[pl.ds(0, tm), :]
            if u in qk_tiles:
                ms = jnp.dot((y * y).astype(jnp.bfloat16), gm_ref[...], preferred_element_type=jnp.float32)
                y = y * lax.rsqrt(ms + RMS_EPS) * cs_ref[:, cols]
        o_ref[0, :, (u - N_HY_TILES) * TILE_N:(u - N_HY_TILES + 1) * TILE_N] = y.astype(o_ref.dtype)

    matmul(0)
    for u in range(N_TILES):
        if u + 1 < N_TILES:
            matmul(u + 1)
        epilogue(u)


def _inproj(x, nw, w, ws, taps, cbias, cscale, gmean):
    B, L, D = x.shape
    tm = min(INPROJ_ROWS, L)
    ni = L // tm
    hb = tm // HALO
    nlast = L // HALO - 1
    ncols = N_TILES * TILE_N
    const = lambda b, i: (0, 0)
    tok = lambda b, i: (b, i, 0)
    return pl.pallas_call(
        functools.partial(_inproj_kernel, tm=tm),
        grid=(B, ni),
        in_specs=[
            pl.BlockSpec((1, tm, D), tok),
            pl.BlockSpec((1, HALO, D), lambda b, i: (b, jnp.maximum(i * hb - 1, 0), 0)),
            pl.BlockSpec((1, HALO, D), lambda b, i: (b, jnp.minimum((i + 1) * hb, nlast), 0)),
            pl.BlockSpec((1, D), const),
            pl.BlockSpec((D, ncols), const, pipeline_mode=pl.Buffered(1)),
            pl.BlockSpec((D, SMALL_OUT), const),
            pl.BlockSpec((3, ncols), const),
            pl.BlockSpec((1, ncols), const),
            pl.BlockSpec((1, ncols), const),
            pl.BlockSpec((TILE_N, TILE_N), const),
        ],
        out_specs=[
            pl.BlockSpec((1, tm, N_MAIN_TILES * TILE_N), tok),
            pl.BlockSpec((1, tm, N_HY_TILES * TILE_N), tok),
            pl.BlockSpec((1, tm, SMALL_OUT), tok),
        ],
        out_shape=[
            jax.ShapeDtypeStruct((B, L, N_MAIN_TILES * TILE_N), jnp.bfloat16),
            jax.ShapeDtypeStruct((B, L, N_HY_TILES * TILE_N), jnp.bfloat16),
            jax.ShapeDtypeStruct((B, L, SMALL_OUT), jnp.float32),
        ],
        scratch_shapes=[
            pltpu.VMEM((tm + 2 * HALO, D), jnp.bfloat16),
            pltpu.VMEM((2, tm + 2 * HALO, TILE_N), jnp.float32),
        ],
        compiler_params=pltpu.CompilerParams(
            dimension_semantics=("arbitrary", "arbitrary"), vmem_limit_bytes=VMEM_LIMIT),
        name="inproj",
    )(x, x, x, nw, w, ws, taps, cbias, cscale, gmean)


CHUNK = 256
NEG = -1e30


def _dot_nt(a, b):
    return lax.dot_general(a, b, (((1,), (1,)), ((), ())), preferred_element_type=jnp.float32)


def _dot_tn(a, b):
    return lax.dot_general(a, b, (((0,), (0,)), ((), ())), preferred_element_type=jnp.float32)


def _chunk_mask(q, reverse):
    t = lax.broadcasted_iota(jnp.int32, (q, q), 0)
    s = lax.broadcasted_iota(jnp.int32, (q, q), 1)
    return (s >= t) if reverse else (s <= t)


MB_H = 8
MB_P = 64
MB_N = 128
MB_GW = 256
ML_H = 4
ML_DH = 128


LOG2E = math.log2(math.e)


def _split_cols(x):
    hi = x.astype(jnp.bfloat16)
    lo = (x - hi.astype(jnp.float32)).astype(jnp.bfloat16)
    return jnp.concatenate([hi, lo], axis=1)


def _split_rows(x):
    hi = x.astype(jnp.bfloat16)
    lo = (x - hi.astype(jnp.float32)).astype(jnp.bfloat16)
    return jnp.concatenate([hi, lo], axis=0)


def _tri2(mask):
    tri = mask.astype(jnp.bfloat16)
    return jnp.concatenate([tri, tri], axis=1)


def _ssd_direction(xs_ref, bc_ref, sm_ref, bias, a_row, ee, ee2, s_ref, o_ref, *, reverse, dcol):
    q = CHUNK
    mask = _chunk_mask(q, reverse)
    dt_s = jax.nn.softplus(sm_ref[0] + bias)
    a_s = dt_s * a_row
    c_s = jnp.dot(_tri2(mask), _split_rows(a_s), preferred_element_type=jnp.float32)
    c_t = c_s.T
    full = jnp.dot(_split_cols(jnp.concatenate([c_s, dt_s], axis=0)), ee, preferred_element_type=jnp.float32)
    c_full, dt_full = full[:q], full[q:]
    cc_all = jnp.dot(_split_cols(c_s), ee2, preferred_element_type=jnp.float32)
    far = 0 if reverse else q - 1
    tot_full = c_full[far:far + 1, :]
    x = xs_ref[0].astype(jnp.float32)
    dtx = dt_full * x
    lane = lax.broadcasted_iota(jnp.int32, (1, MB_GW), 1)
    ys = []
    yield
    for g in range(2):
        bg = bc_ref[0, :, g * MB_N:(g + 1) * MB_N]
        cg = bc_ref[0, :, MB_GW + g * MB_N:MB_GW + (g + 1) * MB_N]
        gs = slice(g * MB_GW, (g + 1) * MB_GW)
        gram = _dot_nt(cg, bg)
        dtx_g = dtx[:, gs]
        ms, xms = [], []
        for j in range(4):
            hd = g * 4 + j
            col = dcol + hd
            decay = jnp.exp2(jnp.where(mask, cc_all[:, hd * q:(hd + 1) * q] - c_t[col:col + 1, :], NEG))
            ms.append((gram * decay).astype(jnp.bfloat16))
            xms.append(jnp.where((lane >= j * MB_P) & (lane < (j + 1) * MB_P), dtx_g, 0.0).astype(jnp.bfloat16))
        yg = jnp.dot(jnp.concatenate(ms, axis=1), jnp.concatenate(xms, axis=0), preferred_element_type=jnp.float32)
        s_g = s_ref[:, gs]
        yg = yg + jnp.exp2(c_full[:, gs]) * jnp.dot(cg, s_g.astype(jnp.bfloat16), preferred_element_type=jnp.float32)
        w = jnp.exp2(tot_full[:, gs] - c_full[:, gs])
        s_ref[:, gs] = jnp.exp2(tot_full[:, gs]) * s_g + _dot_tn(bg, (w * dtx_g).astype(jnp.bfloat16))
        ys.append(yg)
        yield
    o_ref[0] = jnp.concatenate(ys, axis=-1).astype(o_ref.dtype)


def _interleave(*gens):
    live = list(gens)
    while live:
        for g in list(live):
            if next(g, StopIteration) is StopIteration:
                live.remove(g)


def _ssd2_kernel(xf_ref, bcf_ref, smf_ref, xb_ref, bcb_ref, smb_ref, bias_ref, a_ref, ee_ref, ee2_ref,
                 of_ref, ob_ref, s_ref):
    @pl.when(pl.program_id(1) == 0)
    def _():
        s_ref[...] = jnp.zeros_like(s_ref)

    _interleave(
        _ssd_direction(xf_ref, bcf_ref, smf_ref, bias_ref[0:1], a_ref[0:1], ee_ref[0], ee2_ref[0], s_ref.at[0],
                       of_ref, reverse=False, dcol=0),
        _ssd_direction(xb_ref, bcb_ref, smb_ref, bias_ref[1:2], a_ref[1:2], ee_ref[1], ee2_ref[1], s_ref.at[1],
                       ob_ref, reverse=True, dcol=MB_H))


def _ssd2(proj, small, bias, a_row, ee, ee2):
    B, L, _ = proj.shape
    nc = L // CHUNK
    fw = lambda col: (lambda b, i: (b, i, col))
    bw = lambda col: (lambda b, i: (b, nc - 1 - i, col))
    blk = (1, CHUNK, GROUP_W)
    sblk = (1, CHUNK, SMALL_W)
    const2 = lambda b, i: (0, 0)
    const3 = lambda b, i: (0, 0, 0)
    out = jax.ShapeDtypeStruct((B, L, GROUP_W), jnp.bfloat16)
    return pl.pallas_call(
        _ssd2_kernel,
        grid=(B, nc),
        in_specs=[
            pl.BlockSpec(blk, fw(T_MB_X)), pl.BlockSpec(blk, fw(T_MB_BC)), pl.BlockSpec(sblk, fw(0)),
            pl.BlockSpec(blk, bw(T_MB_X)), pl.BlockSpec(blk, bw(T_MB_BC)), pl.BlockSpec(sblk, bw(0)),
            pl.BlockSpec(bias.shape, const2), pl.BlockSpec(a_row.shape, const2),
            pl.BlockSpec(ee.shape, const3), pl.BlockSpec(ee2.shape, const3),
        ],
        out_specs=[pl.BlockSpec(blk, fw(0)), pl.BlockSpec(blk, bw(0))],
        out_shape=[out, out],
        scratch_shapes=[pltpu.VMEM((2, MB_N, GROUP_W), jnp.float32)],
        compiler_params=pltpu.CompilerParams(
            dimension_semantics=("arbitrary", "arbitrary"), vmem_limit_bytes=VMEM_LIMIT),
        name="ssd",
    )(proj, proj, small, proj, proj, small, bias, a_row, ee, ee2)


ML_LANE = 16


def _cummax_rows(u, reverse):
    n = u.shape[0]
    row = lax.broadcasted_iota(jnp.int32, u.shape, 0)
    k = 1
    while k < n:
        if reverse:
            shifted = jnp.where(row < n - k, pltpu.roll(u, n - k, axis=0), NEG)
        else:
            shifted = jnp.where(row >= k, pltpu.roll(u, k, axis=0), NEG)
        u = jnp.maximum(u, shifted)
        k *= 2
    return u


def _mlstm_direction(q_ref, k_ref, v_ref, si_ref, sf_ref, bias_i, bias_f, ee, st_ref, m_ref, o_ref, *, reverse, lane0):
    q = CHUNK
    mask = _chunk_mask(q, reverse)
    gi = si_ref[0] + bias_i
    lf = jax.nn.log_sigmoid(sf_ref[0] + bias_f)
    b = jnp.dot(_tri2(mask), _split_rows(lf), preferred_element_type=jnp.float32)
    u = gi - b
    far = 0 if reverse else q - 1
    m_prev = m_ref[...]
    m_inter = b + m_prev
    m_t = jnp.maximum(m_inter, b + _cummax_rows(u, reverse))
    w_inter = jnp.exp(m_inter - m_t)
    em = jnp.exp(-m_t)
    b_tot = b[far:far + 1, :]
    g = b_tot - b + gi
    m_new = jnp.maximum(b_tot + m_prev, jnp.max(g, axis=0, keepdims=True))
    wk = jnp.exp(g - m_new) * (ML_DH ** -0.5)
    decay = jnp.exp(b_tot + m_prev - m_new)
    m_ref[...] = m_new
    arow = (b - m_t) * LOG2E + math.log2(ML_DH ** -0.5)
    ee_dh, ee_q = ee[:, :GROUP_W], ee[:, GROUP_W:]
    stack = jnp.concatenate([w_inter, em, wk], axis=0)
    full = jnp.dot(_split_cols(stack), ee_dh, preferred_element_type=jnp.float32)
    w_full, em_full, wk_full = full[:q], full[q:2 * q], full[2 * q:]
    a_all = jnp.dot(_split_cols(arow), ee_q, preferred_element_type=jnp.float32)
    dec_full = jnp.dot(_split_cols(jnp.broadcast_to(decay, (8, SMALL_W))), ee_dh,
                       preferred_element_type=jnp.float32)[0:1]
    u_t = (u * LOG2E).T
    ones = jnp.ones((q, ML_DH), jnp.bfloat16)
    hs = []
    yield
    for h in range(ML_H):
        hsl = slice(h * ML_DH, (h + 1) * ML_DH)
        p = jnp.exp2(jnp.where(mask, a_all[:, h * q:(h + 1) * q] + u_t[lane0 + h:lane0 + h + 1, :], NEG))
        qh = q_ref[0, :, hsl]
        kh = k_ref[0, :, hsl]
        v1 = jnp.concatenate([v_ref[0, :, hsl], ones], axis=1)
        s = (_dot_nt(qh, kh) * p).astype(jnp.bfloat16)
        st = st_ref[h]
        inter = jnp.dot(qh, st.astype(jnp.bfloat16), preferred_element_type=jnp.float32)
        both = w_full[:, hsl] * inter[:, :ML_DH], w_full[:, hsl] * inter[:, ML_DH:]
        intra = jnp.dot(s, v1, preferred_element_type=jnp.float32)
        num = both[0] + intra[:, :ML_DH]
        den = both[1] + intra[:, ML_DH:]
        hs.append(num / jnp.maximum(jnp.abs(den), em_full[:, hsl]))
        kw = (kh.astype(jnp.float32) * wk_full[:, hsl]).astype(jnp.bfloat16)
        dh = dec_full[:, hsl]
        st_ref[h] = jnp.concatenate([dh, dh], axis=1) * st + _dot_tn(kw, v1)
        yield
    o_ref[0] = jnp.concatenate(hs, axis=-1).astype(o_ref.dtype)


def _mlstm2_kernel(qf_ref, kf_ref, vf_ref, sif_ref, sff_ref, qb_ref, kb_ref, vb_ref, sib_ref, sfb_ref,
                   bi_ref, bf_ref, ee_ref, of_ref, ob_ref, st_ref, m_ref):
    @pl.when(pl.program_id(1) == 0)
    def _():
        st_ref[...] = jnp.zeros_like(st_ref)
        m_ref[...] = jnp.zeros_like(m_ref)

    _interleave(
        _mlstm_direction(qf_ref, kf_ref, vf_ref, sif_ref, sff_ref, bi_ref[...], bf_ref[...], ee_ref[0], st_ref.at[0],
                         m_ref.at[0], of_ref, reverse=False, lane0=ML_LANE),
        _mlstm_direction(qb_ref, kb_ref, vb_ref, sib_ref, sfb_ref, bi_ref[...], bf_ref[...], ee_ref[1], st_ref.at[1],
                         m_ref.at[1], ob_ref, reverse=True, lane0=ML_LANE + ML_H))


def _mlstm2(proj, small, bias_i, bias_f, ee):
    B, L, _ = proj.shape
    nc = L // CHUNK
    fw = lambda col: (lambda b, i: (b, i, col))
    bw = lambda col: (lambda b, i: (b, nc - 1 - i, col))
    blk = (1, CHUNK, GROUP_W)
    sblk = (1, CHUNK, SMALL_W)
    const2 = lambda b, i: (0, 0)
    out = jax.ShapeDtypeStruct((B, L, GROUP_W), jnp.bfloat16)
    specs = lambda m: [pl.BlockSpec(blk, m(T_ML_Q)), pl.BlockSpec(blk, m(T_ML_K)), pl.BlockSpec(blk, m(T_ML_V)),
                       pl.BlockSpec(sblk, m(0)), pl.BlockSpec(sblk, m(1))]
    return pl.pallas_call(
        _mlstm2_kernel,
        grid=(B, nc),
        in_specs=specs(fw) + specs(bw) + [
            pl.BlockSpec(bias_i.shape, const2), pl.BlockSpec(bias_f.shape, const2),
            pl.BlockSpec(ee.shape, lambda b, i: (0, 0, 0)),
        ],
        out_specs=[pl.BlockSpec(blk, fw(0)), pl.BlockSpec(blk, bw(0))],
        out_shape=[out, out],
        scratch_shapes=[
            pltpu.VMEM((2, ML_H, ML_DH, 2 * ML_DH), jnp.float32),
            pltpu.VMEM((2, 1, SMALL_W), jnp.float32),
        ],
        compiler_params=pltpu.CompilerParams(
            dimension_semantics=("arbitrary", "arbitrary"), vmem_limit_bytes=VMEM_LIMIT),
        name="mlstm",
    )(proj, proj, proj, small, small, proj, proj, proj, small, small, bias_i, bias_f, ee)


def _group_rmsnorm(y, width):
    outs = []
    for g in range(y.shape[-1] // width):
        yg = y[:, g * width:(g + 1) * width]
        ms = jnp.mean(yg * yg, axis=-1, keepdims=True)
        outs.append(yg * lax.rsqrt(ms + RMS_EPS))
    return jnp.concatenate(outs, axis=-1)


def _outproj2_kernel(x_ref, yh_ref, hg_ref, mf_ref, mb_ref, mx_ref, mz_ref, lf_ref, lb_ref, lo_ref, lz_ref, yn_ref,
                     dsk_ref, mnw_ref, lnw_ref, w_ref, o_ref):
    f32 = jnp.float32
    up = lambda r: r[0].astype(f32)
    yh = (up(yh_ref) * _silu(up(hg_ref))).astype(jnp.bfloat16)
    acc = jnp.dot(yh, w_ref[0], preferred_element_type=f32)
    ym = (up(mf_ref) + up(mb_ref) + up(mx_ref) * dsk_ref[...]) * _silu(up(mz_ref))
    ym = (_group_rmsnorm(ym, MB_GW) * mnw_ref[...]).astype(jnp.bfloat16)
    acc += jnp.dot(ym, w_ref[1], preferred_element_type=f32)
    yl = (up(lf_ref) + up(lb_ref)) * jax.nn.sigmoid(up(lo_ref))
    yl = (_group_rmsnorm(yl, ML_DH) * lnw_ref[...] * _silu(up(lz_ref))).astype(jnp.bfloat16)
    acc += jnp.dot(yl, w_ref[2], preferred_element_type=f32)
    acc += jnp.dot(yn_ref[0], w_ref[3], preferred_element_type=f32)
    o_ref[0] = x_ref[0] + acc


def _outproj2(x, y_hy, proj, y_mf, y_mb, h_f, h_b, y_na, dskip, mb_nw, ml_nw, w_out):
    B, L, D = x.shape
    tm = min(512, L)
    tok = lambda b, i: (b, i, 0)
    col = lambda c: (lambda b, i: (b, i, c))
    blk = (1, tm, GROUP_W)
    vec = pl.BlockSpec((1, GROUP_W), lambda b, i: (0, 0))
    return pl.pallas_call(
        _outproj2_kernel,
        grid=(B, L // tm),
        in_specs=[
            pl.BlockSpec((1, tm, D), tok),
            pl.BlockSpec(blk, tok), pl.BlockSpec(blk, col(T_HY_G)),
            pl.BlockSpec(blk, tok), pl.BlockSpec(blk, tok), pl.BlockSpec(blk, col(T_MB_X)), pl.BlockSpec(blk, col(T_MB_Z)),
            pl.BlockSpec(blk, tok), pl.BlockSpec(blk, tok), pl.BlockSpec(blk, col(T_ML_O)), pl.BlockSpec(blk, col(T_ML_Z)),
            pl.BlockSpec(blk, tok),
            vec, vec, vec,
            pl.BlockSpec((4, GROUP_W, D), lambda b, i: (0, 0, 0)),
        ],
        out_specs=pl.BlockSpec((1, tm, D), tok),
        out_shape=jax.ShapeDtypeStruct((B, L, D), jnp.float32),
        compiler_params=pltpu.CompilerParams(
            dimension_semantics=("arbitrary", "arbitrary"), vmem_limit_bytes=VMEM_LIMIT),
        name="outproj",
    )(x, y_hy, proj, y_mf, y_mb, proj, proj, h_f, h_b, proj, proj, y_na, dskip, mb_nw, ml_nw, w_out)


GRID_W = 64
NA_KR = 8
NA_KC = 16
NA_H = 8
NA_DH = 64
ROWS_PER_BLOCK = 8
BLOCK_TOK = ROWS_PER_BLOCK * GRID_W


def _na_kernel(q_ref, kp_ref, kc_ref, kn_ref, vp_ref, vc_ref, vn_ref, g_ref, tbl_ref, o_ref, k_s, v_s, *, rows):
    mblk = pl.program_id(1)
    npair = NA_H // 2
    pw = 2 * NA_DH
    ones = jnp.ones((BLOCK_TOK, pw), jnp.bfloat16)
    for t, (kr, vr) in enumerate(((kp_ref, vp_ref), (kc_ref, vc_ref), (kn_ref, vn_ref))):
        k_s[pl.ds(t * BLOCK_TOK, BLOCK_TOK), :] = kr[0]
        for hp in range(npair):
            v_s[pl.ds(t * BLOCK_TOK, BLOCK_TOK), 2 * hp * pw:(2 * hp + 1) * pw] = vr[0, :, hp * pw:(hp + 1) * pw]
            v_s[pl.ds(t * BLOCK_TOK, BLOCK_TOK), (2 * hp + 1) * pw:(2 * hp + 2) * pw] = ones
    lane = lax.broadcasted_iota(jnp.int32, (GRID_W, pw), 1)
    first = lane < NA_DH
    win = NA_KR * GRID_W
    for j in range(ROWS_PER_BLOCK):
        r = mblk * ROWS_PER_BLOCK + j
        rs = jnp.clip(r - NA_KR // 2, 0, rows - NA_KR)
        didx = r - rs
        off = pl.multiple_of((rs - (mblk - 1) * ROWS_PER_BLOCK) * GRID_W, GRID_W)
        ss = []
        for hp in range(npair):
            ls = slice(hp * pw, (hp + 1) * pw)
            qp = q_ref[0, j * GRID_W:(j + 1) * GRID_W, ls]
            zero = jnp.zeros_like(qp)
            q2 = jnp.concatenate([jnp.where(first, qp, zero), jnp.where(first, zero, qp)], axis=0)
            ss.append(_dot_nt(q2, k_s[pl.ds(off, win), ls]))
        s = jnp.concatenate(ss, axis=0) + tbl_ref[didx].astype(jnp.float32)
        e = jnp.exp2(s - jnp.max(s, axis=-1, keepdims=True)).astype(jnp.bfloat16)
        outs = []
        for hp in range(npair):
            ov = jnp.dot(e[hp * pw:(hp + 1) * pw], v_s[pl.ds(off, win), 2 * hp * pw:(2 * hp + 2) * pw],
                         preferred_element_type=jnp.float32)
            o2 = ov[:, :pw] / ov[:, pw:]
            outs.append(jnp.where(first, o2[:GRID_W], o2[GRID_W:]))
        o = jnp.concatenate(outs, axis=-1)
        gate = _silu(g_ref[0, j * GRID_W:(j + 1) * GRID_W, :].astype(jnp.float32))
        o_ref[0, j * GRID_W:(j + 1) * GRID_W, :] = (o * gate).astype(o_ref.dtype)


def _na(proj, tbl):
    B, L, _ = proj.shape
    rows = L // GRID_W
    nb = rows // ROWS_PER_BLOCK
    prev = lambda col: (lambda b, m: (b, jnp.maximum(m - 1, 0), col))
    cur = lambda col: (lambda b, m: (b, m, col))
    nxt = lambda col: (lambda b, m: (b, jnp.minimum(m + 1, nb - 1), col))
    blk = (1, BLOCK_TOK, GROUP_W)
    return pl.pallas_call(
        functools.partial(_na_kernel, rows=rows),
        grid=(B, nb),
        in_specs=[
            pl.BlockSpec(blk, cur(T_NA_Q)),
            pl.BlockSpec(blk, prev(T_NA_K)), pl.BlockSpec(blk, cur(T_NA_K)), pl.BlockSpec(blk, nxt(T_NA_K)),
            pl.BlockSpec(blk, prev(T_NA_V)), pl.BlockSpec(blk, cur(T_NA_V)), pl.BlockSpec(blk, nxt(T_NA_V)),
            pl.BlockSpec(blk, cur(T_NA_G)),
            pl.BlockSpec(tbl.shape, lambda b, m: (0, 0, 0)),
        ],
        out_specs=pl.BlockSpec(blk, cur(0)),
        out_shape=jax.ShapeDtypeStruct((B, L, GROUP_W), jnp.bfloat16),
        scratch_shapes=[
            pltpu.VMEM((3 * BLOCK_TOK, GROUP_W), jnp.bfloat16),
            pltpu.VMEM((3 * BLOCK_TOK, 2 * GROUP_W), jnp.bfloat16),
        ],
        compiler_params=pltpu.CompilerParams(
            dimension_semantics=("arbitrary", "arbitrary"), vmem_limit_bytes=VMEM_LIMIT),
        name="nbr_attn",
    )(proj, proj, proj, proj, proj, proj, proj, proj, tbl)


def _na_bias_table(rpb):
    nco = 2 * NA_KC - 1
    rows_d = jnp.stack([rpb[:, NA_KR - 1 - d:2 * NA_KR - 1 - d, :] for d in range(NA_KR)])
    lpad = GRID_W - NA_KC
    ext = jnp.pad(rows_d * math.log2(math.e), ((0, 0), (0, 0), (0, 0), (lpad, 2 * GRID_W - lpad - nco)))
    lead = ext.shape[:3]
    skew = jnp.broadcast_to(ext[..., None, :], lead + (GRID_W, 2 * GRID_W)).reshape(lead + (2 * GRID_W * GRID_W,))
    skew = skew[..., :GRID_W * (2 * GRID_W - 1)].reshape(lead + (GRID_W, 2 * GRID_W - 1))[..., GRID_W - 1:]
    w = jnp.arange(GRID_W)
    c = jnp.arange(GRID_W)
    cstart = jnp.clip(w - NA_KC // 2, 0, GRID_W - NA_KC)
    inwin = (c[None, :] >= cstart[:, None]) & (c[None, :] < cstart[:, None] + NA_KC)
    t = jnp.where(inwin, skew, NEG)
    t = t.transpose(0, 1, 3, 2, 4).reshape(NA_KR, NA_H * GRID_W, NA_KR * GRID_W)
    return t.astype(jnp.bfloat16)


HY_LANES = 128
F2N = 128
TW_B = 16
PAD_ROWS = 8
HY_POS_PAD = 128


def _cmul(ar, ai, br, bi):
    return ar * br - ai * bi, ar * bi + ai * br


def _stage_a_store(buf_ref, a, n2, tr, ti, n1f, rs):
    ar, ai = _cmul(a[:n1f], a[n1f:], tr, ti)
    base = pl.multiple_of(n2 * rs, 8)
    buf_ref[pl.ds(base, n1f), :] = ar
    buf_ref[pl.ds(base + n1f, n1f), :] = ai


def _stage_b_load(buf_ref, k1, n1f, rs):
    yr = buf_ref[pl.ds(k1, F2N, stride=rs), :]
    yi = buf_ref[pl.ds(n1f + k1, F2N, stride=rs), :]
    return jnp.concatenate([yr, yi], axis=0).astype(jnp.bfloat16)


def _hy_spec_kernel(ff_ref, fb_ref, mg_ref, tbr_ref, tbi_ref, f2_ref, o_ref, buf_ref, *, n1f, h1, rs, ns):
    s = pl.program_id(2)

    @pl.when(s < ns)
    def _():
        for b in range(TW_B):
            n2 = s * TW_B + b
            hf = ff_ref[pl.ds(n2, h1, stride=F2N), :]
            hb = fb_ref[pl.ds(F2N - n2, h1, stride=F2N), :]
            xs = jnp.concatenate([hf, hb], axis=0).astype(jnp.bfloat16)
            a = jnp.dot(mg_ref[0], xs, preferred_element_type=jnp.float32)
            _stage_a_store(buf_ref, a, n2, tbr_ref[b], tbi_ref[b], n1f, rs)

    @pl.when(s >= ns)
    def _():
        for kk in range(TW_B):
            k1 = (s - ns) * TW_B + kk
            o_ref[0, kk] = jnp.dot(f2_ref[0], _stage_b_load(buf_ref, k1, n1f, rs), preferred_element_type=jnp.float32)


def _hy_spectrum(filt, mg, tbr, tbi, f2, L):
    n1f = 2 * L // F2N
    h1 = n1f // 2
    rs = 2 * n1f + PAD_ROWS
    ns = F2N // TW_B
    nk = n1f // TW_B
    ncb = GROUP_W // HY_LANES
    a_idx = lambda o, c, s: (jnp.minimum(s, ns - 1), 0, 0)
    return pl.pallas_call(
        functools.partial(_hy_spec_kernel, n1f=n1f, h1=h1, rs=rs, ns=ns),
        grid=(2, ncb, ns + nk),
        in_specs=[
            pl.BlockSpec((L + F2N, HY_LANES), lambda o, c, s: (0, o * 2 * ncb + c)),
            pl.BlockSpec((L + F2N, HY_LANES), lambda o, c, s: (0, o * 2 * ncb + ncb + c)),
            pl.BlockSpec((1, 2 * n1f, n1f), a_idx),
            pl.BlockSpec(tbr.shape, lambda o, c, s: (0, 0, 0)),
            pl.BlockSpec(tbi.shape, lambda o, c, s: (0, 0, 0)),
            pl.BlockSpec((1, 2 * F2N, 2 * F2N), lambda o, c, s: (0, 0, 0)),
        ],
        out_specs=pl.BlockSpec((1, TW_B, 2 * F2N, HY_LANES), lambda o, c, s: (o, jnp.maximum(s - ns, 0), 0, c)),
        out_shape=jax.ShapeDtypeStruct((2, n1f, 2 * F2N, GROUP_W), jnp.float32),
        scratch_shapes=[pltpu.VMEM((F2N * rs, HY_LANES), jnp.float32)],
        compiler_params=pltpu.CompilerParams(
            dimension_semantics=("arbitrary", "arbitrary", "arbitrary"), vmem_limit_bytes=VMEM_LIMIT),
        name="hyena_spectrum",
    )(filt, filt, mg, tbr, tbi, f2)


def _hy_conv_kernel(v_ref, x1_ref, x2_ref, g_ref, m1_ref, m3_ref, tbr_ref, tbi_ref, f2_ref, skip_ref,
                    o_ref, buf_ref, z_ref, *, n1f, h1, rs, ns, nk):
    s = pl.program_id(2)
    p1 = ns
    p2 = p1 + nk
    p3 = p2 + ns
    p4 = p3 + nk

    def stage_a(xs, n2, b):
        a = jnp.dot(m1_ref[0], xs, preferred_element_type=jnp.float32)
        _stage_a_store(buf_ref, a, n2, tbr_ref[b], tbi_ref[b], n1f, rs)

    def stage_b(order, kb):
        for kk in range(TW_B):
            k1 = kb * TW_B + kk
            y = jnp.dot(f2_ref[0], _stage_b_load(buf_ref, k1, n1f, rs), preferred_element_type=jnp.float32)
            g = g_ref[0, kk]
            yr, yi = _cmul(y[:F2N], y[F2N:], g[:F2N], g[F2N:])
            ys = jnp.concatenate([yr, yi], axis=0).astype(jnp.bfloat16)
            z = jnp.dot(f2_ref[1], ys, preferred_element_type=jnp.float32)
            buf_ref[pl.ds(k1, F2N, stride=rs), :] = z[:F2N]
            buf_ref[pl.ds(n1f + k1, F2N, stride=rs), :] = z[F2N:]

    def stage_c(n2, b):
        base = pl.multiple_of(n2 * rs, 8)
        zr = buf_ref[pl.ds(base, n1f), :]
        zi = buf_ref[pl.ds(base + n1f, n1f), :]
        wr, wi = _cmul(zr, zi, tbr_ref[b], -tbi_ref[b])
        ws = jnp.concatenate([wr, wi], axis=0).astype(jnp.bfloat16)
        return jnp.dot(m3_ref[0], ws, preferred_element_type=jnp.float32)

    @pl.when(s < p1)
    def _():
        for b in range(TW_B):
            xs = jnp.concatenate([v_ref[0, b], v_ref[1, b]], axis=0)
            stage_a(xs, s * TW_B + b, b)

    @pl.when((s >= p1) & (s < p2))
    def _():
        stage_b(0, s - p1)

    @pl.when((s >= p2) & (s < p3))
    def _():
        for b in range(TW_B):
            n2 = (s - p2) * TW_B + b
            c = stage_c(n2, b)
            zs = []
            for r in range(2):
                vv = v_ref[r, b].astype(jnp.float32)
                z = x1_ref[r, b].astype(jnp.float32) * (c[r * h1:(r + 1) * h1] + skip_ref[0:1, :] * vv)
                z_ref[n2, r] = z
                zs.append(z)
            stage_a(jnp.concatenate(zs, axis=0).astype(jnp.bfloat16), n2, b)

    @pl.when((s >= p3) & (s < p4))
    def _():
        stage_b(1, s - p3)

    @pl.when(s >= p4)
    def _():
        for b in range(TW_B):
            n2 = (s - p4) * TW_B + b
            c = stage_c(n2, b)
            for r in range(2):
                y = x2_ref[r, b].astype(jnp.float32) * (c[r * h1:(r + 1) * h1] + skip_ref[1:2, :] * z_ref[n2, r])
                o_ref[r, b] = y.astype(o_ref.dtype)


def _hy_conv(hyp, spec, m1, m3, tbr, tbi, f2, skip, L):
    B = hyp.shape[0]
    n1f = 2 * L // F2N
    h1 = n1f // 2
    rs = 2 * n1f + PAD_ROWS
    ns = F2N // TW_B
    nk = n1f // TW_B
    ncb = GROUP_W // HY_LANES
    p1, p2, p3, p4 = ns, ns + nk, 2 * ns + nk, 2 * ns + 2 * nk
    clip = lambda v, hi: jnp.clip(v, 0, hi)
    tblk = (2, TW_B, h1, HY_LANES)
    v_idx = lambda c, p, s: (p, jnp.where(s < p1, s, clip(s - p2, ns - 1)), 0, c)
    x1_idx = lambda c, p, s: (p, clip(s - p2, ns - 1), 0, ncb + c)
    x2_idx = lambda c, p, s: (p, clip(s - p4, ns - 1), 0, 2 * ncb + c)
    g_idx = lambda c, p, s: (jnp.where(s < p3, 0, 1), jnp.where(s < p3, clip(s - p1, nk - 1), clip(s - p3, nk - 1)), 0, c)
    m1_idx = lambda c, p, s: (jnp.where(s < p1, s, clip(s - p2, ns - 1)), 0, 0)
    m3_idx = lambda c, p, s: (jnp.where(s < p4, clip(s - p2, ns - 1), s - p4), 0, 0)
    return pl.pallas_call(
        functools.partial(_hy_conv_kernel, n1f=n1f, h1=h1, rs=rs, ns=ns, nk=nk),
        grid=(ncb, B // 2, p4 + ns),
        in_specs=[
            pl.BlockSpec(tblk, v_idx),
            pl.BlockSpec(tblk, x1_idx),
            pl.BlockSpec(tblk, x2_idx),
            pl.BlockSpec((1, TW_B, 2 * F2N, HY_LANES), g_idx),
            pl.BlockSpec((1, 2 * n1f, 2 * h1), m1_idx),
            pl.BlockSpec((1, 2 * h1, 2 * n1f), m3_idx),
            pl.BlockSpec(tbr.shape, lambda c, p, s: (0, 0, 0)),
            pl.BlockSpec(tbi.shape, lambda c, p, s: (0, 0, 0)),
            pl.BlockSpec((2, 2 * F2N, 2 * F2N), lambda c, p, s: (0, 0, 0)),
            pl.BlockSpec((2, HY_LANES), lambda c, p, s: (0, c)),
        ],
        out_specs=pl.BlockSpec(tblk, lambda c, p, s: (p, clip(s - p4, ns - 1), 0, c)),
        out_shape=jax.ShapeDtypeStruct((B, F2N, h1, GROUP_W), jnp.bfloat16),
        scratch_shapes=[
            pltpu.VMEM((F2N * rs, HY_LANES), jnp.float32),
            pltpu.VMEM((F2N, 2, h1, HY_LANES), jnp.float32),
        ],
        compiler_params=pltpu.CompilerParams(
            dimension_semantics=("arbitrary", "arbitrary", "arbitrary"), vmem_limit_bytes=VMEM_LIMIT),
        name="hyena_conv",
    )(hyp, hyp, hyp, spec, m1, m3, tbr, tbi, f2, skip)


def _hy_filter_kernel(pos_ref, w1_ref, b1_ref, w2_ref, b2_ref, w3_ref, fr_ref, dec_ref, o_ref, *, nblk):
    i = pl.program_id(0)
    pos = pos_ref[...]
    fr = fr_ref[...]
    hid = jnp.sin(fr * (jnp.dot(pos, w1_ref[...], precision=_HI, preferred_element_type=jnp.float32) + b1_ref[...]))
    hid = jnp.sin(fr * (jnp.dot(hid, w2_ref[...], precision=_HI, preferred_element_type=jnp.float32) + b2_ref[...]))
    filt = jnp.dot(hid, w3_ref[...], precision=_HI, preferred_element_type=jnp.float32)
    filt = filt * jnp.exp(-pos[:, 0:1] * dec_ref[...])
    o_ref[...] = jnp.where(i < nblk, filt, 0.0)


def _hy_filters(pos, w1, b1, w2, b2, w3, freq, decay, L):
    nblk = L // F2N
    nh = w2.shape[0]
    no = w3.shape[1]
    const = lambda i: (0, 0)
    return pl.pallas_call(
        functools.partial(_hy_filter_kernel, nblk=nblk),
        grid=(nblk + 1,),
        in_specs=[
            pl.BlockSpec((F2N, HY_POS_PAD), lambda i: (jnp.minimum(i, nblk - 1), 0)),
            pl.BlockSpec((HY_POS_PAD, nh), const), pl.BlockSpec((1, nh), const),
            pl.BlockSpec((nh, nh), const), pl.BlockSpec((1, nh), const),
            pl.BlockSpec((nh, no), const), pl.BlockSpec((1, nh), const), pl.BlockSpec((1, no), const),
        ],
        out_specs=pl.BlockSpec((F2N, no), lambda i: (i, 0)),
        out_shape=jax.ShapeDtypeStruct((L + F2N, no), jnp.float32),
        compiler_params=pltpu.CompilerParams(dimension_semantics=("arbitrary",), vmem_limit_bytes=VMEM_LIMIT),
        name="hyena_filters",
    )(pos, w1, b1, w2, b2, w3, freq, decay)


def _hy_constants(L):
    n = 2 * L
    n1f = n // F2N
    h1 = n1f // 2
    na = F2N // TW_B

    def cis(num, den):
        ang = (-2.0 * math.pi / den) * (num % den).astype(jnp.float32)
        return jnp.cos(ang), jnp.sin(ang)

    k1 = jnp.arange(n1f)
    n1 = jnp.arange(n1f)
    a = jnp.arange(na)
    f1r, f1i = cis(k1[:, None] * n1[None, :], n1f)
    tar, tai = cis(k1[None, :] * (TW_B * a)[:, None], n)
    mr, mi = _cmul(f1r[None], f1i[None], tar[:, :, None], tai[:, :, None])

    def blockform(r, i):
        return jnp.concatenate([jnp.concatenate([r, -i], axis=-1), jnp.concatenate([i, r], axis=-1)], axis=-2)

    m1 = blockform(mr[:, :, :h1], mi[:, :, :h1]).astype(jnp.bfloat16)
    m3r = jnp.swapaxes(mr[:, :, :h1], 1, 2) / n
    m3i = -jnp.swapaxes(mi[:, :, :h1], 1, 2) / n
    m3 = blockform(m3r, m3i).astype(jnp.bfloat16)
    mgr = jnp.concatenate([mr[:, :, :h1], mr[:, :, h1:][:, :, ::-1]], axis=-1)
    mgi = jnp.concatenate([mi[:, :, :h1], mi[:, :, h1:][:, :, ::-1]], axis=-1)
    mg = jnp.concatenate([mgr, mgi], axis=1).astype(jnp.bfloat16)
    b = jnp.arange(TW_B)
    tbr, tbi = cis(k1[None, :] * b[:, None], n)
    tbr = jnp.broadcast_to(tbr[:, :, None], (TW_B, n1f, HY_LANES))
    tbi = jnp.broadcast_to(tbi[:, :, None], (TW_B, n1f, HY_LANES))
    k2 = jnp.arange(F2N)
    f2r, f2i = cis(k2[:, None] * k2[None, :], F2N)
    f2 = jnp.stack([blockform(f2r, f2i), blockform(f2r, -f2i)]).astype(jnp.bfloat16)
    return m1, m3, mg, tbr, tbi, f2


def _hy_positions(L):
    t = jnp.arange(L, dtype=jnp.float32)
    bands = jnp.arange(1, 9, dtype=jnp.float32)
    ang = (2.0 * math.pi / L) * t[:, None] * bands[None, :]
    pos = jnp.concatenate([(t / L)[:, None], jnp.cos(ang), jnp.sin(ang)], axis=-1)
    return jnp.pad(pos, ((0, 0), (0, HY_POS_PAD - pos.shape[1])))


MAIN_COL_RANGES = ((0, 1536), (2048, 3072), (3600, 4624), (1536, 2048), (3072, 3584), (4624, 6160), (6176, 8224))
SMALL_COL_RANGES = ((3584, 3600), (6160, 6176))


def _take_cols(a, ranges):
    return jnp.concatenate([a[..., lo:hi] for lo, hi in ranges], axis=-1)


def _pad_cols(a, width, left=0):
    return jnp.pad(a, [(0, 0)] * (a.ndim - 1) + [(left, width - left - a.shape[-1])])


def kernel(x, norm_w, w_in, w_out, hy_conv_w, hy_conv_b, hy_w1, hy_b1, hy_w2, hy_b2, hy_w3, hy_freq, hy_decay,
           hy_skip, mb_conv_w, mb_conv_b, mb_dt_bias, mb_a_log, mb_d, mb_norm_w, ml_conv_w, ml_conv_b, ml_gate_b,
           ml_norm_w, na_qnorm_w, na_knorm_w, na_rpb):
    B, L, D = x.shape
    depth = w_in.shape[0]
    ncols = N_TILES * TILE_N
    f32 = jnp.float32
    pos = _hy_positions(L)
    m1, m3, mg, tbr, tbi, f2 = _hy_constants(L)
    gmean = jnp.kron(jnp.eye(NA_H, dtype=f32), jnp.full((NA_DH, NA_DH), 1.0 / NA_DH)).astype(jnp.bfloat16)

    def expansion(first_lane, heads, width):
        tgt = jnp.arange(heads * width) // width
        e = [(jnp.arange(SMALL_W)[:, None] == (first_lane + d * heads + tgt)[None, :]) for d in range(2)]
        e = jnp.stack(e).astype(jnp.bfloat16)
        return jnp.concatenate([e, e], axis=1)

    ee_mb = expansion(0, MB_H, MB_P)
    ee2_mb = expansion(0, MB_H, CHUNK)
    ee_ml = jnp.concatenate([expansion(ML_LANE, ML_H, ML_DH), expansion(ML_LANE, ML_H, CHUNK)], axis=2)
    for l in range(depth):
        w = _take_cols(w_in[l], MAIN_COL_RANGES).astype(jnp.bfloat16)
        wl = w_in[l]
        zeros = lambda n: jnp.zeros((D, n), wl.dtype)
        ws = jnp.concatenate([wl[:, 3584:3600], wl[:, 6160:6164], wl[:, 6168:6172], zeros(SMALL_W - 24),
                              zeros(ML_LANE), wl[:, 6164:6168], wl[:, 6172:6176], zeros(SMALL_W - 24)],
                             axis=1).astype(jnp.bfloat16)
        taps = _pad_cols(jnp.concatenate([hy_conv_w[l], mb_conv_w[l], ml_conv_w[l]], axis=-1), ncols)
        cbias = _pad_cols(jnp.concatenate([hy_conv_b[l], mb_conv_b[l], ml_conv_b[l]])[None], ncols)
        qk_scale = jnp.concatenate([jnp.tile(na_qnorm_w[l], NA_H) * (NA_DH ** -0.5 * math.log2(math.e)),
                                    jnp.tile(na_knorm_w[l], NA_H)])[None]
        cscale = _pad_cols(qk_scale, ncols, left=(T_NA_Q + N_HY_TILES) * TILE_N)
        proj, hy, small = _inproj(x, norm_w[l][None], w, ws, taps, cbias, cscale, gmean)

        filt = _hy_filters(pos, _pad_cols(hy_w1[l].T, HY_POS_PAD).T, hy_b1[l][None], hy_w2[l], hy_b2[l][None],
                           hy_w3[l], hy_freq[l][None], hy_decay[l][None], L)
        spec = _hy_spectrum(filt, mg, tbr, tbi, f2[:1], L)
        hyp = hy.reshape(B, L // F2N, F2N, N_HY_TILES * GROUP_W).transpose(0, 2, 1, 3)
        y_hy = _hy_conv(hyp, spec, m1, m3, tbr, tbi, f2, hy_skip[l], L)
        y_hy = y_hy.transpose(0, 2, 1, 3).reshape(B, L, GROUP_W)

        a_log2 = -jnp.exp(mb_a_log[l].astype(f32)) * LOG2E
        dt_bias = jnp.concatenate([_pad_cols(mb_dt_bias[l, d][None], SMALL_W, left=d * MB_H) for d in range(2)])
        a_rows = jnp.concatenate([_pad_cols(a_log2[d][None], SMALL_W, left=d * MB_H) for d in range(2)])
        y_mf, y_mb = _ssd2(proj, small, dt_bias, a_rows, ee_mb, ee2_mb)

        bias_i = _pad_cols(ml_gate_b[l][:, 0, :].reshape(1, -1), SMALL_W, left=ML_LANE)
        bias_f = _pad_cols(ml_gate_b[l][:, 1, :].reshape(1, -1), SMALL_W, left=ML_LANE)
        h_f, h_b = _mlstm2(proj, small, bias_i, bias_f, ee_ml)

        y_na = _na(proj, _na_bias_table(na_rpb[l]))

        x = _outproj2(x, y_hy, proj, y_mf, y_mb, h_f, h_b, y_na, jnp.repeat(mb_d[l], MB_P)[None], mb_norm_w[l][None],
                      ml_norm_w[l][None], w_out[l].astype(jnp.bfloat16).reshape(4, GROUP_W, D))
    return x
```

```python
import functools
import math

import jax
import jax.numpy as jnp
from jax import lax
from jax.experimental import pallas as pl
from jax.experimental.pallas import tpu as pltpu

RMS_EPS = 1e-6
GROUP_W = 512
TILE_N = 512
SMALL_W = 128
SMALL_OUT = 2 * SMALL_W
HALO = 16
INPROJ_ROWS = 512
VMEM_LIMIT = 56 * 1024 * 1024

N_HY_TILES = 3
N_CONV_TILES = 7
N_TILES = 16
T_MB_X, T_MB_BC = 0, 1
T_ML_Q, T_ML_K = 2, 3
T_HY_G, T_MB_Z, T_ML_V, T_ML_O, T_ML_Z = 4, 5, 6, 7, 8
T_NA_Q, T_NA_K, T_NA_V, T_NA_G = 9, 10, 11, 12
N_MAIN_TILES = N_TILES - N_HY_TILES

_HI = lax.Precision.HIGHEST


def _silu(x):
    return x * jax.nn.sigmoid(x)


def _inproj_kernel(x_ref, xp_ref, xn_ref, nw_ref, w_ref, ws_ref, taps_ref, cb_ref, cs_ref, gm_ref,
                   o_ref, oh_ref, os_ref, h_ref, acc_ref, *, tm):
    i = pl.program_id(1)
    ni = pl.num_programs(1)

    def norm(xv):
        ms = jnp.mean(xv * xv, axis=-1, keepdims=True)
        return (xv * lax.rsqrt(ms + RMS_EPS) * nw_ref[...]).astype(jnp.bfloat16)

    h_ref[pl.ds(HALO, tm), :] = norm(x_ref[0])
    hp = norm(xp_ref[0])
    hn = norm(xn_ref[0])
    h_ref[pl.ds(0, HALO), :] = jnp.where(i == 0, jnp.zeros_like(hp), hp)
    h_ref[pl.ds(HALO + tm, HALO), :] = jnp.where(i == ni - 1, jnp.zeros_like(hn), hn)
    os_ref[0] = jnp.dot(h_ref[pl.ds(HALO, tm), :], ws_ref[...], preferred_element_type=jnp.float32)

    qk_tiles = (T_NA_Q + N_HY_TILES, T_NA_K + N_HY_TILES)

    def matmul(u):
        wt = w_ref[:, u * TILE_N:(u + 1) * TILE_N]
        if u < N_CONV_TILES:
            acc_ref[u % 2] = jnp.dot(h_ref[...], wt, preferred_element_type=jnp.float32)
        else:
            acc_ref[u % 2, pl.ds(0, tm), :] = jnp.dot(h_ref[pl.ds(HALO, tm), :], wt, preferred_element_type=jnp.float32)

    def epilogue(u):
        a = acc_ref.at[u % 2]
        cols = slice(u * TILE_N, (u + 1) * TILE_N)
        if u < N_CONV_TILES:
            t = taps_ref[:, cols]
            y = (a[pl.ds(HALO - 1, tm), :] * t[0:1] + a[pl.ds(HALO, tm), :] * t[1:2]
                 + a[pl.ds(HALO + 1, tm), :] * t[2:3] + cb_ref[:, cols])
            if u < N_HY_TILES:
                oh_ref[0, :, cols] = y.astype(oh_ref.dtype)
                return
            y = _silu(y)
        else:
            y = a[pl.ds(0, tm), :]
            if u in qk_tiles:
                ms = jnp.dot((y * y).astype(jnp.bfloat16), gm_ref[...], preferred_element_type=jnp.float32)
                y = y * lax.rsqrt(ms + RMS_EPS) * cs_ref[:, cols]
        o_ref[0, :, (u - N_HY_TILES) * TILE_N:(u - N_HY_TILES + 1) * TILE_N] = y.astype(o_ref.dtype)

    matmul(0)
    for u in range(N_TILES):
        if u + 1 < N_TILES:
            matmul(u + 1)
        epilogue(u)


def _inproj(x, nw, w, ws, taps, cbias, cscale, gmean):
    B, L, D = x.shape
    tm = min(INPROJ_ROWS, L)
    ni = L // tm
    hb = tm // HALO
    nlast = L // HALO - 1
    ncols = N_TILES * TILE_N
    const = lambda b, i: (0, 0)
    tok = lambda b, i: (b, i, 0)
    return pl.pallas_call(
        functools.partial(_inproj_kernel, tm=tm),
        grid=(B, ni),
        in_specs=[
            pl.BlockSpec((1, tm, D), tok),
            pl.BlockSpec((1, HALO, D), lambda b, i: (b, jnp.maximum(i * hb - 1, 0), 0)),
            pl.BlockSpec((1, HALO, D), lambda b, i: (b, jnp.minimum((i + 1) * hb, nlast), 0)),
            pl.BlockSpec((1, D), const),
            pl.BlockSpec((D, ncols), const, pipeline_mode=pl.Buffered(1)),
            pl.BlockSpec((D, SMALL_OUT), const),
            pl.BlockSpec((3, ncols), const),
            pl.BlockSpec((1, ncols), const),
            pl.BlockSpec((1, ncols), const),
            pl.BlockSpec((TILE_N, TILE_N), const),
        ],
        out_specs=[
            pl.BlockSpec((1, tm, N_MAIN_TILES * TILE_N), tok),
            pl.BlockSpec((1, tm, N_HY_TILES * TILE_N), tok),
            pl.BlockSpec((1, tm, SMALL_OUT), tok),
        ],
        out_shape=[
            jax.ShapeDtypeStruct((B, L, N_MAIN_TILES * TILE_N), jnp.bfloat16),
            jax.ShapeDtypeStruct((B, L, N_HY_TILES * TILE_N), jnp.bfloat16),
            jax.ShapeDtypeStruct((B, L, SMALL_OUT), jnp.float32),
        ],
        scratch_shapes=[
            pltpu.VMEM((tm + 2 * HALO, D), jnp.bfloat16),
            pltpu.VMEM((2, tm + 2 * HALO, TILE_N), jnp.float32),
        ],
        compiler_params=pltpu.CompilerParams(
            dimension_semantics=("arbitrary", "arbitrary"), vmem_limit_bytes=VMEM_LIMIT),
        name="inproj",
    )(x, x, x, nw, w, ws, taps, cbias, cscale, gmean)


CHUNK = 256
NEG = -1e30


def _dot_nt(a, b):
    return lax.dot_general(a, b, (((1,), (1,)), ((), ())), preferred_element_type=jnp.float32)


def _dot_tn(a, b):
    return lax.dot_general(a, b, (((0,), (0,)), ((), ())), preferred_element_type=jnp.float32)


def _chunk_mask(q, reverse):
    t = lax.broadcasted_iota(jnp.int32, (q, q), 0)
    s = lax.broadcasted_iota(jnp.int32, (q, q), 1)
    return (s >= t) if reverse else (s <= t)


MB_H = 8
MB_P = 64
MB_N = 128
MB_GW = 256
ML_H = 4
ML_DH = 128


LOG2E = math.log2(math.e)


def _split_cols(x):
    hi = x.astype(jnp.bfloat16)
    lo = (x - hi.astype(jnp.float32)).astype(jnp.bfloat16)
    return jnp.concatenate([hi, lo], axis=1)


def _split_rows(x):
    hi = x.astype(jnp.bfloat16)
    lo = (x - hi.astype(jnp.float32)).astype(jnp.bfloat16)
    return jnp.concatenate([hi, lo], axis=0)


def _tri2(mask):
    tri = mask.astype(jnp.bfloat16)
    return jnp.concatenate([tri, tri], axis=1)


def _ssd_direction(xs_ref, bc_ref, sm_ref, bias, a_row, ee, ee2, s_ref, o_ref, *, reverse, dcol):
    q = CHUNK
    mask = _chunk_mask(q, reverse)
    dt_s = jax.nn.softplus(sm_ref[0] + bias)
    a_s = dt_s * a_row
    c_s = jnp.dot(_tri2(mask), _split_rows(a_s), preferred_element_type=jnp.float32)
    c_t = c_s.T
    full = jnp.dot(_split_cols(jnp.concatenate([c_s, dt_s], axis=0)), ee, preferred_element_type=jnp.float32)
    c_full, dt_full = full[:q], full[q:]
    cc_all = jnp.dot(_split_cols(c_s), ee2, preferred_element_type=jnp.float32)
    far = 0 if reverse else q - 1
    tot_full = c_full[far:far + 1, :]
    x = xs_ref[0].astype(jnp.float32)
    dtx = dt_full * x
    lane = lax.broadcasted_iota(jnp.int32, (1, MB_GW), 1)
    ys = []
    yield
    for g in range(2):
        bg = bc_ref[0, :, g * MB_N:(g + 1) * MB_N]
        cg = bc_ref[0, :, MB_GW + g * MB_N:MB_GW + (g + 1) * MB_N]
        gs = slice(g * MB_GW, (g + 1) * MB_GW)
        gram = _dot_nt(cg, bg)
        dtx_g = dtx[:, gs]
        ms, xms = [], []
        for j in range(4):
            hd = g * 4 + j
            col = dcol + hd
            decay = jnp.exp2(jnp.where(mask, cc_all[:, hd * q:(hd + 1) * q] - c_t[col:col + 1, :], NEG))
            ms.append((gram * decay).astype(jnp.bfloat16))
            xms.append(jnp.where((lane >= j * MB_P) & (lane < (j + 1) * MB_P), dtx_g, 0.0).astype(jnp.bfloat16))
        yg = jnp.dot(jnp.concatenate(ms, axis=1), jnp.concatenate(xms, axis=0), preferred_element_type=jnp.float32)
        s_g = s_ref[:, gs]
        yg = yg + jnp.exp2(c_full[:, gs]) * jnp.dot(cg, s_g.astype(jnp.bfloat16), preferred_element_type=jnp.float32)
        w = jnp.exp2(tot_full[:, gs] - c_full[:, gs])
        s_ref[:, gs] = jnp.exp2(tot_full[:, gs]) * s_g + _dot_tn(bg, (w * dtx_g).astype(jnp.bfloat16))
        ys.append(yg)
        yield
    o_ref[0] = jnp.concatenate(ys, axis=-1).astype(o_ref.dtype)


def _interleave(*gens):
    live = list(gens)
    while live:
        for g in list(live):
            if next(g, StopIteration) is StopIteration:
                live.remove(g)


def _ssd2_kernel(xf_ref, bcf_ref, smf_ref, xb_ref, bcb_ref, smb_ref, bias_ref, a_ref, ee_ref, ee2_ref,
                 of_ref, ob_ref, s_ref):
    @pl.when(pl.program_id(1) == 0)
    def _():
        s_ref[...] = jnp.zeros_like(s_ref)

    _interleave(
        _ssd_direction(xf_ref, bcf_ref, smf_ref, bias_ref[0:1], a_ref[0:1], ee_ref[0], ee2_ref[0], s_ref.at[0],
                       of_ref, reverse=False, dcol=0),
        _ssd_direction(xb_ref, bcb_ref, smb_ref, bias_ref[1:2], a_ref[1:2], ee_ref[1], ee2_ref[1], s_ref.at[1],
                       ob_ref, reverse=True, dcol=MB_H))


def _ssd2(proj, small, bias, a_row, ee, ee2):
    B, L, _ = proj.shape
    nc = L // CHUNK
    fw = lambda col: (lambda b, i: (b, i, col))
    bw = lambda col: (lambda b, i: (b, nc - 1 - i, col))
    blk = (1, CHUNK, GROUP_W)
    sblk = (1, CHUNK, SMALL_W)
    const2 = lambda b, i: (0, 0)
    const3 = lambda b, i: (0, 0, 0)
    out = jax.ShapeDtypeStruct((B, L, GROUP_W), jnp.bfloat16)
    return pl.pallas_call(
        _ssd2_kernel,
        grid=(B, nc),
        in_specs=[
            pl.BlockSpec(blk, fw(T_MB_X)), pl.BlockSpec(blk, fw(T_MB_BC)), pl.BlockSpec(sblk, fw(0)),
            pl.BlockSpec(blk, bw(T_MB_X)), pl.BlockSpec(blk, bw(T_MB_BC)), pl.BlockSpec(sblk, bw(0)),
            pl.BlockSpec(bias.shape, const2), pl.BlockSpec(a_row.shape, const2),
            pl.BlockSpec(ee.shape, const3), pl.BlockSpec(ee2.shape, const3),
        ],
        out_specs=[pl.BlockSpec(blk, fw(0)), pl.BlockSpec(blk, bw(0))],
        out_shape=[out, out],
        scratch_shapes=[pltpu.VMEM((2, MB_N, GROUP_W), jnp.float32)],
        compiler_params=pltpu.CompilerParams(
            dimension_semantics=("arbitrary", "arbitrary"), vmem_limit_bytes=VMEM_LIMIT),
        name="ssd",
    )(proj, proj, small, proj, proj, small, bias, a_row, ee, ee2)


ML_LANE = 16


def _cummax_rows(u, reverse):
    n = u.shape[0]
    row = lax.broadcasted_iota(jnp.int32, u.shape, 0)
    k = 1
    while k < n:
        if reverse:
            shifted = jnp.where(row < n - k, pltpu.roll(u, n - k, axis=0), NEG)
        else:
            shifted = jnp.where(row >= k, pltpu.roll(u, k, axis=0), NEG)
        u = jnp.maximum(u, shifted)
        k *= 2
    return u


def _mlstm_direction(q_ref, k_ref, v_ref, si_ref, sf_ref, bias_i, bias_f, ee, st_ref, m_ref, o_ref, *, reverse, lane0):
    q = CHUNK
    mask = _chunk_mask(q, reverse)
    gi = si_ref[0] + bias_i
    lf = jax.nn.log_sigmoid(sf_ref[0] + bias_f)
    b = jnp.dot(_tri2(mask), _split_rows(lf), preferred_element_type=jnp.float32)
    u = gi - b
    far = 0 if reverse else q - 1
    m_prev = m_ref[...]
    m_inter = b + m_prev
    m_t = jnp.maximum(m_inter, b + _cummax_rows(u, reverse))
    w_inter = jnp.exp(m_inter - m_t)
    em = jnp.exp(-m_t)
    b_tot = b[far:far + 1, :]
    g = b_tot - b + gi
    m_new = jnp.maximum(b_tot + m_prev, jnp.max(g, axis=0, keepdims=True))
    wk = jnp.exp(g - m_new) * (ML_DH ** -0.5)
    decay = jnp.exp(b_tot + m_prev - m_new)
    m_ref[...] = m_new
    arow = (b - m_t) * LOG2E + math.log2(ML_DH ** -0.5)
    ee_dh, ee_q = ee[:, :GROUP_W], ee[:, GROUP_W:]
    stack = jnp.concatenate([w_inter, em, wk], axis=0)
    full = jnp.dot(_split_cols(stack), ee_dh, preferred_element_type=jnp.float32)
    w_full, em_full, wk_full = full[:q], full[q:2 * q], full[2 * q:]
    a_all = jnp.dot(_split_cols(arow), ee_q, preferred_element_type=jnp.float32)
    dec_full = jnp.dot(_split_cols(jnp.broadcast_to(decay, (8, SMALL_W))), ee_dh,
                       preferred_element_type=jnp.float32)[0:1]
    u_t = (u * LOG2E).T
    ones = jnp.ones((q, ML_DH), jnp.bfloat16)
    hs = []
    yield
    for h in range(ML_H):
        hsl = slice(h * ML_DH, (h + 1) * ML_DH)
        p = jnp.exp2(jnp.where(mask, a_all[:, h * q:(h + 1) * q] + u_t[lane0 + h:lane0 + h + 1, :], NEG))
        qh = q_ref[0, :, hsl]
        kh = k_ref[0, :, hsl]
        v1 = jnp.concatenate([v_ref[0, :, hsl], ones], axis=1)
        s = (_dot_nt(qh, kh) * p).astype(jnp.bfloat16)
        st = st_ref[h]
        inter = jnp.dot(qh, st.astype(jnp.bfloat16), preferred_element_type=jnp.float32)
        both = w_full[:, hsl] * inter[:, :ML_DH], w_full[:, hsl] * inter[:, ML_DH:]
        intra = jnp.dot(s, v1, preferred_element_type=jnp.float32)
        num = both[0] + intra[:, :ML_DH]
        den = both[1] + intra[:, ML_DH:]
        hs.append(num / jnp.maximum(jnp.abs(den), em_full[:, hsl]))
        kw = (kh.astype(jnp.float32) * wk_full[:, hsl]).astype(jnp.bfloat16)
        dh = dec_full[:, hsl]
        st_ref[h] = jnp.concatenate([dh, dh], axis=1) * st + _dot_tn(kw, v1)
        yield
    o_ref[0] = jnp.concatenate(hs, axis=-1).astype(o_ref.dtype)


def _mlstm2_kernel(qf_ref, kf_ref, vf_ref, sif_ref, sff_ref, qb_ref, kb_ref, vb_ref, sib_ref, sfb_ref,
                   bi_ref, bf_ref, ee_ref, of_ref, ob_ref, st_ref, m_ref):
    @pl.when(pl.program_id(1) == 0)
    def _():
        st_ref[...] = jnp.zeros_like(st_ref)
        m_ref[...] = jnp.zeros_like(m_ref)

    _interleave(
        _mlstm_direction(qf_ref, kf_ref, vf_ref, sif_ref, sff_ref, bi_ref[...], bf_ref[...], ee_ref[0], st_ref.at[0],
                         m_ref.at[0], of_ref, reverse=False, lane0=ML_LANE),
        _mlstm_direction(qb_ref, kb_ref, vb_ref, sib_ref, sfb_ref, bi_ref[...], bf_ref[...], ee_ref[1], st_ref.at[1],
                         m_ref.at[1], ob_ref, reverse=True, lane0=ML_LANE + ML_H))


def _mlstm2(proj, small, bias_i, bias_f, ee):
    B, L, _ = proj.shape
    nc = L // CHUNK
    fw = lambda col: (lambda b, i: (b, i, col))
    bw = lambda col: (lambda b, i: (b, nc - 1 - i, col))
    blk = (1, CHUNK, GROUP_W)
    sblk = (1, CHUNK, SMALL_W)
    const2 = lambda b, i: (0, 0)
    out = jax.ShapeDtypeStruct((B, L, GROUP_W), jnp.bfloat16)
    specs = lambda m: [pl.BlockSpec(blk, m(T_ML_Q)), pl.BlockSpec(blk, m(T_ML_K)), pl.BlockSpec(blk, m(T_ML_V)),
                       pl.BlockSpec(sblk, m(0)), pl.BlockSpec(sblk, m(1))]
    return pl.pallas_call(
        _mlstm2_kernel,
        grid=(B, nc),
        in_specs=specs(fw) + specs(bw) + [
            pl.BlockSpec(bias_i.shape, const2), pl.BlockSpec(bias_f.shape, const2),
            pl.BlockSpec(ee.shape, lambda b, i: (0, 0, 0)),
        ],
        out_specs=[pl.BlockSpec(blk, fw(0)), pl.BlockSpec(blk, bw(0))],
        out_shape=[out, out],
        scratch_shapes=[
            pltpu.VMEM((2, ML_H, ML_DH, 2 * ML_DH), jnp.float32),
            pltpu.VMEM((2, 1, SMALL_W), jnp.float32),
        ],
        compiler_params=pltpu.CompilerParams(
            dimension_semantics=("arbitrary", "arbitrary"), vmem_limit_bytes=VMEM_LIMIT),
        name="mlstm",
    )(proj, proj, proj, small, small, proj, proj, proj, small, small, bias_i, bias_f, ee)


def _group_rmsnorm(y, width):
    outs = []
    for g in range(y.shape[-1] // width):
        yg = y[:, g * width:(g + 1) * width]
        ms = jnp.mean(yg * yg, axis=-1, keepdims=True)
        outs.append(yg * lax.rsqrt(ms + RMS_EPS))
    return jnp.concatenate(outs, axis=-1)


def _outproj2_kernel(x_ref, yh_ref, hg_ref, mf_ref, mb_ref, mx_ref, mz_ref, lf_ref, lb_ref, lo_ref, lz_ref, yn_ref,
                     dsk_ref, mnw_ref, lnw_ref, w_ref, o_ref):
    f32 = jnp.float32
    up = lambda r: r[0].astype(f32)
    yh = (up(yh_ref) * _silu(up(hg_ref))).astype(jnp.bfloat16)
    acc = jnp.dot(yh, w_ref[0], preferred_element_type=f32)
    ym = (up(mf_ref) + up(mb_ref) + up(mx_ref) * dsk_ref[...]) * _silu(up(mz_ref))
    ym = (_group_rmsnorm(ym, MB_GW) * mnw_ref[...]).astype(jnp.bfloat16)
    acc += jnp.dot(ym, w_ref[1], preferred_element_type=f32)
    yl = (up(lf_ref) + up(lb_ref)) * jax.nn.sigmoid(up(lo_ref))
    yl = (_group_rmsnorm(yl, ML_DH) * lnw_ref[...] * _silu(up(lz_ref))).astype(jnp.bfloat16)
    acc += jnp.dot(yl, w_ref[2], preferred_element_type=f32)
    acc += jnp.dot(yn_ref[0], w_ref[3], preferred_element_type=f32)
    o_ref[0] = x_ref[0] + acc


def _outproj2(x, y_hy, proj, y_mf, y_mb, h_f, h_b, y_na, dskip, mb_nw, ml_nw, w_out):
    B, L, D = x.shape
    tm = min(512, L)
    tok = lambda b, i: (b, i, 0)
    col = lambda c: (lambda b, i: (b, i, c))
    blk = (1, tm, GROUP_W)
    vec = pl.BlockSpec((1, GROUP_W), lambda b, i: (0, 0))
    return pl.pallas_call(
        _outproj2_kernel,
        grid=(B, L // tm),
        in_specs=[
            pl.BlockSpec((1, tm, D), tok),
            pl.BlockSpec(blk, tok), pl.BlockSpec(blk, col(T_HY_G)),
            pl.BlockSpec(blk, tok), pl.BlockSpec(blk, tok), pl.BlockSpec(blk, col(T_MB_X)), pl.BlockSpec(blk, col(T_MB_Z)),
            pl.BlockSpec(blk, tok), pl.BlockSpec(blk, tok), pl.BlockSpec(blk, col(T_ML_O)), pl.BlockSpec(blk, col(T_ML_Z)),
            pl.BlockSpec(blk, tok),
            vec, vec, vec,
            pl.BlockSpec((4, GROUP_W, D), lambda b, i: (0, 0, 0)),
        ],
        out_specs=pl.BlockSpec((1, tm, D), tok),
        out_shape=jax.ShapeDtypeStruct((B, L, D), jnp.float32),
        compiler_params=pltpu.CompilerParams(
            dimension_semantics=("arbitrary", "arbitrary"), vmem_limit_bytes=VMEM_LIMIT),
        name="outproj",
    )(x, y_hy, proj, y_mf, y_mb, proj, proj, h_f, h_b, proj, proj, y_na, dskip, mb_nw, ml_nw, w_out)


GRID_W = 64
NA_KR = 8
NA_KC = 16
NA_H = 8
NA_DH = 64
ROWS_PER_BLOCK = 8
BLOCK_TOK = ROWS_PER_BLOCK * GRID_W


def _na_kernel(q_ref, kp_ref, kc_ref, kn_ref, vp_ref, vc_ref, vn_ref, g_ref, tbl_ref, o_ref, k_s, v_s, *, rows):
    mblk = pl.program_id(1)
    npair = NA_H // 2
    pw = 2 * NA_DH
    ones = jnp.ones((BLOCK_TOK, pw), jnp.bfloat16)
    for t, (kr, vr) in enumerate(((kp_ref, vp_ref), (kc_ref, vc_ref), (kn_ref, vn_ref))):
        k_s[pl.ds(t * BLOCK_TOK, BLOCK_TOK), :] = kr[0]
        for hp in range(npair):
            v_s[pl.ds(t * BLOCK_TOK, BLOCK_TOK), 2 * hp * pw:(2 * hp + 1) * pw] = vr[0, :, hp * pw:(hp + 1) * pw]
            v_s[pl.ds(t * BLOCK_TOK, BLOCK_TOK), (2 * hp + 1) * pw:(2 * hp + 2) * pw] = ones
    lane = lax.broadcasted_iota(jnp.int32, (GRID_W, pw), 1)
    first = lane < NA_DH
    win = NA_KR * GRID_W
    for j in range(ROWS_PER_BLOCK):
        r = mblk * ROWS_PER_BLOCK + j
        rs = jnp.clip(r - NA_KR // 2, 0, rows - NA_KR)
        didx = r - rs
        off = pl.multiple_of((rs - (mblk - 1) * ROWS_PER_BLOCK) * GRID_W, GRID_W)
        ss = []
        for hp in range(npair):
            ls = slice(hp * pw, (hp + 1) * pw)
            qp = q_ref[0, j * GRID_W:(j + 1) * GRID_W, ls]
            zero = jnp.zeros_like(qp)
            q2 = jnp.concatenate([jnp.where(first, qp, zero), jnp.where(first, zero, qp)], axis=0)
            ss.append(_dot_nt(q2, k_s[pl.ds(off, win), ls]))
        s = jnp.concatenate(ss, axis=0) + tbl_ref[didx].astype(jnp.float32)
        e = jnp.exp2(s - jnp.max(s, axis=-1, keepdims=True)).astype(jnp.bfloat16)
        outs = []
        for hp in range(npair):
            ov = jnp.dot(e[hp * pw:(hp + 1) * pw], v_s[pl.ds(off, win), 2 * hp * pw:(2 * hp + 2) * pw],
                         preferred_element_type=jnp.float32)
            o2 = ov[:, :pw] / ov[:, pw:]
            outs.append(jnp.where(first, o2[:GRID_W], o2[GRID_W:]))
        o = jnp.concatenate(outs, axis=-1)
        gate = _silu(g_ref[0, j * GRID_W:(j + 1) * GRID_W, :].astype(jnp.float32))
        o_ref[0, j * GRID_W:(j + 1) * GRID_W, :] = (o * gate).astype(o_ref.dtype)


def _na(proj, tbl):
    B, L, _ = proj.shape
    rows = L // GRID_W
    nb = rows // ROWS_PER_BLOCK
    prev = lambda col: (lambda b, m: (b, jnp.maximum(m - 1, 0), col))
    cur = lambda col: (lambda b, m: (b, m, col))
    nxt = lambda col: (lambda b, m: (b, jnp.minimum(m + 1, nb - 1), col))
    blk = (1, BLOCK_TOK, GROUP_W)
    return pl.pallas_call(
        functools.partial(_na_kernel, rows=rows),
        grid=(B, nb),
        in_specs=[
            pl.BlockSpec(blk, cur(T_NA_Q)),
            pl.BlockSpec(blk, prev(T_NA_K)), pl.BlockSpec(blk, cur(T_NA_K)), pl.BlockSpec(blk, nxt(T_NA_K)),
            pl.BlockSpec(blk, prev(T_NA_V)), pl.BlockSpec(blk, cur(T_NA_V)), pl.BlockSpec(blk, nxt(T_NA_V)),
            pl.BlockSpec(blk, cur(T_NA_G)),
            pl.BlockSpec(tbl.shape, lambda b, m: (0, 0, 0)),
        ],
        out_specs=pl.BlockSpec(blk, cur(0)),
        out_shape=jax.ShapeDtypeStruct((B, L, GROUP_W), jnp.bfloat16),
        scratch_shapes=[
            pltpu.VMEM((3 * BLOCK_TOK, GROUP_W), jnp.bfloat16),
            pltpu.VMEM((3 * BLOCK_TOK, 2 * GROUP_W), jnp.bfloat16),
        ],
        compiler_params=pltpu.CompilerParams(
            dimension_semantics=("arbitrary", "arbitrary"), vmem_limit_bytes=VMEM_LIMIT),
        name="nbr_attn",
    )(proj, proj, proj, proj, proj, proj, proj, proj, tbl)


def _na_bias_table(rpb):
    nco = 2 * NA_KC - 1
    rows_d = jnp.stack([rpb[:, NA_KR - 1 - d:2 * NA_KR - 1 - d, :] for d in range(NA_KR)])
    lpad = GRID_W - NA_KC
    ext = jnp.pad(rows_d * math.log2(math.e), ((0, 0), (0, 0), (0, 0), (lpad, 2 * GRID_W - lpad - nco)))
    lead = ext.shape[:3]
    skew = jnp.broadcast_to(ext[..., None, :], lead + (GRID_W, 2 * GRID_W)).reshape(lead + (2 * GRID_W * GRID_W,))
    skew = skew[..., :GRID_W * (2 * GRID_W - 1)].reshape(lead + (GRID_W, 2 * GRID_W - 1))[..., GRID_W - 1:]
    w = jnp.arange(GRID_W)
    c = jnp.arange(GRID_W)
    cstart = jnp.clip(w - NA_KC // 2, 0, GRID_W - NA_KC)
    inwin = (c[None, :] >= cstart[:, None]) & (c[None, :] < cstart[:, None] + NA_KC)
    t = jnp.where(inwin, skew, NEG)
    t = t.transpose(0, 1, 3, 2, 4).reshape(NA_KR, NA_H * GRID_W, NA_KR * GRID_W)
    return t.astype(jnp.bfloat16)


HY_LANES = 128
F2N = 128
TW_B = 16
PAD_ROWS = 8
HY_POS_PAD = 128


def _cmul(ar, ai, br, bi):
    return ar * br - ai * bi, ar * bi + ai * br


def _stage_a_store(buf_ref, a, n2, tr, ti, n1f, rs):
    ar, ai = _cmul(a[:n1f], a[n1f:], tr, ti)
    base = pl.multiple_of(n2 * rs, 8)
    buf_ref[pl.ds(base, n1f), :] = ar
    buf_ref[pl.ds(base + n1f, n1f), :] = ai


def _stage_b_load(buf_ref, k1, n1f, rs):
    yr = buf_ref[pl.ds(k1, F2N, stride=rs), :]
    yi = buf_ref[pl.ds(n1f + k1, F2N, stride=rs), :]
    return jnp.concatenate([yr, yi], axis=0).astype(jnp.bfloat16)


def _hy_spec_kernel(ff_ref, fb_ref, mg_ref, tbr_ref, tbi_ref, f2_ref, o_ref, buf_ref, *, n1f, h1, rs, ns):
    s = pl.program_id(2)

    @pl.when(s < ns)
    def _():
        for b in range(TW_B):
            n2 = s * TW_B + b
            hf = ff_ref[pl.ds(n2, h1, stride=F2N), :]
            hb = fb_ref[pl.ds(F2N - n2, h1, stride=F2N), :]
            xs = jnp.concatenate([hf, hb], axis=0).astype(jnp.bfloat16)
            a = jnp.dot(mg_ref[0], xs, preferred_element_type=jnp.float32)
            _stage_a_store(buf_ref, a, n2, tbr_ref[b], tbi_ref[b], n1f, rs)

    @pl.when(s >= ns)
    def _():
        for kk in range(TW_B):
            k1 = (s - ns) * TW_B + kk
            o_ref[0, kk] = jnp.dot(f2_ref[0], _stage_b_load(buf_ref, k1, n1f, rs), preferred_element_type=jnp.float32)


def _hy_spectrum(filt, mg, tbr, tbi, f2, L):
    n1f = 2 * L // F2N
    h1 = n1f // 2
    rs = 2 * n1f + PAD_ROWS
    ns = F2N // TW_B
    nk = n1f // TW_B
    ncb = GROUP_W // HY_LANES
    a_idx = lambda o, c, s: (jnp.minimum(s, ns - 1), 0, 0)
    return pl.pallas_call(
        functools.partial(_hy_spec_kernel, n1f=n1f, h1=h1, rs=rs, ns=ns),
        grid=(2, ncb, ns + nk),
        in_specs=[
            pl.BlockSpec((L + F2N, HY_LANES), lambda o, c, s: (0, o * 2 * ncb + c)),
            pl.BlockSpec((L + F2N, HY_LANES), lambda o, c, s: (0, o * 2 * ncb + ncb + c)),
            pl.BlockSpec((1, 2 * n1f, n1f), a_idx),
            pl.BlockSpec(tbr.shape, lambda o, c, s: (0, 0, 0)),
            pl.BlockSpec(tbi.shape, lambda o, c, s: (0, 0, 0)),
            pl.BlockSpec((1, 2 * F2N, 2 * F2N), lambda o, c, s: (0, 0, 0)),
        ],
        out_specs=pl.BlockSpec((1, TW_B, 2 * F2N, HY_LANES), lambda o, c, s: (o, jnp.maximum(s - ns, 0), 0, c)),
        out_shape=jax.ShapeDtypeStruct((2, n1f, 2 * F2N, GROUP_W), jnp.float32),
        scratch_shapes=[pltpu.VMEM((F2N * rs, HY_LANES), jnp.float32)],
        compiler_params=pltpu.CompilerParams(
            dimension_semantics=("arbitrary", "arbitrary", "arbitrary"), vmem_limit_bytes=VMEM_LIMIT),
        name="hyena_spectrum",
    )(filt, filt, mg, tbr, tbi, f2)


def _hy_conv_kernel(v_ref, x1_ref, x2_ref, g_ref, m1_ref, m3_ref, tbr_ref, tbi_ref, f2_ref, skip_ref,
                    o_ref, buf_ref, z_ref, *, n1f, h1, rs, ns, nk):
    s = pl.program_id(2)
    p1 = ns
    p2 = p1 + nk
    p3 = p2 + ns
    p4 = p3 + nk

    def stage_a(xs, n2, b):
        a = jnp.dot(m1_ref[0], xs, preferred_element_type=jnp.float32)
        _stage_a_store(buf_ref, a, n2, tbr_ref[b], tbi_ref[b], n1f, rs)

    def stage_b(order, kb):
        for kk in range(TW_B):
            k1 = kb * TW_B + kk
            y = jnp.dot(f2_ref[0], _stage_b_load(buf_ref, k1, n1f, rs), preferred_element_type=jnp.float32)
            g = g_ref[0, kk]
            yr, yi = _cmul(y[:F2N], y[F2N:], g[:F2N], g[F2N:])
            ys = jnp.concatenate([yr, yi], axis=0).astype(jnp.bfloat16)
            z = jnp.dot(f2_ref[1], ys, preferred_element_type=jnp.float32)
            buf_ref[pl.ds(k1, F2N, stride=rs), :] = z[:F2N]
            buf_ref[pl.ds(n1f + k1, F2N, stride=rs), :] = z[F2N:]

    def stage_c(n2, b):
        base = pl.multiple_of(n2 * rs, 8)
        zr = buf_ref[pl.ds(base, n1f), :]
        zi = buf_ref[pl.ds(base + n1f, n1f), :]
        wr, wi = _cmul(zr, zi, tbr_ref[b], -tbi_ref[b])
        ws = jnp.concatenate([wr, wi], axis=0).astype(jnp.bfloat16)
        return jnp.dot(m3_ref[0], ws, preferred_element_type=jnp.float32)

    @pl.when(s < p1)
    def _():
        for b in range(TW_B):
            xs = jnp.concatenate([v_ref[0, b], v_ref[1, b]], axis=0)
            stage_a(xs, s * TW_B + b, b)

    @pl.when((s >= p1) & (s < p2))
    def _():
        stage_b(0, s - p1)

    @pl.when((s >= p2) & (s < p3))
    def _():
        for b in range(TW_B):
            n2 = (s - p2) * TW_B + b
            c = stage_c(n2, b)
            zs = []
            for r in range(2):
                vv = v_ref[r, b].astype(jnp.float32)
                z = x1_ref[r, b].astype(jnp.float32) * (c[r * h1:(r + 1) * h1] + skip_ref[0:1, :] * vv)
                zb = z.astype(jnp.bfloat16)
                z_ref[n2, r] = zb
                zs.append(zb)
            stage_a(jnp.concatenate(zs, axis=0), n2, b)

    @pl.when((s >= p3) & (s < p4))
    def _():
        stage_b(1, s - p3)

    @pl.when(s >= p4)
    def _():
        for b in range(TW_B):
            n2 = (s - p4) * TW_B + b
            c = stage_c(n2, b)
            for r in range(2):
                zz = z_ref[n2, r].astype(jnp.float32)
                y = x2_ref[r, b].astype(jnp.float32) * (c[r * h1:(r + 1) * h1] + skip_ref[1:2, :] * zz)
                o_ref[r, pl.ds(n2, h1, stride=F2N), :] = y


def _hy_conv(hyp, spec, m1, m3, tbr, tbi, f2, skip, L):
    B = hyp.shape[0]
    n1f = 2 * L // F2N
    h1 = n1f // 2
    rs = 2 * n1f + PAD_ROWS
    ns = F2N // TW_B
    nk = n1f // TW_B
    ncb = GROUP_W // HY_LANES
    p1, p2, p3, p4 = ns, ns + nk, 2 * ns + nk, 2 * ns + 2 * nk
    clip = lambda v, hi: jnp.clip(v, 0, hi)
    tblk = (2, TW_B, h1, HY_LANES)
    v_idx = lambda c, p, s: (p, jnp.where(s < p1, s, clip(s - p2, ns - 1)), 0, c)
    x1_idx = lambda c, p, s: (p, clip(s - p2, ns - 1), 0, ncb + c)
    x2_idx = lambda c, p, s: (p, clip(s - p4, ns - 1), 0, 2 * ncb + c)
    g_idx = lambda c, p, s: (jnp.where(s < p3, 0, 1), jnp.where(s < p3, clip(s - p1, nk - 1), clip(s - p3, nk - 1)), 0, c)
    m1_idx = lambda c, p, s: (jnp.where(s < p1, s, clip(s - p2, ns - 1)), 0, 0)
    m3_idx = lambda c, p, s: (jnp.where(s < p4, clip(s - p2, ns - 1), s - p4), 0, 0)
    return pl.pallas_call(
        functools.partial(_hy_conv_kernel, n1f=n1f, h1=h1, rs=rs, ns=ns, nk=nk),
        grid=(ncb, B // 2, p4 + ns),
        in_specs=[
            pl.BlockSpec(tblk, v_idx),
            pl.BlockSpec(tblk, x1_idx),
            pl.BlockSpec(tblk, x2_idx),
            pl.BlockSpec((1, TW_B, 2 * F2N, HY_LANES), g_idx),
            pl.BlockSpec((1, 2 * n1f, 2 * h1), m1_idx),
            pl.BlockSpec((1, 2 * h1, 2 * n1f), m3_idx),
            pl.BlockSpec(tbr.shape, lambda c, p, s: (0, 0, 0)),
            pl.BlockSpec(tbi.shape, lambda c, p, s: (0, 0, 0)),
            pl.BlockSpec((2, 2 * F2N, 2 * F2N), lambda c, p, s: (0, 0, 0)),
            pl.BlockSpec((2, HY_LANES), lambda c, p, s: (0, c)),
        ],
        out_specs=pl.BlockSpec((2, L, HY_LANES), lambda c, p, s: (p, 0, c), pipeline_mode=pl.Buffered(1)),
        out_shape=jax.ShapeDtypeStruct((B, L, GROUP_W), jnp.float32),
        scratch_shapes=[
            pltpu.VMEM((F2N * rs, HY_LANES), jnp.float32),
            pltpu.VMEM((F2N, 2, h1, HY_LANES), jnp.bfloat16),
        ],
        compiler_params=pltpu.CompilerParams(
            dimension_semantics=("arbitrary", "arbitrary", "arbitrary"), vmem_limit_bytes=VMEM_LIMIT),
        name="hyena_conv",
    )(hyp, hyp, hyp, spec, m1, m3, tbr, tbi, f2, skip)


def _hy_filter_kernel(pos_ref, w1_ref, b1_ref, w2_ref, b2_ref, w3_ref, fr_ref, dec_ref, o_ref, *, nblk):
    i = pl.program_id(0)
    pos = pos_ref[...]
    fr = fr_ref[...]
    hid = jnp.sin(fr * (jnp.dot(pos, w1_ref[...], precision=_HI, preferred_element_type=jnp.float32) + b1_ref[...]))
    hid = jnp.sin(fr * (jnp.dot(hid, w2_ref[...], precision=_HI, preferred_element_type=jnp.float32) + b2_ref[...]))
    hi = hid.astype(jnp.bfloat16)
    lo = (hid - hi.astype(jnp.float32)).astype(jnp.bfloat16)
    filt = jnp.dot(jnp.concatenate([hi, lo, hi], axis=1), w3_ref[...], preferred_element_type=jnp.float32)
    filt = filt * jnp.exp(-pos[:, 0:1] * dec_ref[...])
    o_ref[...] = jnp.where(i < nblk, filt, 0.0)


def _hy_filters(pos, w1, b1, w2, b2, w3, freq, decay, L):
    nblk = L // F2N
    nh = w2.shape[0]
    no = w3.shape[1]
    const = lambda i: (0, 0)
    return pl.pallas_call(
        functools.partial(_hy_filter_kernel, nblk=nblk),
        grid=(nblk + 1,),
        in_specs=[
            pl.BlockSpec((F2N, HY_POS_PAD), lambda i: (jnp.minimum(i, nblk - 1), 0)),
            pl.BlockSpec((HY_POS_PAD, nh), const), pl.BlockSpec((1, nh), const),
            pl.BlockSpec((nh, nh), const), pl.BlockSpec((1, nh), const),
            pl.BlockSpec((3 * nh, no), const), pl.BlockSpec((1, nh), const), pl.BlockSpec((1, no), const),
        ],
        out_specs=pl.BlockSpec((F2N, no), lambda i: (i, 0)),
        out_shape=jax.ShapeDtypeStruct((L + F2N, no), jnp.float32),
        compiler_params=pltpu.CompilerParams(dimension_semantics=("arbitrary",), vmem_limit_bytes=VMEM_LIMIT),
        name="hyena_filters",
    )(pos, w1, b1, w2, b2, w3, freq, decay)


def _hy_constants(L):
    n = 2 * L
    n1f = n // F2N
    h1 = n1f // 2
    na = F2N // TW_B

    def cis(num, den):
        ang = (-2.0 * math.pi / den) * (num % den).astype(jnp.float32)
        return jnp.cos(ang), jnp.sin(ang)

    k1 = jnp.arange(n1f)
    n1 = jnp.arange(n1f)
    a = jnp.arange(na)
    f1r, f1i = cis(k1[:, None] * n1[None, :], n1f)
    tar, tai = cis(k1[None, :] * (TW_B * a)[:, None], n)
    mr, mi = _cmul(f1r[None], f1i[None], tar[:, :, None], tai[:, :, None])

    def blockform(r, i):
        return jnp.concatenate([jnp.concatenate([r, -i], axis=-1), jnp.concatenate([i, r], axis=-1)], axis=-2)

    m1 = blockform(mr[:, :, :h1], mi[:, :, :h1]).astype(jnp.bfloat16)
    m3r = jnp.swapaxes(mr[:, :, :h1], 1, 2) / n
    m3i = -jnp.swapaxes(mi[:, :, :h1], 1, 2) / n
    m3 = blockform(m3r, m3i).astype(jnp.bfloat16)
    mgr = jnp.concatenate([mr[:, :, :h1], mr[:, :, h1:][:, :, ::-1]], axis=-1)
    mgi = jnp.concatenate([mi[:, :, :h1], mi[:, :, h1:][:, :, ::-1]], axis=-1)
    mg = jnp.concatenate([mgr, mgi], axis=1).astype(jnp.bfloat16)
    b = jnp.arange(TW_B)
    tbr, tbi = cis(k1[None, :] * b[:, None], n)
    tbr = jnp.broadcast_to(tbr[:, :, None], (TW_B, n1f, HY_LANES))
    tbi = jnp.broadcast_to(tbi[:, :, None], (TW_B, n1f, HY_LANES))
    k2 = jnp.arange(F2N)
    f2r, f2i = cis(k2[:, None] * k2[None, :], F2N)
    f2 = jnp.stack([blockform(f2r, f2i), blockform(f2r, -f2i)]).astype(jnp.bfloat16)
    return m1, m3, mg, tbr, tbi, f2


def _hy_positions(L):
    t = jnp.arange(L, dtype=jnp.float32)
    bands = jnp.arange(1, 9, dtype=jnp.float32)
    ang = (2.0 * math.pi / L) * t[:, None] * bands[None, :]
    pos = jnp.concatenate([(t / L)[:, None], jnp.cos(ang), jnp.sin(ang)], axis=-1)
    return jnp.pad(pos, ((0, 0), (0, HY_POS_PAD - pos.shape[1])))


MAIN_COL_RANGES = ((0, 1536), (2048, 3072), (3600, 4624), (1536, 2048), (3072, 3584), (4624, 6160), (6176, 8224))
SMALL_COL_RANGES = ((3584, 3600), (6160, 6176))


def _take_cols(a, ranges):
    return jnp.concatenate([a[..., lo:hi] for lo, hi in ranges], axis=-1)


def _pad_cols(a, width, left=0):
    return jnp.pad(a, [(0, 0)] * (a.ndim - 1) + [(left, width - left - a.shape[-1])])


def kernel(x, norm_w, w_in, w_out, hy_conv_w, hy_conv_b, hy_w1, hy_b1, hy_w2, hy_b2, hy_w3, hy_freq, hy_decay,
           hy_skip, mb_conv_w, mb_conv_b, mb_dt_bias, mb_a_log, mb_d, mb_norm_w, ml_conv_w, ml_conv_b, ml_gate_b,
           ml_norm_w, na_qnorm_w, na_knorm_w, na_rpb):
    B, L, D = x.shape
    depth = w_in.shape[0]
    ncols = N_TILES * TILE_N
    f32 = jnp.float32
    pos = _hy_positions(L)
    m1, m3, mg, tbr, tbi, f2 = _hy_constants(L)
    gmean = jnp.kron(jnp.eye(NA_H, dtype=f32), jnp.full((NA_DH, NA_DH), 1.0 / NA_DH)).astype(jnp.bfloat16)

    def expansion(first_lane, heads, width):
        tgt = jnp.arange(heads * width) // width
        e = [(jnp.arange(SMALL_W)[:, None] == (first_lane + d * heads + tgt)[None, :]) for d in range(2)]
        e = jnp.stack(e).astype(jnp.bfloat16)
        return jnp.concatenate([e, e], axis=1)

    ee_mb = expansion(0, MB_H, MB_P)
    ee2_mb = expansion(0, MB_H, CHUNK)
    ee_ml = jnp.concatenate([expansion(ML_LANE, ML_H, ML_DH), expansion(ML_LANE, ML_H, CHUNK)], axis=2)
    for l in range(depth):
        w = _take_cols(w_in[l], MAIN_COL_RANGES).astype(jnp.bfloat16)
        wl = w_in[l]
        zeros = lambda n: jnp.zeros((D, n), wl.dtype)
        ws = jnp.concatenate([wl[:, 3584:3600], wl[:, 6160:6164], wl[:, 6168:6172], zeros(SMALL_W - 24),
                              zeros(ML_LANE), wl[:, 6164:6168], wl[:, 6172:6176], zeros(SMALL_W - 24)],
                             axis=1).astype(jnp.bfloat16)
        taps = _pad_cols(jnp.concatenate([hy_conv_w[l], mb_conv_w[l], ml_conv_w[l]], axis=-1), ncols)
        cbias = _pad_cols(jnp.concatenate([hy_conv_b[l], mb_conv_b[l], ml_conv_b[l]])[None], ncols)
        qk_scale = jnp.concatenate([jnp.tile(na_qnorm_w[l], NA_H) * (NA_DH ** -0.5 * math.log2(math.e)),
                                    jnp.tile(na_knorm_w[l], NA_H)])[None]
        cscale = _pad_cols(qk_scale, ncols, left=(T_NA_Q + N_HY_TILES) * TILE_N)
        proj, hy, small = _inproj(x, norm_w[l][None], w, ws, taps, cbias, cscale, gmean)

        w3_hi = hy_w3[l].astype(jnp.bfloat16)
        w3_lo = (hy_w3[l] - w3_hi.astype(f32)).astype(jnp.bfloat16)
        filt = _hy_filters(pos, _pad_cols(hy_w1[l].T, HY_POS_PAD).T, hy_b1[l][None], hy_w2[l], hy_b2[l][None],
                           jnp.concatenate([w3_hi, w3_hi, w3_lo]), hy_freq[l][None], hy_decay[l][None], L)
        spec = _hy_spectrum(filt, mg, tbr, tbi, f2[:1], L)
        hyp = hy.reshape(B, L // F2N, F2N, N_HY_TILES * GROUP_W).transpose(0, 2, 1, 3)
        y_hy = _hy_conv(hyp, spec, m1, m3, tbr, tbi, f2, hy_skip[l], L)

        a_log2 = -jnp.exp(mb_a_log[l].astype(f32)) * LOG2E
        dt_bias = jnp.concatenate([_pad_cols(mb_dt_bias[l, d][None], SMALL_W, left=d * MB_H) for d in range(2)])
        a_rows = jnp.concatenate([_pad_cols(a_log2[d][None], SMALL_W, left=d * MB_H) for d in range(2)])
        y_mf, y_mb = _ssd2(proj, small, dt_bias, a_rows, ee_mb, ee2_mb)

        bias_i = _pad_cols(ml_gate_b[l][:, 0, :].reshape(1, -1), SMALL_W, left=ML_LANE)
        bias_f = _pad_cols(ml_gate_b[l][:, 1, :].reshape(1, -1), SMALL_W, left=ML_LANE)
        h_f, h_b = _mlstm2(proj, small, bias_i, bias_f, ee_ml)

        y_na = _na(proj, _na_bias_table(na_rpb[l]))

        x = _outproj2(x, y_hy, proj, y_mf, y_mb, h_f, h_b, y_na, jnp.repeat(mb_d[l], MB_P)[None], mb_norm_w[l][None],
                      ml_norm_w[l][None], w_out[l].astype(jnp.bfloat16).reshape(4, GROUP_W, D))
    return x
```

```python
import functools
import math

import jax
import jax.numpy as jnp
from jax import lax
from jax.experimental import pallas as pl
from jax.experimental.pallas import tpu as pltpu

RMS_EPS = 1e-6
GROUP_W = 512
TILE_N = 512
SMALL_W = 128
SMALL_OUT = 2 * SMALL_W
HALO = 16
INPROJ_ROWS = 512
VMEM_LIMIT = 56 * 1024 * 1024

N_HY_TILES = 3
N_CONV_TILES = 7
N_TILES = 16
T_MB_X, T_MB_BC = 0, 1
T_ML_Q, T_ML_K = 2, 3
T_HY_G, T_MB_Z, T_ML_V, T_ML_O, T_ML_Z = 4, 5, 6, 7, 8
T_NA_Q, T_NA_K, T_NA_V, T_NA_G = 9, 10, 11, 12
N_MAIN_TILES = N_TILES - N_HY_TILES

_HI = lax.Precision.HIGHEST


def _silu(x):
    return x * jax.nn.sigmoid(x)


def _inproj_kernel(x_ref, xp_ref, xn_ref, nw_ref, w_ref, ws_ref, taps_ref, cb_ref, cs_ref, gm_ref,
                   o_ref, oh_ref, os_ref, h_ref, acc_ref, *, tm):
    i = pl.program_id(1)
    ni = pl.num_programs(1)

    def norm(xv):
        ms = jnp.mean(xv * xv, axis=-1, keepdims=True)
        return (xv * lax.rsqrt(ms + RMS_EPS) * nw_ref[...]).astype(jnp.bfloat16)

    h_ref[pl.ds(HALO, tm), :] = norm(x_ref[0])
    hp = norm(xp_ref[0])
    hn = norm(xn_ref[0])
    h_ref[pl.ds(0, HALO), :] = jnp.where(i == 0, jnp.zeros_like(hp), hp)
    h_ref[pl.ds(HALO + tm, HALO), :] = jnp.where(i == ni - 1, jnp.zeros_like(hn), hn)
    os_ref[0] = jnp.dot(h_ref[pl.ds(HALO, tm), :], ws_ref[...], preferred_element_type=jnp.float32)

    qk_tiles = (T_NA_Q + N_HY_TILES, T_NA_K + N_HY_TILES)

    def matmul(u):
        wt = w_ref[:, u * TILE_N:(u + 1) * TILE_N]
        if u < N_CONV_TILES:
            acc_ref[u % 2] = jnp.dot(h_ref[...], wt, preferred_element_type=jnp.float32)
        else:
            acc_ref[u % 2, pl.ds(0, tm), :] = jnp.dot(h_ref[pl.ds(HALO, tm), :], wt, preferred_element_type=jnp.float32)

    def epilogue(u):
        a = acc_ref.at[u % 2]
        cols = slice(u * TILE_N, (u + 1) * TILE_N)
        if u < N_CONV_TILES:
            t = taps_ref[:, cols]
            y = (a[pl.ds(HALO - 1, tm), :] * t[0:1] + a[pl.ds(HALO, tm), :] * t[1:2]
                 + a[pl.ds(HALO + 1, tm), :] * t[2:3] + cb_ref[:, cols])
            if u < N_HY_TILES:
                oh_ref[0, :, cols] = y.astype(oh_ref.dtype)
                return
            y = _silu(y)
        else:
            y = a[pl.ds(0, tm), :]
            if u in qk_tiles:
                ms = jnp.dot((y * y).astype(jnp.bfloat16), gm_ref[...], preferred_element_type=jnp.float32)
                y = y * lax.rsqrt(ms + RMS_EPS) * cs_ref[:, cols]
        o_ref[0, :, (u - N_HY_TILES) * TILE_N:(u - N_HY_TILES + 1) * TILE_N] = y.astype(o_ref.dtype)

    matmul(0)
    for u in range(N_TILES):
        if u + 1 < N_TILES:
            matmul(u + 1)
        epilogue(u)


def _inproj(x, nw, w, ws, taps, cbias, cscale, gmean):
    B, L, D = x.shape
    tm = min(INPROJ_ROWS, L)
    ni = L // tm
    hb = tm // HALO
    nlast = L // HALO - 1
    ncols = N_TILES * TILE_N
    const = lambda b, i: (0, 0)
    tok = lambda b, i: (b, i, 0)
    return pl.pallas_call(
        functools.partial(_inproj_kernel, tm=tm),
        grid=(B, ni),
        in_specs=[
            pl.BlockSpec((1, tm, D), tok),
            pl.BlockSpec((1, HALO, D), lambda b, i: (b, jnp.maximum(i * hb - 1, 0), 0)),
            pl.BlockSpec((1, HALO, D), lambda b, i: (b, jnp.minimum((i + 1) * hb, nlast), 0)),
            pl.BlockSpec((1, D), const),
            pl.BlockSpec((D, ncols), const, pipeline_mode=pl.Buffered(1)),
            pl.BlockSpec((D, SMALL_OUT), const),
            pl.BlockSpec((3, ncols), const),
            pl.BlockSpec((1, ncols), const),
            pl.BlockSpec((1, ncols), const),
            pl.BlockSpec((TILE_N, TILE_N), const),
        ],
        out_specs=[
            pl.BlockSpec((1, tm, N_MAIN_TILES * TILE_N), tok),
            pl.BlockSpec((1, tm, N_HY_TILES * TILE_N), tok),
            pl.BlockSpec((1, tm, SMALL_OUT), tok),
        ],
        out_shape=[
            jax.ShapeDtypeStruct((B, L, N_MAIN_TILES * TILE_N), jnp.bfloat16),
            jax.ShapeDtypeStruct((B, L, N_HY_TILES * TILE_N), jnp.bfloat16),
            jax.ShapeDtypeStruct((B, L, SMALL_OUT), jnp.float32),
        ],
        scratch_shapes=[
            pltpu.VMEM((tm + 2 * HALO, D), jnp.bfloat16),
            pltpu.VMEM((2, tm + 2 * HALO, TILE_N), jnp.float32),
        ],
        compiler_params=pltpu.CompilerParams(
            dimension_semantics=("arbitrary", "arbitrary"), vmem_limit_bytes=VMEM_LIMIT),
        name="inproj",
    )(x, x, x, nw, w, ws, taps, cbias, cscale, gmean)


CHUNK = 256
NEG = -1e30


def _dot_nt(a, b):
    return lax.dot_general(a, b, (((1,), (1,)), ((), ())), preferred_element_type=jnp.float32)


def _dot_tn(a, b):
    return lax.dot_general(a, b, (((0,), (0,)), ((), ())), preferred_element_type=jnp.float32)


def _chunk_mask(q, reverse):
    t = lax.broadcasted_iota(jnp.int32, (q, q), 0)
    s = lax.broadcasted_iota(jnp.int32, (q, q), 1)
    return (s >= t) if reverse else (s <= t)


MB_H = 8
MB_P = 64
MB_N = 128
MB_GW = 256
ML_H = 4
ML_DH = 128


LOG2E = math.log2(math.e)


def _split_cols(x):
    hi = x.astype(jnp.bfloat16)
    lo = (x - hi.astype(jnp.float32)).astype(jnp.bfloat16)
    return jnp.concatenate([hi, lo], axis=1)


def _split_rows(x):
    hi = x.astype(jnp.bfloat16)
    lo = (x - hi.astype(jnp.float32)).astype(jnp.bfloat16)
    return jnp.concatenate([hi, lo], axis=0)


def _tri2(mask):
    tri = mask.astype(jnp.bfloat16)
    return jnp.concatenate([tri, tri], axis=1)


def _ssd_direction(xs_ref, bc_ref, sm_ref, bias, a_row, ee, ee2, s_ref, o_ref, *, reverse, dcol):
    q = CHUNK
    mask = _chunk_mask(q, reverse)
    dt_s = jax.nn.softplus(sm_ref[0] + bias)
    a_s = dt_s * a_row
    c_s = jnp.dot(_tri2(mask), _split_rows(a_s), preferred_element_type=jnp.float32)
    c_t = c_s.T
    full = jnp.dot(_split_cols(jnp.concatenate([c_s, dt_s], axis=0)), ee, preferred_element_type=jnp.float32)
    c_full, dt_full = full[:q], full[q:]
    cc_all = jnp.dot(_split_cols(c_s), ee2, preferred_element_type=jnp.float32)
    far = 0 if reverse else q - 1
    tot_full = c_full[far:far + 1, :]
    x = xs_ref[0].astype(jnp.float32)
    dtx = dt_full * x
    lane = lax.broadcasted_iota(jnp.int32, (1, MB_GW), 1)
    ys = []
    yield
    for g in range(2):
        bg = bc_ref[0, :, g * MB_N:(g + 1) * MB_N]
        cg = bc_ref[0, :, MB_GW + g * MB_N:MB_GW + (g + 1) * MB_N]
        gs = slice(g * MB_GW, (g + 1) * MB_GW)
        gram = _dot_nt(cg, bg)
        dtx_g = dtx[:, gs]
        ms, xms = [], []
        for j in range(4):
            hd = g * 4 + j
            col = dcol + hd
            decay = jnp.exp2(jnp.where(mask, cc_all[:, hd * q:(hd + 1) * q] - c_t[col:col + 1, :], NEG))
            ms.append((gram * decay).astype(jnp.bfloat16))
            xms.append(jnp.where((lane >= j * MB_P) & (lane < (j + 1) * MB_P), dtx_g, 0.0).astype(jnp.bfloat16))
        yg = jnp.dot(jnp.concatenate(ms, axis=1), jnp.concatenate(xms, axis=0), preferred_element_type=jnp.float32)
        s_g = s_ref[:, gs]
        yg = yg + jnp.exp2(c_full[:, gs]) * jnp.dot(cg, s_g.astype(jnp.bfloat16), preferred_element_type=jnp.float32)
        w = jnp.exp2(tot_full[:, gs] - c_full[:, gs])
        s_ref[:, gs] = jnp.exp2(tot_full[:, gs]) * s_g + _dot_tn(bg, (w * dtx_g).astype(jnp.bfloat16))
        ys.append(yg)
        yield
    o_ref[0] = jnp.concatenate(ys, axis=-1).astype(o_ref.dtype)


def _interleave(*gens):
    live = list(gens)
    while live:
        for g in list(live):
            if next(g, StopIteration) is StopIteration:
                live.remove(g)


def _ssd2_kernel(xf_ref, bcf_ref, smf_ref, xb_ref, bcb_ref, smb_ref, bias_ref, a_ref, ee_ref, ee2_ref,
                 of_ref, ob_ref, s_ref):
    @pl.when(pl.program_id(1) == 0)
    def _():
        s_ref[...] = jnp.zeros_like(s_ref)

    _interleave(
        _ssd_direction(xf_ref, bcf_ref, smf_ref, bias_ref[0:1], a_ref[0:1], ee_ref[0], ee2_ref[0], s_ref.at[0],
                       of_ref, reverse=False, dcol=0),
        _ssd_direction(xb_ref, bcb_ref, smb_ref, bias_ref[1:2], a_ref[1:2], ee_ref[1], ee2_ref[1], s_ref.at[1],
                       ob_ref, reverse=True, dcol=MB_H))


def _ssd2(proj, small, bias, a_row, ee, ee2):
    B, L, _ = proj.shape
    nc = L // CHUNK
    fw = lambda col: (lambda b, i: (b, i, col))
    bw = lambda col: (lambda b, i: (b, nc - 1 - i, col))
    blk = (1, CHUNK, GROUP_W)
    sblk = (1, CHUNK, SMALL_W)
    const2 = lambda b, i: (0, 0)
    const3 = lambda b, i: (0, 0, 0)
    out = jax.ShapeDtypeStruct((B, L, GROUP_W), jnp.bfloat16)
    return pl.pallas_call(
        _ssd2_kernel,
        grid=(B, nc),
        in_specs=[
            pl.BlockSpec(blk, fw(T_MB_X)), pl.BlockSpec(blk, fw(T_MB_BC)), pl.BlockSpec(sblk, fw(0)),
            pl.BlockSpec(blk, bw(T_MB_X)), pl.BlockSpec(blk, bw(T_MB_BC)), pl.BlockSpec(sblk, bw(0)),
            pl.BlockSpec(bias.shape, const2), pl.BlockSpec(a_row.shape, const2),
            pl.BlockSpec(ee.shape, const3), pl.BlockSpec(ee2.shape, const3),
        ],
        out_specs=[pl.BlockSpec(blk, fw(0)), pl.BlockSpec(blk, bw(0))],
        out_shape=[out, out],
        scratch_shapes=[pltpu.VMEM((2, MB_N, GROUP_W), jnp.float32)],
        compiler_params=pltpu.CompilerParams(
            dimension_semantics=("arbitrary", "arbitrary"), vmem_limit_bytes=VMEM_LIMIT),
        name="ssd",
    )(proj, proj, small, proj, proj, small, bias, a_row, ee, ee2)


ML_LANE = 16


def _cummax_rows(u, reverse):
    n = u.shape[0]
    row = lax.broadcasted_iota(jnp.int32, u.shape, 0)
    k = 1
    while k < n:
        if reverse:
            shifted = jnp.where(row < n - k, pltpu.roll(u, n - k, axis=0), NEG)
        else:
            shifted = jnp.where(row >= k, pltpu.roll(u, k, axis=0), NEG)
        u = jnp.maximum(u, shifted)
        k *= 2
    return u


def _mlstm_direction(q_ref, k_ref, v_ref, si_ref, sf_ref, bias_i, bias_f, ee, st_ref, m_ref, o_ref, *, reverse, lane0):
    q = CHUNK
    mask = _chunk_mask(q, reverse)
    gi = si_ref[0] + bias_i
    lf = jax.nn.log_sigmoid(sf_ref[0] + bias_f)
    b = jnp.dot(_tri2(mask), _split_rows(lf), preferred_element_type=jnp.float32)
    u = gi - b
    far = 0 if reverse else q - 1
    m_prev = m_ref[...]
    m_inter = b + m_prev
    m_t = jnp.maximum(m_inter, b + _cummax_rows(u, reverse))
    w_inter = jnp.exp(m_inter - m_t)
    em = jnp.exp(-m_t)
    b_tot = b[far:far + 1, :]
    g = b_tot - b + gi
    m_new = jnp.maximum(b_tot + m_prev, jnp.max(g, axis=0, keepdims=True))
    wk = jnp.exp(g - m_new) * (ML_DH ** -0.5)
    decay = jnp.exp(b_tot + m_prev - m_new)
    m_ref[...] = m_new
    arow = (b - m_t) * LOG2E + math.log2(ML_DH ** -0.5)
    ee_dh, ee_q = ee[:, :GROUP_W], ee[:, GROUP_W:]
    stack = jnp.concatenate([w_inter, em, wk], axis=0)
    full = jnp.dot(_split_cols(stack), ee_dh, preferred_element_type=jnp.float32)
    w_full, em_full, wk_full = full[:q], full[q:2 * q], full[2 * q:]
    a_all = jnp.dot(_split_cols(arow), ee_q, preferred_element_type=jnp.float32)
    dec_full = jnp.dot(_split_cols(jnp.broadcast_to(decay, (8, SMALL_W))), ee_dh,
                       preferred_element_type=jnp.float32)[0:1]
    u_t = (u * LOG2E).T
    ones = jnp.ones((q, ML_DH), jnp.bfloat16)
    hs = []
    yield
    for h in range(ML_H):
        hsl = slice(h * ML_DH, (h + 1) * ML_DH)
        p = jnp.exp2(jnp.where(mask, a_all[:, h * q:(h + 1) * q] + u_t[lane0 + h:lane0 + h + 1, :], NEG))
        qh = q_ref[0, :, hsl]
        kh = k_ref[0, :, hsl]
        v1 = jnp.concatenate([v_ref[0, :, hsl], ones], axis=1)
        s = (_dot_nt(qh, kh) * p).astype(jnp.bfloat16)
        st = st_ref[h]
        inter = jnp.dot(qh, st.astype(jnp.bfloat16), preferred_element_type=jnp.float32)
        both = w_full[:, hsl] * inter[:, :ML_DH], w_full[:, hsl] * inter[:, ML_DH:]
        intra = jnp.dot(s, v1, preferred_element_type=jnp.float32)
        num = both[0] + intra[:, :ML_DH]
        den = both[1] + intra[:, ML_DH:]
        hs.append(num / jnp.maximum(jnp.abs(den), em_full[:, hsl]))
        kw = (kh.astype(jnp.float32) * wk_full[:, hsl]).astype(jnp.bfloat16)
        dh = dec_full[:, hsl]
        st_ref[h] = jnp.concatenate([dh, dh], axis=1) * st + _dot_tn(kw, v1)
        yield
    o_ref[0] = jnp.concatenate(hs, axis=-1).astype(o_ref.dtype)


def _mlstm2_kernel(qf_ref, kf_ref, vf_ref, sif_ref, sff_ref, qb_ref, kb_ref, vb_ref, sib_ref, sfb_ref,
                   bi_ref, bf_ref, ee_ref, of_ref, ob_ref, st_ref, m_ref):
    @pl.when(pl.program_id(1) == 0)
    def _():
        st_ref[...] = jnp.zeros_like(st_ref)
        m_ref[...] = jnp.zeros_like(m_ref)

    _interleave(
        _mlstm_direction(qf_ref, kf_ref, vf_ref, sif_ref, sff_ref, bi_ref[...], bf_ref[...], ee_ref[0], st_ref.at[0],
                         m_ref.at[0], of_ref, reverse=False, lane0=ML_LANE),
        _mlstm_direction(qb_ref, kb_ref, vb_ref, sib_ref, sfb_ref, bi_ref[...], bf_ref[...], ee_ref[1], st_ref.at[1],
                         m_ref.at[1], ob_ref, reverse=True, lane0=ML_LANE + ML_H))


def _mlstm2(proj, small, bias_i, bias_f, ee):
    B, L, _ = proj.shape
    nc = L // CHUNK
    fw = lambda col: (lambda b, i: (b, i, col))
    bw = lambda col: (lambda b, i: (b, nc - 1 - i, col))
    blk = (1, CHUNK, GROUP_W)
    sblk = (1, CHUNK, SMALL_W)
    const2 = lambda b, i: (0, 0)
    out = jax.ShapeDtypeStruct((B, L, GROUP_W), jnp.bfloat16)
    specs = lambda m: [pl.BlockSpec(blk, m(T_ML_Q)), pl.BlockSpec(blk, m(T_ML_K)), pl.BlockSpec(blk, m(T_ML_V)),
                       pl.BlockSpec(sblk, m(0)), pl.BlockSpec(sblk, m(1))]
    return pl.pallas_call(
        _mlstm2_kernel,
        grid=(B, nc),
        in_specs=specs(fw) + specs(bw) + [
            pl.BlockSpec(bias_i.shape, const2), pl.BlockSpec(bias_f.shape, const2),
            pl.BlockSpec(ee.shape, lambda b, i: (0, 0, 0)),
        ],
        out_specs=[pl.BlockSpec(blk, fw(0)), pl.BlockSpec(blk, bw(0))],
        out_shape=[out, out],
        scratch_shapes=[
            pltpu.VMEM((2, ML_H, ML_DH, 2 * ML_DH), jnp.float32),
            pltpu.VMEM((2, 1, SMALL_W), jnp.float32),
        ],
        compiler_params=pltpu.CompilerParams(
            dimension_semantics=("arbitrary", "arbitrary"), vmem_limit_bytes=VMEM_LIMIT),
        name="mlstm",
    )(proj, proj, proj, small, small, proj, proj, proj, small, small, bias_i, bias_f, ee)


def _group_rmsnorm(y, width):
    outs = []
    for g in range(y.shape[-1] // width):
        yg = y[:, g * width:(g + 1) * width]
        ms = jnp.mean(yg * yg, axis=-1, keepdims=True)
        outs.append(yg * lax.rsqrt(ms + RMS_EPS))
    return jnp.concatenate(outs, axis=-1)


def _outproj2_kernel(x_ref, yh_ref, hg_ref, mf_ref, mb_ref, mx_ref, mz_ref, lf_ref, lb_ref, lo_ref, lz_ref, yn_ref,
                     dsk_ref, mnw_ref, lnw_ref, w_ref, o_ref):
    f32 = jnp.float32
    up = lambda r: r[0].astype(f32)
    yh = (up(yh_ref) * _silu(up(hg_ref))).astype(jnp.bfloat16)
    acc = jnp.dot(yh, w_ref[0], preferred_element_type=f32)
    ym = (up(mf_ref) + up(mb_ref) + up(mx_ref) * dsk_ref[...]) * _silu(up(mz_ref))
    ym = (_group_rmsnorm(ym, MB_GW) * mnw_ref[...]).astype(jnp.bfloat16)
    acc += jnp.dot(ym, w_ref[1], preferred_element_type=f32)
    yl = (up(lf_ref) + up(lb_ref)) * jax.nn.sigmoid(up(lo_ref))
    yl = (_group_rmsnorm(yl, ML_DH) * lnw_ref[...] * _silu(up(lz_ref))).astype(jnp.bfloat16)
    acc += jnp.dot(yl, w_ref[2], preferred_element_type=f32)
    acc += jnp.dot(yn_ref[0], w_ref[3], preferred_element_type=f32)
    o_ref[0] = x_ref[0] + acc


def _outproj2(x, y_hy, proj, y_mf, y_mb, h_f, h_b, y_na, dskip, mb_nw, ml_nw, w_out):
    B, L, D = x.shape
    tm = min(512, L)
    tok = lambda b, i: (b, i, 0)
    col = lambda c: (lambda b, i: (b, i, c))
    blk = (1, tm, GROUP_W)
    vec = pl.BlockSpec((1, GROUP_W), lambda b, i: (0, 0))
    return pl.pallas_call(
        _outproj2_kernel,
        grid=(B, L // tm),
        in_specs=[
            pl.BlockSpec((1, tm, D), tok),
            pl.BlockSpec(blk, tok), pl.BlockSpec(blk, col(T_HY_G)),
            pl.BlockSpec(blk, tok), pl.BlockSpec(blk, tok), pl.BlockSpec(blk, col(T_MB_X)), pl.BlockSpec(blk, col(T_MB_Z)),
            pl.BlockSpec(blk, tok), pl.BlockSpec(blk, tok), pl.BlockSpec(blk, col(T_ML_O)), pl.BlockSpec(blk, col(T_ML_Z)),
            pl.BlockSpec(blk, tok),
            vec, vec, vec,
            pl.BlockSpec((4, GROUP_W, D), lambda b, i: (0, 0, 0)),
        ],
        out_specs=pl.BlockSpec((1, tm, D), tok),
        out_shape=jax.ShapeDtypeStruct((B, L, D), jnp.float32),
        compiler_params=pltpu.CompilerParams(
            dimension_semantics=("arbitrary", "arbitrary"), vmem_limit_bytes=VMEM_LIMIT),
        name="outproj",
    )(x, y_hy, proj, y_mf, y_mb, proj, proj, h_f, h_b, proj, proj, y_na, dskip, mb_nw, ml_nw, w_out)


GRID_W = 64
NA_KR = 8
NA_KC = 16
NA_H = 8
NA_DH = 64
ROWS_PER_BLOCK = 8
BLOCK_TOK = ROWS_PER_BLOCK * GRID_W


def _na_kernel(q_ref, kp_ref, kc_ref, kn_ref, vp_ref, vc_ref, vn_ref, g_ref, tbl_ref, o_ref, k_s, v_s, *, rows):
    mblk = pl.program_id(1)
    npair = NA_H // 2
    pw = 2 * NA_DH
    ones = jnp.ones((BLOCK_TOK, pw), jnp.bfloat16)
    for t, (kr, vr) in enumerate(((kp_ref, vp_ref), (kc_ref, vc_ref), (kn_ref, vn_ref))):
        k_s[pl.ds(t * BLOCK_TOK, BLOCK_TOK), :] = kr[0]
        for hp in range(npair):
            v_s[pl.ds(t * BLOCK_TOK, BLOCK_TOK), 2 * hp * pw:(2 * hp + 1) * pw] = vr[0, :, hp * pw:(hp + 1) * pw]
            v_s[pl.ds(t * BLOCK_TOK, BLOCK_TOK), (2 * hp + 1) * pw:(2 * hp + 2) * pw] = ones
    lane = lax.broadcasted_iota(jnp.int32, (GRID_W, pw), 1)
    first = lane < NA_DH
    win = NA_KR * GRID_W
    for j in range(ROWS_PER_BLOCK):
        r = mblk * ROWS_PER_BLOCK + j
        rs = jnp.clip(r - NA_KR // 2, 0, rows - NA_KR)
        didx = r - rs
        off = pl.multiple_of((rs - (mblk - 1) * ROWS_PER_BLOCK) * GRID_W, GRID_W)
        ss = []
        for hp in range(npair):
            ls = slice(hp * pw, (hp + 1) * pw)
            qp = q_ref[0, j * GRID_W:(j + 1) * GRID_W, ls]
            zero = jnp.zeros_like(qp)
            q2 = jnp.concatenate([jnp.where(first, qp, zero), jnp.where(first, zero, qp)], axis=0)
            ss.append(_dot_nt(q2, k_s[pl.ds(off, win), ls]))
        s = jnp.concatenate(ss, axis=0) + tbl_ref[didx].astype(jnp.float32)
        e = jnp.exp2(s - jnp.max(s, axis=-1, keepdims=True)).astype(jnp.bfloat16)
        outs = []
        for hp in range(npair):
            ov = jnp.dot(e[hp * pw:(hp + 1) * pw], v_s[pl.ds(off, win), 2 * hp * pw:(2 * hp + 2) * pw],
                         preferred_element_type=jnp.float32)
            o2 = ov[:, :pw] / ov[:, pw:]
            outs.append(jnp.where(first, o2[:GRID_W], o2[GRID_W:]))
        o = jnp.concatenate(outs, axis=-1)
        gate = _silu(g_ref[0, j * GRID_W:(j + 1) * GRID_W, :].astype(jnp.float32))
        o_ref[0, j * GRID_W:(j + 1) * GRID_W, :] = (o * gate).astype(o_ref.dtype)


def _na(proj, tbl):
    B, L, _ = proj.shape
    rows = L // GRID_W
    nb = rows // ROWS_PER_BLOCK
    prev = lambda col: (lambda b, m: (b, jnp.maximum(m - 1, 0), col))
    cur = lambda col: (lambda b, m: (b, m, col))
    nxt = lambda col: (lambda b, m: (b, jnp.minimum(m + 1, nb - 1), col))
    blk = (1, BLOCK_TOK, GROUP_W)
    return pl.pallas_call(
        functools.partial(_na_kernel, rows=rows),
        grid=(B, nb),
        in_specs=[
            pl.BlockSpec(blk, cur(T_NA_Q)),
            pl.BlockSpec(blk, prev(T_NA_K)), pl.BlockSpec(blk, cur(T_NA_K)), pl.BlockSpec(blk, nxt(T_NA_K)),
            pl.BlockSpec(blk, prev(T_NA_V)), pl.BlockSpec(blk, cur(T_NA_V)), pl.BlockSpec(blk, nxt(T_NA_V)),
            pl.BlockSpec(blk, cur(T_NA_G)),
            pl.BlockSpec(tbl.shape, lambda b, m: (0, 0, 0)),
        ],
        out_specs=pl.BlockSpec(blk, cur(0)),
        out_shape=jax.ShapeDtypeStruct((B, L, GROUP_W), jnp.bfloat16),
        scratch_shapes=[
            pltpu.VMEM((3 * BLOCK_TOK, GROUP_W), jnp.bfloat16),
            pltpu.VMEM((3 * BLOCK_TOK, 2 * GROUP_W), jnp.bfloat16),
        ],
        compiler_params=pltpu.CompilerParams(
            dimension_semantics=("arbitrary", "arbitrary"), vmem_limit_bytes=VMEM_LIMIT),
        name="nbr_attn",
    )(proj, proj, proj, proj, proj, proj, proj, proj, tbl)


def _na_bias_table(rpb):
    nco = 2 * NA_KC - 1
    rows_d = jnp.stack([rpb[:, NA_KR - 1 - d:2 * NA_KR - 1 - d, :] for d in range(NA_KR)])
    lpad = GRID_W - NA_KC
    ext = jnp.pad(rows_d * math.log2(math.e), ((0, 0), (0, 0), (0, 0), (lpad, 2 * GRID_W - lpad - nco)))
    lead = ext.shape[:3]
    skew = jnp.broadcast_to(ext[..., None, :], lead + (GRID_W, 2 * GRID_W)).reshape(lead + (2 * GRID_W * GRID_W,))
    skew = skew[..., :GRID_W * (2 * GRID_W - 1)].reshape(lead + (GRID_W, 2 * GRID_W - 1))[..., GRID_W - 1:]
    w = jnp.arange(GRID_W)
    c = jnp.arange(GRID_W)
    cstart = jnp.clip(w - NA_KC // 2, 0, GRID_W - NA_KC)
    inwin = (c[None, :] >= cstart[:, None]) & (c[None, :] < cstart[:, None] + NA_KC)
    t = jnp.where(inwin, skew, NEG)
    t = t.transpose(0, 1, 3, 2, 4).reshape(NA_KR, NA_H * GRID_W, NA_KR * GRID_W)
    return t.astype(jnp.bfloat16)


HY_LANES = 256
TBL_LANES = 128
F2N = 128
TW_B = 16
PAD_ROWS = 8
HY_POS_PAD = 128


def _cmul(ar, ai, br, bi):
    return ar * br - ai * bi, ar * bi + ai * br


def _pack_c(re, im):
    half = jnp.uint32(0x8000)
    rb = lax.bitcast_convert_type(re, jnp.uint32) + half
    ib = lax.bitcast_convert_type(im, jnp.uint32) + half
    return (rb & jnp.uint32(0xFFFF0000)) | (ib >> 16)


def _unpack_c(w):
    re = lax.bitcast_convert_type(w & jnp.uint32(0xFFFF0000), jnp.float32)
    im = lax.bitcast_convert_type(w << 16, jnp.float32)
    return re, im


def _tile_lanes(t):
    return jnp.concatenate([t] * (HY_LANES // TBL_LANES), axis=1)


LANE_PARTS = HY_LANES // TBL_LANES


def _rows_load(refs_or_ref, idx):
    parts = [refs_or_ref[p, idx, :] for p in range(LANE_PARTS)] if not isinstance(refs_or_ref, (list, tuple)) \
        else [r[idx, :] for r in refs_or_ref]
    return jnp.concatenate(parts, axis=1)


def _rows_store(buf_ref, idx, val):
    for p in range(LANE_PARTS):
        buf_ref[p, idx, :] = val[:, p * TBL_LANES:(p + 1) * TBL_LANES]


def _stage_a_store(buf_ref, a, n2, tr, ti, n1f, rs):
    ar, ai = _cmul(a[:n1f], a[n1f:], _tile_lanes(tr), _tile_lanes(ti))
    _rows_store(buf_ref, pl.ds(pl.multiple_of(n2 * rs, 8), n1f), _pack_c(ar, ai))


def _stage_b_load(buf_ref, k1, rs):
    yr, yi = _unpack_c(_rows_load(buf_ref, pl.ds(k1, F2N, stride=rs)))
    return jnp.concatenate([yr, yi], axis=0).astype(jnp.bfloat16)


def _hy_spec_kernel(*refs, n1f, h1, rs, ns):
    ff_refs, fb_refs = refs[:LANE_PARTS], refs[LANE_PARTS:2 * LANE_PARTS]
    mg_ref, tbr_ref, tbi_ref, f2_ref, o_ref, buf_ref = refs[2 * LANE_PARTS:]
    s = pl.program_id(2)

    @pl.when(s < ns)
    def _():
        for b in range(TW_B):
            n2 = s * TW_B + b
            hf = _rows_load(ff_refs, pl.ds(n2, h1, stride=F2N))
            hb = _rows_load(fb_refs, pl.ds(F2N - n2, h1, stride=F2N))
            xs = jnp.concatenate([hf, hb], axis=0).astype(jnp.bfloat16)
            a = jnp.dot(mg_ref[0], xs, preferred_element_type=jnp.float32)
            _stage_a_store(buf_ref, a, n2, tbr_ref[b], tbi_ref[b], n1f, rs)

    @pl.when(s >= ns)
    def _():
        for kk in range(TW_B):
            k1 = (s - ns) * TW_B + kk
            o_ref[0, kk] = jnp.dot(f2_ref[0], _stage_b_load(buf_ref, k1, rs), preferred_element_type=jnp.float32)


def _hy_spectrum(filt, mg, tbr, tbi, f2, L):
    n1f = 2 * L // F2N
    h1 = n1f // 2
    rs = n1f + PAD_ROWS
    ns = F2N // TW_B
    nk = n1f // TW_B
    ncb = GROUP_W // HY_LANES
    a_idx = lambda o, c, s: (jnp.minimum(s, ns - 1), 0, 0)
    once = pl.Buffered(1)
    nparts = GROUP_W // TBL_LANES

    def filt_spec(direction, part):
        return pl.BlockSpec(
            (L + F2N, TBL_LANES),
            lambda o, c, s: (0, (o * 2 + direction) * nparts + c * LANE_PARTS + part), pipeline_mode=once)

    return pl.pallas_call(
        functools.partial(_hy_spec_kernel, n1f=n1f, h1=h1, rs=rs, ns=ns),
        grid=(2, ncb, ns + nk),
        in_specs=[filt_spec(0, p) for p in range(LANE_PARTS)] + [filt_spec(1, p) for p in range(LANE_PARTS)] + [
            pl.BlockSpec((1, 2 * n1f, n1f), a_idx),
            pl.BlockSpec(tbr.shape, lambda o, c, s: (0, 0, 0)),
            pl.BlockSpec(tbi.shape, lambda o, c, s: (0, 0, 0)),
            pl.BlockSpec((1, 2 * F2N, 2 * F2N), lambda o, c, s: (0, 0, 0)),
        ],
        out_specs=pl.BlockSpec((1, TW_B, 2 * F2N, HY_LANES), lambda o, c, s: (o, jnp.maximum(s - ns, 0), 0, c)),
        out_shape=jax.ShapeDtypeStruct((2, n1f, 2 * F2N, GROUP_W), jnp.float32),
        scratch_shapes=[pltpu.VMEM((LANE_PARTS, F2N * rs, TBL_LANES), jnp.uint32)],
        compiler_params=pltpu.CompilerParams(
            dimension_semantics=("arbitrary", "arbitrary", "arbitrary"), vmem_limit_bytes=VMEM_LIMIT),
        name="hyena_spectrum",
    )(*([filt] * (2 * LANE_PARTS)), mg, tbr, tbi, f2)


def _hy_conv_kernel(v_ref, x1_ref, x2_ref, g_ref, m1_ref, m3_ref, tbr_ref, tbi_ref, f2_ref, skip_ref,
                    o_ref, buf_ref, z_ref, *, n1f, h1, rs, ns, nk):
    s = pl.program_id(2)
    p1 = ns
    p2 = p1 + nk
    p3 = p2 + ns
    p4 = p3 + nk

    def stage_a(xs, n2, b):
        a = jnp.dot(m1_ref[0], xs, preferred_element_type=jnp.float32)
        _stage_a_store(buf_ref, a, n2, tbr_ref[b], tbi_ref[b], n1f, rs)

    def stage_b(order, kb):
        for kk in range(TW_B):
            k1 = kb * TW_B + kk
            y = jnp.dot(f2_ref[0], _stage_b_load(buf_ref, k1, rs), preferred_element_type=jnp.float32)
            g = g_ref[0, kk]
            yr, yi = _cmul(y[:F2N], y[F2N:], g[:F2N], g[F2N:])
            ys = jnp.concatenate([yr, yi], axis=0).astype(jnp.bfloat16)
            z = jnp.dot(f2_ref[1], ys, preferred_element_type=jnp.float32)
            _rows_store(buf_ref, pl.ds(k1, F2N, stride=rs), _pack_c(z[:F2N], z[F2N:]))

    def stage_c(n2, b):
        zr, zi = _unpack_c(_rows_load(buf_ref, pl.ds(pl.multiple_of(n2 * rs, 8), n1f)))
        wr, wi = _cmul(zr, zi, _tile_lanes(tbr_ref[b]), -_tile_lanes(tbi_ref[b]))
        ws = jnp.concatenate([wr, wi], axis=0).astype(jnp.bfloat16)
        return jnp.dot(m3_ref[0], ws, preferred_element_type=jnp.float32)

    @pl.when(s < p1)
    def _():
        for b in range(TW_B):
            xs = jnp.concatenate([v_ref[0, b], v_ref[1, b]], axis=0)
            stage_a(xs, s * TW_B + b, b)

    @pl.when((s >= p1) & (s < p2))
    def _():
        stage_b(0, s - p1)

    @pl.when((s >= p2) & (s < p3))
    def _():
        for b in range(TW_B):
            n2 = (s - p2) * TW_B + b
            c = stage_c(n2, b)
            zs = []
            for r in range(2):
                vv = v_ref[r, b].astype(jnp.float32)
                z = x1_ref[r, b].astype(jnp.float32) * (c[r * h1:(r + 1) * h1] + skip_ref[0:1, :] * vv)
                zb = z.astype(jnp.bfloat16)
                z_ref[n2, r] = zb
                zs.append(zb)
            stage_a(jnp.concatenate(zs, axis=0), n2, b)

    @pl.when((s >= p3) & (s < p4))
    def _():
        stage_b(1, s - p3)

    @pl.when(s >= p4)
    def _():
        for b in range(TW_B):
            n2 = (s - p4) * TW_B + b
            c = stage_c(n2, b)
            for r in range(2):
                zz = z_ref[n2, r].astype(jnp.float32)
                y = x2_ref[r, b].astype(jnp.float32) * (c[r * h1:(r + 1) * h1] + skip_ref[1:2, :] * zz)
                o_ref[r, b] = y.astype(o_ref.dtype)


def _hy_conv(hyp, spec, m1, m3, tbr, tbi, f2, skip, L):
    B = hyp.shape[0]
    n1f = 2 * L // F2N
    h1 = n1f // 2
    rs = n1f + PAD_ROWS
    ns = F2N // TW_B
    nk = n1f // TW_B
    ncb = GROUP_W // HY_LANES
    p1, p2, p3, p4 = ns, ns + nk, 2 * ns + nk, 2 * ns + 2 * nk
    clip = lambda v, hi: jnp.clip(v, 0, hi)
    tblk = (2, TW_B, h1, HY_LANES)
    v_idx = lambda c, p, s: (p, jnp.where(s < p1, s, clip(s - p2, ns - 1)), 0, c)
    x1_idx = lambda c, p, s: (p, clip(s - p2, ns - 1), 0, ncb + c)
    x2_idx = lambda c, p, s: (p, clip(s - p4, ns - 1), 0, 2 * ncb + c)
    g_idx = lambda c, p, s: (jnp.where(s < p3, 0, 1), jnp.where(s < p3, clip(s - p1, nk - 1), clip(s - p3, nk - 1)), 0, c)
    m1_idx = lambda c, p, s: (jnp.where(s < p1, s, clip(s - p2, ns - 1)), 0, 0)
    m3_idx = lambda c, p, s: (jnp.where(s < p4, clip(s - p2, ns - 1), s - p4), 0, 0)
    return pl.pallas_call(
        functools.partial(_hy_conv_kernel, n1f=n1f, h1=h1, rs=rs, ns=ns, nk=nk),
        grid=(ncb, B // 2, p4 + ns),
        in_specs=[
            pl.BlockSpec(tblk, v_idx),
            pl.BlockSpec(tblk, x1_idx),
            pl.BlockSpec(tblk, x2_idx),
            pl.BlockSpec((1, TW_B, 2 * F2N, HY_LANES), g_idx),
            pl.BlockSpec((1, 2 * n1f, 2 * h1), m1_idx),
            pl.BlockSpec((1, 2 * h1, 2 * n1f), m3_idx),
            pl.BlockSpec(tbr.shape, lambda c, p, s: (0, 0, 0)),
            pl.BlockSpec(tbi.shape, lambda c, p, s: (0, 0, 0)),
            pl.BlockSpec((2, 2 * F2N, 2 * F2N), lambda c, p, s: (0, 0, 0)),
            pl.BlockSpec((2, HY_LANES), lambda c, p, s: (0, c)),
        ],
        out_specs=pl.BlockSpec(tblk, lambda c, p, s: (p, clip(s - p4, ns - 1), 0, c)),
        out_shape=jax.ShapeDtypeStruct((B, F2N, h1, GROUP_W), jnp.bfloat16),
        scratch_shapes=[
            pltpu.VMEM((LANE_PARTS, F2N * rs, TBL_LANES), jnp.uint32),
            pltpu.VMEM((F2N, 2, h1, HY_LANES), jnp.bfloat16),
        ],
        compiler_params=pltpu.CompilerParams(
            dimension_semantics=("arbitrary", "arbitrary", "arbitrary"), vmem_limit_bytes=VMEM_LIMIT),
        name="hyena_conv",
    )(hyp, hyp, hyp, spec, m1, m3, tbr, tbi, f2, skip)


def _hy_filter_kernel(pos_ref, w1_ref, b1_ref, w2_ref, b2_ref, w3_ref, fr_ref, dec_ref, o_ref, *, nblk):
    i = pl.program_id(0)
    pos = pos_ref[...]
    fr = fr_ref[...]
    hid = jnp.sin(fr * (jnp.dot(pos, w1_ref[...], precision=_HI, preferred_element_type=jnp.float32) + b1_ref[...]))
    hid = jnp.sin(fr * (jnp.dot(hid, w2_ref[...], precision=_HI, preferred_element_type=jnp.float32) + b2_ref[...]))
    hi = hid.astype(jnp.bfloat16)
    lo = (hid - hi.astype(jnp.float32)).astype(jnp.bfloat16)
    filt = jnp.dot(jnp.concatenate([hi, lo, hi], axis=1), w3_ref[...], preferred_element_type=jnp.float32)
    filt = filt * jnp.exp(-pos[:, 0:1] * dec_ref[...])
    o_ref[...] = jnp.where(i < nblk, filt, 0.0)


def _hy_filters(pos, w1, b1, w2, b2, w3, freq, decay, L):
    nblk = L // F2N
    nh = w2.shape[0]
    no = w3.shape[1]
    const = lambda i: (0, 0)
    return pl.pallas_call(
        functools.partial(_hy_filter_kernel, nblk=nblk),
        grid=(nblk + 1,),
        in_specs=[
            pl.BlockSpec((F2N, HY_POS_PAD), lambda i: (jnp.minimum(i, nblk - 1), 0)),
            pl.BlockSpec((HY_POS_PAD, nh), const), pl.BlockSpec((1, nh), const),
            pl.BlockSpec((nh, nh), const), pl.BlockSpec((1, nh), const),
            pl.BlockSpec((3 * nh, no), const), pl.BlockSpec((1, nh), const), pl.BlockSpec((1, no), const),
        ],
        out_specs=pl.BlockSpec((F2N, no), lambda i: (i, 0)),
        out_shape=jax.ShapeDtypeStruct((L + F2N, no), jnp.float32),
        compiler_params=pltpu.CompilerParams(dimension_semantics=("arbitrary",), vmem_limit_bytes=VMEM_LIMIT),
        name="hyena_filters",
    )(pos, w1, b1, w2, b2, w3, freq, decay)


def _hy_constants(L):
    n = 2 * L
    n1f = n // F2N
    h1 = n1f // 2
    na = F2N // TW_B

    def cis(num, den):
        ang = (-2.0 * math.pi / den) * (num % den).astype(jnp.float32)
        return jnp.cos(ang), jnp.sin(ang)

    k1 = jnp.arange(n1f)
    n1 = jnp.arange(n1f)
    a = jnp.arange(na)
    f1r, f1i = cis(k1[:, None] * n1[None, :], n1f)
    tar, tai = cis(k1[None, :] * (TW_B * a)[:, None], n)
    mr, mi = _cmul(f1r[None], f1i[None], tar[:, :, None], tai[:, :, None])

    def blockform(r, i):
        return jnp.concatenate([jnp.concatenate([r, -i], axis=-1), jnp.concatenate([i, r], axis=-1)], axis=-2)

    m1 = blockform(mr[:, :, :h1], mi[:, :, :h1]).astype(jnp.bfloat16)
    m3r = jnp.swapaxes(mr[:, :, :h1], 1, 2) / n
    m3i = -jnp.swapaxes(mi[:, :, :h1], 1, 2) / n
    m3 = blockform(m3r, m3i).astype(jnp.bfloat16)
    mgr = jnp.concatenate([mr[:, :, :h1], mr[:, :, h1:][:, :, ::-1]], axis=-1)
    mgi = jnp.concatenate([mi[:, :, :h1], mi[:, :, h1:][:, :, ::-1]], axis=-1)
    mg = jnp.concatenate([mgr, mgi], axis=1).astype(jnp.bfloat16)
    b = jnp.arange(TW_B)
    tbr, tbi = cis(k1[None, :] * b[:, None], n)
    tbr = jnp.broadcast_to(tbr[:, :, None], (TW_B, n1f, TBL_LANES))
    tbi = jnp.broadcast_to(tbi[:, :, None], (TW_B, n1f, TBL_LANES))
    k2 = jnp.arange(F2N)
    f2r, f2i = cis(k2[:, None] * k2[None, :], F2N)
    f2 = jnp.stack([blockform(f2r, f2i), blockform(f2r, -f2i)]).astype(jnp.bfloat16)
    return m1, m3, mg, tbr, tbi, f2


def _hy_positions(L):
    t = jnp.arange(L, dtype=jnp.float32)
    bands = jnp.arange(1, 9, dtype=jnp.float32)
    ang = (2.0 * math.pi / L) * t[:, None] * bands[None, :]
    pos = jnp.concatenate([(t / L)[:, None], jnp.cos(ang), jnp.sin(ang)], axis=-1)
    return jnp.pad(pos, ((0, 0), (0, HY_POS_PAD - pos.shape[1])))


MAIN_COL_RANGES = ((0, 1536), (2048, 3072), (3600, 4624), (1536, 2048), (3072, 3584), (4624, 6160), (6176, 8224))
SMALL_COL_RANGES = ((3584, 3600), (6160, 6176))


def _take_cols(a, ranges):
    return jnp.concatenate([a[..., lo:hi] for lo, hi in ranges], axis=-1)


def _pad_cols(a, width, left=0):
    return jnp.pad(a, [(0, 0)] * (a.ndim - 1) + [(left, width - left - a.shape[-1])])


def kernel(x, norm_w, w_in, w_out, hy_conv_w, hy_conv_b, hy_w1, hy_b1, hy_w2, hy_b2, hy_w3, hy_freq, hy_decay,
           hy_skip, mb_conv_w, mb_conv_b, mb_dt_bias, mb_a_log, mb_d, mb_norm_w, ml_conv_w, ml_conv_b, ml_gate_b,
           ml_norm_w, na_qnorm_w, na_knorm_w, na_rpb):
    B, L, D = x.shape
    depth = w_in.shape[0]
    ncols = N_TILES * TILE_N
    f32 = jnp.float32
    pos = _hy_positions(L)
    m1, m3, mg, tbr, tbi, f2 = _hy_constants(L)
    gmean = jnp.kron(jnp.eye(NA_H, dtype=f32), jnp.full((NA_DH, NA_DH), 1.0 / NA_DH)).astype(jnp.bfloat16)

    def expansion(first_lane, heads, width):
        tgt = jnp.arange(heads * width) // width
        e = [(jnp.arange(SMALL_W)[:, None] == (first_lane + d * heads + tgt)[None, :]) for d in range(2)]
        e = jnp.stack(e).astype(jnp.bfloat16)
        return jnp.concatenate([e, e], axis=1)

    ee_mb = expansion(0, MB_H, MB_P)
    ee2_mb = expansion(0, MB_H, CHUNK)
    ee_ml = jnp.concatenate([expansion(ML_LANE, ML_H, ML_DH), expansion(ML_LANE, ML_H, CHUNK)], axis=2)
    for l in range(depth):
        w = _take_cols(w_in[l], MAIN_COL_RANGES).astype(jnp.bfloat16)
        wl = w_in[l]
        zeros = lambda n: jnp.zeros((D, n), wl.dtype)
        ws = jnp.concatenate([wl[:, 3584:3600], wl[:, 6160:6164], wl[:, 6168:6172], zeros(SMALL_W - 24),
                              zeros(ML_LANE), wl[:, 6164:6168], wl[:, 6172:6176], zeros(SMALL_W - 24)],
                             axis=1).astype(jnp.bfloat16)
        taps = _pad_cols(jnp.concatenate([hy_conv_w[l], mb_conv_w[l], ml_conv_w[l]], axis=-1), ncols)
        cbias = _pad_cols(jnp.concatenate([hy_conv_b[l], mb_conv_b[l], ml_conv_b[l]])[None], ncols)
        qk_scale = jnp.concatenate([jnp.tile(na_qnorm_w[l], NA_H) * (NA_DH ** -0.5 * math.log2(math.e)),
                                    jnp.tile(na_knorm_w[l], NA_H)])[None]
        cscale = _pad_cols(qk_scale, ncols, left=(T_NA_Q + N_HY_TILES) * TILE_N)
        proj, hy, small = _inproj(x, norm_w[l][None], w, ws, taps, cbias, cscale, gmean)

        w3_hi = hy_w3[l].astype(jnp.bfloat16)
        w3_lo = (hy_w3[l] - w3_hi.astype(f32)).astype(jnp.bfloat16)
        filt = _hy_filters(pos, _pad_cols(hy_w1[l].T, HY_POS_PAD).T, hy_b1[l][None], hy_w2[l], hy_b2[l][None],
                           jnp.concatenate([w3_hi, w3_hi, w3_lo]), hy_freq[l][None], hy_decay[l][None], L)
        spec = _hy_spectrum(filt, mg, tbr, tbi, f2[:1], L)
        hyp = hy.reshape(B, L // F2N, F2N, N_HY_TILES * GROUP_W).transpose(0, 2, 1, 3)
        y_hy = _hy_conv(hyp, spec, m1, m3, tbr, tbi, f2, hy_skip[l], L)
        y_hy = y_hy.transpose(0, 2, 1, 3).reshape(B, L, GROUP_W)

        a_log2 = -jnp.exp(mb_a_log[l].astype(f32)) * LOG2E
        dt_bias = jnp.concatenate([_pad_cols(mb_dt_bias[l, d][None], SMALL_W, left=d * MB_H) for d in range(2)])
        a_rows = jnp.concatenate([_pad_cols(a_log2[d][None], SMALL_W, left=d * MB_H) for d in range(2)])
        y_mf, y_mb = _ssd2(proj, small, dt_bias, a_rows, ee_mb, ee2_mb)

        bias_i = _pad_cols(ml_gate_b[l][:, 0, :].reshape(1, -1), SMALL_W, left=ML_LANE)
        bias_f = _pad_cols(ml_gate_b[l][:, 1, :].reshape(1, -1), SMALL_W, left=ML_LANE)
        h_f, h_b = _mlstm2(proj, small, bias_i, bias_f, ee_ml)

        y_na = _na(proj, _na_bias_table(na_rpb[l]))

        x = _outproj2(x, y_hy, proj, y_mf, y_mb, h_f, h_b, y_na, jnp.repeat(mb_d[l], MB_P)[None], mb_norm_w[l][None],
                      ml_norm_w[l][None], w_out[l].astype(jnp.bfloat16).reshape(4, GROUP_W, D))
    return x
```

```python
import functools
import math

import jax
import jax.numpy as jnp
from jax import lax
from jax.experimental import pallas as pl
from jax.experimental.pallas import tpu as pltpu

RMS_EPS = 1e-6
GROUP_W = 512
TILE_N = 512
SMALL_W = 128
SMALL_OUT = 2 * SMALL_W
HALO = 16
INPROJ_ROWS = 512
VMEM_LIMIT = 56 * 1024 * 1024

N_HY_TILES = 3
N_CONV_TILES = 7
N_TILES = 16
T_MB_X, T_MB_BC = 0, 1
T_ML_Q, T_ML_K = 2, 3
T_HY_G, T_MB_Z, T_ML_V, T_ML_O, T_ML_Z = 4, 5, 6, 7, 8
T_NA_Q, T_NA_K, T_NA_V, T_NA_G = 9, 10, 11, 12
N_MAIN_TILES = N_TILES - N_HY_TILES

_HI = lax.Precision.HIGHEST


def _silu(x):
    return x * jax.nn.sigmoid(x)


def _inproj_kernel(x_ref, xp_ref, xn_ref, nw_ref, w_ref, ws_ref, taps_ref, cb_ref, cs_ref, gm_ref,
                   o_ref, oh_ref, os_ref, h_ref, acc_ref, *, tm):
    i = pl.program_id(1)
    ni = pl.num_programs(1)

    def norm(xv):
        ms = jnp.mean(xv * xv, axis=-1, keepdims=True)
        return (xv * lax.rsqrt(ms + RMS_EPS) * nw_ref[...]).astype(jnp.bfloat16)

    h_ref[pl.ds(HALO, tm), :] = norm(x_ref[0])
    hp = norm(xp_ref[0])
    hn = norm(xn_ref[0])
    h_ref[pl.ds(0, HALO), :] = jnp.where(i == 0, jnp.zeros_like(hp), hp)
    h_ref[pl.ds(HALO + tm, HALO), :] = jnp.where(i == ni - 1, jnp.zeros_like(hn), hn)
    os_ref[0] = jnp.dot(h_ref[pl.ds(HALO, tm), :], ws_ref[...], preferred_element_type=jnp.float32)

    qk_tiles = (T_NA_Q + N_HY_TILES, T_NA_K + N_HY_TILES)

    def matmul(u):
        wt = w_ref[:, u * TILE_N:(u + 1) * TILE_N]
        if u < N_CONV_TILES:
            acc_ref[u % 2] = jnp.dot(h_ref[...], wt, preferred_element_type=jnp.float32)
        else:
            acc_ref[u % 2, pl.ds(0, tm), :] = jnp.dot(h_ref[pl.ds(HALO, tm), :], wt, preferred_element_type=jnp.float32)

    def epilogue(u):
        a = acc_ref.at[u % 2]
        cols = slice(u * TILE_N, (u + 1) * TILE_N)
        if u < N_CONV_TILES:
            t = taps_ref[:, cols]
            y = (a[pl.ds(HALO - 1, tm), :] * t[0:1] + a[pl.ds(HALO, tm), :] * t[1:2]
                 + a[pl.ds(HALO + 1, tm), :] * t[2:3] + cb_ref[:, cols])
            if u < N_HY_TILES:
                oh_ref[0, :, cols] = y.astype(oh_ref.dtype)
                return
            y = _silu(y)
        else:
            y = a[pl.ds(0, tm), :]
            if u in qk_tiles:
                ms = jnp.dot((y * y).astype(jnp.bfloat16), gm_ref[...], preferred_element_type=jnp.float32)
                y = y * lax.rsqrt(ms + RMS_EPS) * cs_ref[:, cols]
        o_ref[0, :, (u - N_HY_TILES) * TILE_N:(u - N_HY_TILES + 1) * TILE_N] = y.astype(o_ref.dtype)

    matmul(0)
    for u in range(N_TILES):
        if u + 1 < N_TILES:
            matmul(u + 1)
        epilogue(u)


def _inproj(x, nw, w, ws, taps, cbias, cscale, gmean):
    B, L, D = x.shape
    tm = min(INPROJ_ROWS, L)
    ni = L // tm
    hb = tm // HALO
    nlast = L // HALO - 1
    ncols = N_TILES * TILE_N
    const = lambda b, i: (0, 0)
    tok = lambda b, i: (b, i, 0)
    return pl.pallas_call(
        functools.partial(_inproj_kernel, tm=tm),
        grid=(B, ni),
        in_specs=[
            pl.BlockSpec((1, tm, D), tok),
            pl.BlockSpec((1, HALO, D), lambda b, i: (b, jnp.maximum(i * hb - 1, 0), 0)),
            pl.BlockSpec((1, HALO, D), lambda b, i: (b, jnp.minimum((i + 1) * hb, nlast), 0)),
            pl.BlockSpec((1, D), const),
            pl.BlockSpec((D, ncols), const, pipeline_mode=pl.Buffered(1)),
            pl.BlockSpec((D, SMALL_OUT), const),
            pl.BlockSpec((3, ncols), const),
            pl.BlockSpec((1, ncols), const),
            pl.BlockSpec((1, ncols), const),
            pl.BlockSpec((TILE_N, TILE_N), const),
        ],
        out_specs=[
            pl.BlockSpec((1, tm, N_MAIN_TILES * TILE_N), tok),
            pl.BlockSpec((1, tm, N_HY_TILES * TILE_N), tok),
            pl.BlockSpec((1, tm, SMALL_OUT), tok),
        ],
        out_shape=[
            jax.ShapeDtypeStruct((B, L, N_MAIN_TILES * TILE_N), jnp.bfloat16),
            jax.ShapeDtypeStruct((B, L, N_HY_TILES * TILE_N), jnp.bfloat16),
            jax.ShapeDtypeStruct((B, L, SMALL_OUT), jnp.float32),
        ],
        scratch_shapes=[
            pltpu.VMEM((tm + 2 * HALO, D), jnp.bfloat16),
            pltpu.VMEM((2, tm + 2 * HALO, TILE_N), jnp.float32),
        ],
        compiler_params=pltpu.CompilerParams(
            dimension_semantics=("arbitrary", "arbitrary"), vmem_limit_bytes=VMEM_LIMIT),
        name="inproj",
    )(x, x, x, nw, w, ws, taps, cbias, cscale, gmean)


CHUNK = 256
CHUNKS_PER_STEP = 4
STEP_TOKENS = CHUNKS_PER_STEP * CHUNK
NEG = -1e30


def _sub_chunk(ref, k):
    return ref.at[:, pl.ds(k * CHUNK, CHUNK), :]


def _dot_nt(a, b):
    return lax.dot_general(a, b, (((1,), (1,)), ((), ())), preferred_element_type=jnp.float32)


def _dot_tn(a, b):
    return lax.dot_general(a, b, (((0,), (0,)), ((), ())), preferred_element_type=jnp.float32)


def _chunk_mask(q, reverse):
    t = lax.broadcasted_iota(jnp.int32, (q, q), 0)
    s = lax.broadcasted_iota(jnp.int32, (q, q), 1)
    return (s >= t) if reverse else (s <= t)


MB_H = 8
MB_P = 64
MB_N = 128
MB_GW = 256
ML_H = 4
ML_DH = 128


LOG2E = math.log2(math.e)


def _split_cols(x):
    hi = x.astype(jnp.bfloat16)
    lo = (x - hi.astype(jnp.float32)).astype(jnp.bfloat16)
    return jnp.concatenate([hi, lo], axis=1)


def _split_rows(x):
    hi = x.astype(jnp.bfloat16)
    lo = (x - hi.astype(jnp.float32)).astype(jnp.bfloat16)
    return jnp.concatenate([hi, lo], axis=0)


def _tri2(mask):
    tri = mask.astype(jnp.bfloat16)
    return jnp.concatenate([tri, tri], axis=1)


def _ssd_direction(xs_ref, bc_ref, sm_ref, bias, a_row, ee, ee2, s_ref, o_ref, *, reverse, dcol):
    q = CHUNK
    mask = _chunk_mask(q, reverse)
    dt_s = jax.nn.softplus(sm_ref[0] + bias)
    a_s = dt_s * a_row
    c_s = jnp.dot(_tri2(mask), _split_rows(a_s), preferred_element_type=jnp.float32)
    c_t = c_s.T
    full = jnp.dot(_split_cols(jnp.concatenate([c_s, dt_s], axis=0)), ee, preferred_element_type=jnp.float32)
    c_full, dt_full = full[:q], full[q:]
    cc_all = jnp.dot(_split_cols(c_s), ee2, preferred_element_type=jnp.float32)
    far = 0 if reverse else q - 1
    tot_full = c_full[far:far + 1, :]
    x = xs_ref[0].astype(jnp.float32)
    dtx = dt_full * x
    lane = lax.broadcasted_iota(jnp.int32, (1, MB_GW), 1)
    ys = []
    yield
    for g in range(2):
        bg = bc_ref[0, :, g * MB_N:(g + 1) * MB_N]
        cg = bc_ref[0, :, MB_GW + g * MB_N:MB_GW + (g + 1) * MB_N]
        gs = slice(g * MB_GW, (g + 1) * MB_GW)
        gram = _dot_nt(cg, bg)
        dtx_g = dtx[:, gs]
        ms, xms = [], []
        for j in range(4):
            hd = g * 4 + j
            col = dcol + hd
            decay = jnp.exp2(jnp.where(mask, cc_all[:, hd * q:(hd + 1) * q] - c_t[col:col + 1, :], NEG))
            ms.append((gram * decay).astype(jnp.bfloat16))
            xms.append(jnp.where((lane >= j * MB_P) & (lane < (j + 1) * MB_P), dtx_g, 0.0).astype(jnp.bfloat16))
        yg = jnp.dot(jnp.concatenate(ms, axis=1), jnp.concatenate(xms, axis=0), preferred_element_type=jnp.float32)
        s_g = s_ref[:, gs]
        yg = yg + jnp.exp2(c_full[:, gs]) * jnp.dot(cg, s_g.astype(jnp.bfloat16), preferred_element_type=jnp.float32)
        w = jnp.exp2(tot_full[:, gs] - c_full[:, gs])
        s_ref[:, gs] = jnp.exp2(tot_full[:, gs]) * s_g + _dot_tn(bg, (w * dtx_g).astype(jnp.bfloat16))
        ys.append(yg)
        yield
    o_ref[0] = jnp.concatenate(ys, axis=-1).astype(o_ref.dtype)


def _interleave(*gens):
    live = list(gens)
    while live:
        for g in list(live):
            if next(g, StopIteration) is StopIteration:
                live.remove(g)


def _ssd2_kernel(xf_ref, bcf_ref, smf_ref, xb_ref, bcb_ref, smb_ref, bias_ref, a_ref, ee_ref, ee2_ref,
                 of_ref, ob_ref, s_ref):
    @pl.when(pl.program_id(1) == 0)
    def _():
        s_ref[...] = jnp.zeros_like(s_ref)

    for k in range(CHUNKS_PER_STEP):
        f = lambda r: _sub_chunk(r, k)
        g = lambda r: _sub_chunk(r, CHUNKS_PER_STEP - 1 - k)
        _interleave(
            _ssd_direction(f(xf_ref), f(bcf_ref), f(smf_ref), bias_ref[0:1], a_ref[0:1], ee_ref[0], ee2_ref[0],
                           s_ref.at[0], f(of_ref), reverse=False, dcol=0),
            _ssd_direction(g(xb_ref), g(bcb_ref), g(smb_ref), bias_ref[1:2], a_ref[1:2], ee_ref[1], ee2_ref[1],
                           s_ref.at[1], g(ob_ref), reverse=True, dcol=MB_H))


def _ssd2(proj, small, bias, a_row, ee, ee2):
    B, L, _ = proj.shape
    nc = L // STEP_TOKENS
    fw = lambda col: (lambda b, i: (b, i, col))
    bw = lambda col: (lambda b, i: (b, nc - 1 - i, col))
    blk = (1, STEP_TOKENS, GROUP_W)
    sblk = (1, STEP_TOKENS, SMALL_W)
    const2 = lambda b, i: (0, 0)
    const3 = lambda b, i: (0, 0, 0)
    out = jax.ShapeDtypeStruct((B, L, GROUP_W), jnp.bfloat16)
    return pl.pallas_call(
        _ssd2_kernel,
        grid=(B, nc),
        in_specs=[
            pl.BlockSpec(blk, fw(T_MB_X)), pl.BlockSpec(blk, fw(T_MB_BC)), pl.BlockSpec(sblk, fw(0)),
            pl.BlockSpec(blk, bw(T_MB_X)), pl.BlockSpec(blk, bw(T_MB_BC)), pl.BlockSpec(sblk, bw(0)),
            pl.BlockSpec(bias.shape, const2), pl.BlockSpec(a_row.shape, const2),
            pl.BlockSpec(ee.shape, const3), pl.BlockSpec(ee2.shape, const3),
        ],
        out_specs=[pl.BlockSpec(blk, fw(0)), pl.BlockSpec(blk, bw(0))],
        out_shape=[out, out],
        scratch_shapes=[pltpu.VMEM((2, MB_N, GROUP_W), jnp.float32)],
        compiler_params=pltpu.CompilerParams(
            dimension_semantics=("arbitrary", "arbitrary"), vmem_limit_bytes=VMEM_LIMIT),
        name="ssd",
    )(proj, proj, small, proj, proj, small, bias, a_row, ee, ee2)


ML_LANE = 16


def _cummax_rows(u, reverse):
    n = u.shape[0]
    row = lax.broadcasted_iota(jnp.int32, u.shape, 0)
    k = 1
    while k < n:
        if reverse:
            shifted = jnp.where(row < n - k, pltpu.roll(u, n - k, axis=0), NEG)
        else:
            shifted = jnp.where(row >= k, pltpu.roll(u, k, axis=0), NEG)
        u = jnp.maximum(u, shifted)
        k *= 2
    return u


def _mlstm_direction(q_ref, k_ref, v_ref, si_ref, sf_ref, bias_i, bias_f, ee, st_ref, m_ref, o_ref, *, reverse, lane0):
    q = CHUNK
    mask = _chunk_mask(q, reverse)
    gi = si_ref[0] + bias_i
    lf = jax.nn.log_sigmoid(sf_ref[0] + bias_f)
    b = jnp.dot(_tri2(mask), _split_rows(lf), preferred_element_type=jnp.float32)
    u = gi - b
    far = 0 if reverse else q - 1
    m_prev = m_ref[...]
    m_inter = b + m_prev
    m_t = jnp.maximum(m_inter, b + _cummax_rows(u, reverse))
    w_inter = jnp.exp(m_inter - m_t)
    em = jnp.exp(-m_t)
    b_tot = b[far:far + 1, :]
    g = b_tot - b + gi
    m_new = jnp.maximum(b_tot + m_prev, jnp.max(g, axis=0, keepdims=True))
    wk = jnp.exp(g - m_new) * (ML_DH ** -0.5)
    decay = jnp.exp(b_tot + m_prev - m_new)
    m_ref[...] = m_new
    arow = (b - m_t) * LOG2E + math.log2(ML_DH ** -0.5)
    ee_dh, ee_q = ee[:, :GROUP_W], ee[:, GROUP_W:]
    stack = jnp.concatenate([w_inter, em, wk], axis=0)
    full = jnp.dot(_split_cols(stack), ee_dh, preferred_element_type=jnp.float32)
    w_full, em_full, wk_full = full[:q], full[q:2 * q], full[2 * q:]
    a_all = jnp.dot(_split_cols(arow), ee_q, preferred_element_type=jnp.float32)
    dec_full = jnp.dot(_split_cols(jnp.broadcast_to(decay, (8, SMALL_W))), ee_dh,
                       preferred_element_type=jnp.float32)[0:1]
    u_t = (u * LOG2E).T
    ones = jnp.ones((q, ML_DH), jnp.bfloat16)
    hs = []
    yield
    for h in range(ML_H):
        hsl = slice(h * ML_DH, (h + 1) * ML_DH)
        p = jnp.exp2(jnp.where(mask, a_all[:, h * q:(h + 1) * q] + u_t[lane0 + h:lane0 + h + 1, :], NEG))
        qh = q_ref[0, :, hsl]
        kh = k_ref[0, :, hsl]
        v1 = jnp.concatenate([v_ref[0, :, hsl], ones], axis=1)
        s = (_dot_nt(qh, kh) * p).astype(jnp.bfloat16)
        st = st_ref[h]
        inter = jnp.dot(qh, st.astype(jnp.bfloat16), preferred_element_type=jnp.float32)
        both = w_full[:, hsl] * inter[:, :ML_DH], w_full[:, hsl] * inter[:, ML_DH:]
        intra = jnp.dot(s, v1, preferred_element_type=jnp.float32)
        num = both[0] + intra[:, :ML_DH]
        den = both[1] + intra[:, ML_DH:]
        hs.append(num / jnp.maximum(jnp.abs(den), em_full[:, hsl]))
        kw = (kh.astype(jnp.float32) * wk_full[:, hsl]).astype(jnp.bfloat16)
        dh = dec_full[:, hsl]
        st_ref[h] = jnp.concatenate([dh, dh], axis=1) * st + _dot_tn(kw, v1)
        yield
    o_ref[0] = jnp.concatenate(hs, axis=-1).astype(o_ref.dtype)


def _mlstm2_kernel(qf_ref, kf_ref, vf_ref, sif_ref, sff_ref, qb_ref, kb_ref, vb_ref, sib_ref, sfb_ref,
                   bi_ref, bf_ref, ee_ref, of_ref, ob_ref, st_ref, m_ref):
    @pl.when(pl.program_id(1) == 0)
    def _():
        st_ref[...] = jnp.zeros_like(st_ref)
        m_ref[...] = jnp.zeros_like(m_ref)

    for k in range(CHUNKS_PER_STEP):
        f = lambda r: _sub_chunk(r, k)
        g = lambda r: _sub_chunk(r, CHUNKS_PER_STEP - 1 - k)
        _interleave(
            _mlstm_direction(f(qf_ref), f(kf_ref), f(vf_ref), f(sif_ref), f(sff_ref), bi_ref[...], bf_ref[...],
                             ee_ref[0], st_ref.at[0], m_ref.at[0], f(of_ref), reverse=False, lane0=ML_LANE),
            _mlstm_direction(g(qb_ref), g(kb_ref), g(vb_ref), g(sib_ref), g(sfb_ref), bi_ref[...], bf_ref[...],
                             ee_ref[1], st_ref.at[1], m_ref.at[1], g(ob_ref), reverse=True, lane0=ML_LANE + ML_H))


def _mlstm2(proj, small, bias_i, bias_f, ee):
    B, L, _ = proj.shape
    nc = L // STEP_TOKENS
    fw = lambda col: (lambda b, i: (b, i, col))
    bw = lambda col: (lambda b, i: (b, nc - 1 - i, col))
    blk = (1, STEP_TOKENS, GROUP_W)
    sblk = (1, STEP_TOKENS, SMALL_W)
    const2 = lambda b, i: (0, 0)
    out = jax.ShapeDtypeStruct((B, L, GROUP_W), jnp.bfloat16)
    specs = lambda m: [pl.BlockSpec(blk, m(T_ML_Q)), pl.BlockSpec(blk, m(T_ML_K)), pl.BlockSpec(blk, m(T_ML_V)),
                       pl.BlockSpec(sblk, m(0)), pl.BlockSpec(sblk, m(1))]
    return pl.pallas_call(
        _mlstm2_kernel,
        grid=(B, nc),
        in_specs=specs(fw) + specs(bw) + [
            pl.BlockSpec(bias_i.shape, const2), pl.BlockSpec(bias_f.shape, const2),
            pl.BlockSpec(ee.shape, lambda b, i: (0, 0, 0)),
        ],
        out_specs=[pl.BlockSpec(blk, fw(0)), pl.BlockSpec(blk, bw(0))],
        out_shape=[out, out],
        scratch_shapes=[
            pltpu.VMEM((2, ML_H, ML_DH, 2 * ML_DH), jnp.float32),
            pltpu.VMEM((2, 1, SMALL_W), jnp.float32),
        ],
        compiler_params=pltpu.CompilerParams(
            dimension_semantics=("arbitrary", "arbitrary"), vmem_limit_bytes=VMEM_LIMIT),
        name="mlstm",
    )(proj, proj, proj, small, small, proj, proj, proj, small, small, bias_i, bias_f, ee)


def _group_rmsnorm(y, width):
    outs = []
    for g in range(y.shape[-1] // width):
        yg = y[:, g * width:(g + 1) * width]
        ms = jnp.mean(yg * yg, axis=-1, keepdims=True)
        outs.append(yg * lax.rsqrt(ms + RMS_EPS))
    return jnp.concatenate(outs, axis=-1)


def _outproj2_kernel(x_ref, yh_ref, hg_ref, mf_ref, mb_ref, mx_ref, mz_ref, lf_ref, lb_ref, lo_ref, lz_ref, yn_ref,
                     dsk_ref, mnw_ref, lnw_ref, w_ref, o_ref):
    f32 = jnp.float32
    up = lambda r: r[0].astype(f32)
    yh = (up(yh_ref) * _silu(up(hg_ref))).astype(jnp.bfloat16)
    acc = jnp.dot(yh, w_ref[0], preferred_element_type=f32)
    ym = (up(mf_ref) + up(mb_ref) + up(mx_ref) * dsk_ref[...]) * _silu(up(mz_ref))
    ym = (_group_rmsnorm(ym, MB_GW) * mnw_ref[...]).astype(jnp.bfloat16)
    acc += jnp.dot(ym, w_ref[1], preferred_element_type=f32)
    yl = (up(lf_ref) + up(lb_ref)) * jax.nn.sigmoid(up(lo_ref))
    yl = (_group_rmsnorm(yl, ML_DH) * lnw_ref[...] * _silu(up(lz_ref))).astype(jnp.bfloat16)
    acc += jnp.dot(yl, w_ref[2], preferred_element_type=f32)
    acc += jnp.dot(yn_ref[0], w_ref[3], preferred_element_type=f32)
    o_ref[0] = x_ref[0] + acc


def _outproj2(x, y_hy, proj, y_mf, y_mb, h_f, h_b, y_na, dskip, mb_nw, ml_nw, w_out):
    B, L, D = x.shape
    tm = min(512, L)
    tok = lambda b, i: (b, i, 0)
    col = lambda c: (lambda b, i: (b, i, c))
    blk = (1, tm, GROUP_W)
    vec = pl.BlockSpec((1, GROUP_W), lambda b, i: (0, 0))
    return pl.pallas_call(
        _outproj2_kernel,
        grid=(B, L // tm),
        in_specs=[
            pl.BlockSpec((1, tm, D), tok),
            pl.BlockSpec(blk, tok), pl.BlockSpec(blk, col(T_HY_G)),
            pl.BlockSpec(blk, tok), pl.BlockSpec(blk, tok), pl.BlockSpec(blk, col(T_MB_X)), pl.BlockSpec(blk, col(T_MB_Z)),
            pl.BlockSpec(blk, tok), pl.BlockSpec(blk, tok), pl.BlockSpec(blk, col(T_ML_O)), pl.BlockSpec(blk, col(T_ML_Z)),
            pl.BlockSpec(blk, tok),
            vec, vec, vec,
            pl.BlockSpec((4, GROUP_W, D), lambda b, i: (0, 0, 0)),
        ],
        out_specs=pl.BlockSpec((1, tm, D), tok),
        out_shape=jax.ShapeDtypeStruct((B, L, D), jnp.float32),
        compiler_params=pltpu.CompilerParams(
            dimension_semantics=("arbitrary", "arbitrary"), vmem_limit_bytes=VMEM_LIMIT),
        name="outproj",
    )(x, y_hy, proj, y_mf, y_mb, proj, proj, h_f, h_b, proj, proj, y_na, dskip, mb_nw, ml_nw, w_out)


GRID_W = 64
NA_KR = 8
NA_KC = 16
NA_H = 8
NA_DH = 64
ROWS_PER_BLOCK = 8
BLOCK_TOK = ROWS_PER_BLOCK * GRID_W


def _na_kernel(q_ref, kp_ref, kc_ref, kn_ref, vp_ref, vc_ref, vn_ref, g_ref, tbl_ref, o_ref, k_s, v_s, *, rows):
    mblk = pl.program_id(1)
    npair = NA_H // 2
    pw = 2 * NA_DH
    ones = jnp.ones((BLOCK_TOK, pw), jnp.bfloat16)
    for t, (kr, vr) in enumerate(((kp_ref, vp_ref), (kc_ref, vc_ref), (kn_ref, vn_ref))):
        k_s[pl.ds(t * BLOCK_TOK, BLOCK_TOK), :] = kr[0]
        for hp in range(npair):
            v_s[pl.ds(t * BLOCK_TOK, BLOCK_TOK), 2 * hp * pw:(2 * hp + 1) * pw] = vr[0, :, hp * pw:(hp + 1) * pw]
            v_s[pl.ds(t * BLOCK_TOK, BLOCK_TOK), (2 * hp + 1) * pw:(2 * hp + 2) * pw] = ones
    lane = lax.broadcasted_iota(jnp.int32, (GRID_W, pw), 1)
    first = lane < NA_DH
    win = NA_KR * GRID_W
    for j in range(ROWS_PER_BLOCK):
        r = mblk * ROWS_PER_BLOCK + j
        rs = jnp.clip(r - NA_KR // 2, 0, rows - NA_KR)
        didx = r - rs
        off = pl.multiple_of((rs - (mblk - 1) * ROWS_PER_BLOCK) * GRID_W, GRID_W)
        ss = []
        for hp in range(npair):
            ls = slice(hp * pw, (hp + 1) * pw)
            qp = q_ref[0, j * GRID_W:(j + 1) * GRID_W, ls]
            zero = jnp.zeros_like(qp)
            q2 = jnp.concatenate([jnp.where(first, qp, zero), jnp.where(first, zero, qp)], axis=0)
            ss.append(_dot_nt(q2, k_s[pl.ds(off, win), ls]))
        s = jnp.concatenate(ss, axis=0) + tbl_ref[didx].astype(jnp.float32)
        e = jnp.exp2(s - jnp.max(s, axis=-1, keepdims=True)).astype(jnp.bfloat16)
        outs = []
        for hp in range(npair):
            ov = jnp.dot(e[hp * pw:(hp + 1) * pw], v_s[pl.ds(off, win), 2 * hp * pw:(2 * hp + 2) * pw],
                         preferred_element_type=jnp.float32)
            o2 = ov[:, :pw] / ov[:, pw:]
            outs.append(jnp.where(first, o2[:GRID_W], o2[GRID_W:]))
        o = jnp.concatenate(outs, axis=-1)
        gate = _silu(g_ref[0, j * GRID_W:(j + 1) * GRID_W, :].astype(jnp.float32))
        o_ref[0, j * GRID_W:(j + 1) * GRID_W, :] = (o * gate).astype(o_ref.dtype)


def _na(proj, tbl):
    B, L, _ = proj.shape
    rows = L // GRID_W
    nb = rows // ROWS_PER_BLOCK
    prev = lambda col: (lambda b, m: (b, jnp.maximum(m - 1, 0), col))
    cur = lambda col: (lambda b, m: (b, m, col))
    nxt = lambda col: (lambda b, m: (b, jnp.minimum(m + 1, nb - 1), col))
    blk = (1, BLOCK_TOK, GROUP_W)
    return pl.pallas_call(
        functools.partial(_na_kernel, rows=rows),
        grid=(B, nb),
        in_specs=[
            pl.BlockSpec(blk, cur(T_NA_Q)),
            pl.BlockSpec(blk, prev(T_NA_K)), pl.BlockSpec(blk, cur(T_NA_K)), pl.BlockSpec(blk, nxt(T_NA_K)),
            pl.BlockSpec(blk, prev(T_NA_V)), pl.BlockSpec(blk, cur(T_NA_V)), pl.BlockSpec(blk, nxt(T_NA_V)),
            pl.BlockSpec(blk, cur(T_NA_G)),
            pl.BlockSpec(tbl.shape, lambda b, m: (0, 0, 0)),
        ],
        out_specs=pl.BlockSpec(blk, cur(0)),
        out_shape=jax.ShapeDtypeStruct((B, L, GROUP_W), jnp.bfloat16),
        scratch_shapes=[
            pltpu.VMEM((3 * BLOCK_TOK, GROUP_W), jnp.bfloat16),
            pltpu.VMEM((3 * BLOCK_TOK, 2 * GROUP_W), jnp.bfloat16),
        ],
        compiler_params=pltpu.CompilerParams(
            dimension_semantics=("arbitrary", "arbitrary"), vmem_limit_bytes=VMEM_LIMIT),
        name="nbr_attn",
    )(proj, proj, proj, proj, proj, proj, proj, proj, tbl)


def _na_bias_table(rpb):
    nco = 2 * NA_KC - 1
    rows_d = jnp.stack([rpb[:, NA_KR - 1 - d:2 * NA_KR - 1 - d, :] for d in range(NA_KR)])
    lpad = GRID_W - NA_KC
    ext = jnp.pad(rows_d * math.log2(math.e), ((0, 0), (0, 0), (0, 0), (lpad, 2 * GRID_W - lpad - nco)))
    lead = ext.shape[:3]
    skew = jnp.broadcast_to(ext[..., None, :], lead + (GRID_W, 2 * GRID_W)).reshape(lead + (2 * GRID_W * GRID_W,))
    skew = skew[..., :GRID_W * (2 * GRID_W - 1)].reshape(lead + (GRID_W, 2 * GRID_W - 1))[..., GRID_W - 1:]
    w = jnp.arange(GRID_W)
    c = jnp.arange(GRID_W)
    cstart = jnp.clip(w - NA_KC // 2, 0, GRID_W - NA_KC)
    inwin = (c[None, :] >= cstart[:, None]) & (c[None, :] < cstart[:, None] + NA_KC)
    t = jnp.where(inwin, skew, NEG)
    t = t.transpose(0, 1, 3, 2, 4).reshape(NA_KR, NA_H * GRID_W, NA_KR * GRID_W)
    return t.astype(jnp.bfloat16)


HY_LANES = 256
TBL_LANES = 128
F2N = 128
TW_B = 16
PAD_ROWS = 8
HY_POS_PAD = 128


def _cmul(ar, ai, br, bi):
    return ar * br - ai * bi, ar * bi + ai * br


def _pack_c(re, im):
    half = jnp.uint32(0x8000)
    rb = lax.bitcast_convert_type(re, jnp.uint32) + half
    ib = lax.bitcast_convert_type(im, jnp.uint32) + half
    return (rb & jnp.uint32(0xFFFF0000)) | (ib >> 16)


def _unpack_c(w):
    re = lax.bitcast_convert_type(w & jnp.uint32(0xFFFF0000), jnp.float32)
    im = lax.bitcast_convert_type(w << 16, jnp.float32)
    return re, im


def _tile_lanes(t):
    return jnp.concatenate([t] * (HY_LANES // TBL_LANES), axis=1)


LANE_PARTS = HY_LANES // TBL_LANES


def _rows_load(refs_or_ref, idx):
    parts = [refs_or_ref[p, idx, :] for p in range(LANE_PARTS)] if not isinstance(refs_or_ref, (list, tuple)) \
        else [r[idx, :] for r in refs_or_ref]
    return jnp.concatenate(parts, axis=1)


def _rows_store(buf_ref, idx, val):
    for p in range(LANE_PARTS):
        buf_ref[p, idx, :] = val[:, p * TBL_LANES:(p + 1) * TBL_LANES]


def _stage_a_store(buf_ref, a, n2, tr, ti, n1f, rs):
    ar, ai = _cmul(a[:n1f], a[n1f:], _tile_lanes(tr), _tile_lanes(ti))
    _rows_store(buf_ref, pl.ds(pl.multiple_of(n2 * rs, 8), n1f), _pack_c(ar, ai))


def _stage_b_load(buf_ref, k1, rs):
    yr, yi = _unpack_c(_rows_load(buf_ref, pl.ds(k1, F2N, stride=rs)))
    return jnp.concatenate([yr, yi], axis=0).astype(jnp.bfloat16)


def _hy_spec_kernel(*refs, n1f, h1, rs, ns):
    ff_refs, fb_refs = refs[:LANE_PARTS], refs[LANE_PARTS:2 * LANE_PARTS]
    mg_ref, tbr_ref, tbi_ref, f2_ref, o_ref, buf_ref = refs[2 * LANE_PARTS:]
    s = pl.program_id(2)

    @pl.when(s < ns)
    def _():
        for b in range(TW_B):
            n2 = s * TW_B + b
            hf = _rows_load(ff_refs, pl.ds(n2, h1, stride=F2N))
            hb = _rows_load(fb_refs, pl.ds(F2N - n2, h1, stride=F2N))
            xs = jnp.concatenate([hf, hb], axis=0).astype(jnp.bfloat16)
            a = jnp.dot(mg_ref[0], xs, preferred_element_type=jnp.float32)
            _stage_a_store(buf_ref, a, n2, tbr_ref[b], tbi_ref[b], n1f, rs)

    @pl.when(s >= ns)
    def _():
        for kk in range(TW_B):
            k1 = (s - ns) * TW_B + kk
            o_ref[0, kk] = jnp.dot(f2_ref[0], _stage_b_load(buf_ref, k1, rs), preferred_element_type=jnp.float32)


def _hy_spectrum(filt, mg, tbr, tbi, f2, L):
    n1f = 2 * L // F2N
    h1 = n1f // 2
    rs = n1f + PAD_ROWS
    ns = F2N // TW_B
    nk = n1f // TW_B
    ncb = GROUP_W // HY_LANES
    a_idx = lambda o, c, s: (jnp.minimum(s, ns - 1), 0, 0)
    once = pl.Buffered(1)
    nparts = GROUP_W // TBL_LANES

    def filt_spec(direction, part):
        return pl.BlockSpec(
            (L + F2N, TBL_LANES),
            lambda o, c, s: (0, (o * 2 + direction) * nparts + c * LANE_PARTS + part), pipeline_mode=once)

    return pl.pallas_call(
        functools.partial(_hy_spec_kernel, n1f=n1f, h1=h1, rs=rs, ns=ns),
        grid=(2, ncb, ns + nk),
        in_specs=[filt_spec(0, p) for p in range(LANE_PARTS)] + [filt_spec(1, p) for p in range(LANE_PARTS)] + [
            pl.BlockSpec((1, 2 * n1f, n1f), a_idx),
            pl.BlockSpec(tbr.shape, lambda o, c, s: (0, 0, 0)),
            pl.BlockSpec(tbi.shape, lambda o, c, s: (0, 0, 0)),
            pl.BlockSpec((1, 2 * F2N, 2 * F2N), lambda o, c, s: (0, 0, 0)),
        ],
        out_specs=pl.BlockSpec((1, TW_B, 2 * F2N, HY_LANES), lambda o, c, s: (o, jnp.maximum(s - ns, 0), 0, c)),
        out_shape=jax.ShapeDtypeStruct((2, n1f, 2 * F2N, GROUP_W), jnp.float32),
        scratch_shapes=[pltpu.VMEM((LANE_PARTS, F2N * rs, TBL_LANES), jnp.uint32)],
        compiler_params=pltpu.CompilerParams(
            dimension_semantics=("arbitrary", "arbitrary", "arbitrary"), vmem_limit_bytes=VMEM_LIMIT),
        name="hyena_spectrum",
    )(*([filt] * (2 * LANE_PARTS)), mg, tbr, tbi, f2)


def _hy_conv_kernel(v_ref, x1_ref, x2_ref, g_ref, m1_ref, m3_ref, tbr_ref, tbi_ref, f2_ref, skip_ref,
                    o_ref, buf_ref, z_ref, *, n1f, h1, rs, ns, nk):
    s = pl.program_id(2)
    p1 = ns
    p2 = p1 + nk
    p3 = p2 + ns
    p4 = p3 + nk

    def stage_a(xs, n2, b):
        a = jnp.dot(m1_ref[0], xs, preferred_element_type=jnp.float32)
        _stage_a_store(buf_ref, a, n2, tbr_ref[b], tbi_ref[b], n1f, rs)

    def stage_b(order, kb):
        for kk in range(TW_B):
            k1 = kb * TW_B + kk
            y = jnp.dot(f2_ref[0], _stage_b_load(buf_ref, k1, rs), preferred_element_type=jnp.float32)
            g = g_ref[0, kk]
            yr, yi = _cmul(y[:F2N], y[F2N:], g[:F2N], g[F2N:])
            ys = jnp.concatenate([yr, yi], axis=0).astype(jnp.bfloat16)
            z = jnp.dot(f2_ref[1], ys, preferred_element_type=jnp.float32)
            _rows_store(buf_ref, pl.ds(k1, F2N, stride=rs), _pack_c(z[:F2N], z[F2N:]))

    def stage_c(n2, b):
        zr, zi = _unpack_c(_rows_load(buf_ref, pl.ds(pl.multiple_of(n2 * rs, 8), n1f)))
        wr, wi = _cmul(zr, zi, _tile_lanes(tbr_ref[b]), -_tile_lanes(tbi_ref[b]))
        ws = jnp.concatenate([wr, wi], axis=0).astype(jnp.bfloat16)
        return jnp.dot(m3_ref[0], ws, preferred_element_type=jnp.float32)

    @pl.when(s < p1)
    def _():
        for b in range(TW_B):
            xs = jnp.concatenate([v_ref[0, b], v_ref[1, b]], axis=0)
            stage_a(xs, s * TW_B + b, b)

    @pl.when((s >= p1) & (s < p2))
    def _():
        stage_b(0, s - p1)

    @pl.when((s >= p2) & (s < p3))
    def _():
        for b in range(TW_B):
            n2 = (s - p2) * TW_B + b
            c = stage_c(n2, b)
            zs = []
            for r in range(2):
                vv = v_ref[r, b].astype(jnp.float32)
                z = x1_ref[r, b].astype(jnp.float32) * (c[r * h1:(r + 1) * h1] + skip_ref[0:1, :] * vv)
                zb = z.astype(jnp.bfloat16)
                z_ref[n2, r] = zb
                zs.append(zb)
            stage_a(jnp.concatenate(zs, axis=0), n2, b)

    @pl.when((s >= p3) & (s < p4))
    def _():
        stage_b(1, s - p3)

    @pl.when(s >= p4)
    def _():
        for b in range(TW_B):
            n2 = (s - p4) * TW_B + b
            c = stage_c(n2, b)
            for r in range(2):
                zz = z_ref[n2, r].astype(jnp.float32)
                y = x2_ref[r, b].astype(jnp.float32) * (c[r * h1:(r + 1) * h1] + skip_ref[1:2, :] * zz)
                o_ref[r, b] = y.astype(o_ref.dtype)


def _hy_conv(hyp, spec, m1, m3, tbr, tbi, f2, skip, L):
    B = hyp.shape[0]
    n1f = 2 * L // F2N
    h1 = n1f // 2
    rs = n1f + PAD_ROWS
    ns = F2N // TW_B
    nk = n1f // TW_B
    ncb = GROUP_W // HY_LANES
    p1, p2, p3, p4 = ns, ns + nk, 2 * ns + nk, 2 * ns + 2 * nk
    clip = lambda v, hi: jnp.clip(v, 0, hi)
    tblk = (2, TW_B, h1, HY_LANES)
    v_idx = lambda c, p, s: (p, jnp.where(s < p1, s, clip(s - p2, ns - 1)), 0, c)
    x1_idx = lambda c, p, s: (p, clip(s - p2, ns - 1), 0, ncb + c)
    x2_idx = lambda c, p, s: (p, clip(s - p4, ns - 1), 0, 2 * ncb + c)
    g_idx = lambda c, p, s: (jnp.where(s < p3, 0, 1), jnp.where(s < p3, clip(s - p1, nk - 1), clip(s - p3, nk - 1)), 0, c)
    m1_idx = lambda c, p, s: (jnp.where(s < p1, s, clip(s - p2, ns - 1)), 0, 0)
    m3_idx = lambda c, p, s: (jnp.where(s < p4, clip(s - p2, ns - 1), s - p4), 0, 0)
    return pl.pallas_call(
        functools.partial(_hy_conv_kernel, n1f=n1f, h1=h1, rs=rs, ns=ns, nk=nk),
        grid=(ncb, B // 2, p4 + ns),
        in_specs=[
            pl.BlockSpec(tblk, v_idx),
            pl.BlockSpec(tblk, x1_idx),
            pl.BlockSpec(tblk, x2_idx),
            pl.BlockSpec((1, TW_B, 2 * F2N, HY_LANES), g_idx),
            pl.BlockSpec((1, 2 * n1f, 2 * h1), m1_idx),
            pl.BlockSpec((1, 2 * h1, 2 * n1f), m3_idx),
            pl.BlockSpec(tbr.shape, lambda c, p, s: (0, 0, 0)),
            pl.BlockSpec(tbi.shape, lambda c, p, s: (0, 0, 0)),
            pl.BlockSpec((2, 2 * F2N, 2 * F2N), lambda c, p, s: (0, 0, 0)),
            pl.BlockSpec((2, HY_LANES), lambda c, p, s: (0, c)),
        ],
        out_specs=pl.BlockSpec(tblk, lambda c, p, s: (p, clip(s - p4, ns - 1), 0, c)),
        out_shape=jax.ShapeDtypeStruct((B, F2N, h1, GROUP_W), jnp.bfloat16),
        scratch_shapes=[
            pltpu.VMEM((LANE_PARTS, F2N * rs, TBL_LANES), jnp.uint32),
            pltpu.VMEM((F2N, 2, h1, HY_LANES), jnp.bfloat16),
        ],
        compiler_params=pltpu.CompilerParams(
            dimension_semantics=("arbitrary", "arbitrary", "arbitrary"), vmem_limit_bytes=VMEM_LIMIT),
        name="hyena_conv",
    )(hyp, hyp, hyp, spec, m1, m3, tbr, tbi, f2, skip)


def _hy_filter_kernel(pos_ref, w1_ref, b1_ref, w2_ref, b2_ref, w3_ref, fr_ref, dec_ref, o_ref, *, nblk):
    i = pl.program_id(0)
    pos = pos_ref[...]
    fr = fr_ref[...]
    hid = jnp.sin(fr * (jnp.dot(pos, w1_ref[...], precision=_HI, preferred_element_type=jnp.float32) + b1_ref[...]))
    hid = jnp.sin(fr * (jnp.dot(hid, w2_ref[...], precision=_HI, preferred_element_type=jnp.float32) + b2_ref[...]))
    hi = hid.astype(jnp.bfloat16)
    lo = (hid - hi.astype(jnp.float32)).astype(jnp.bfloat16)
    filt = jnp.dot(jnp.concatenate([hi, lo, hi], axis=1), w3_ref[...], preferred_element_type=jnp.float32)
    filt = filt * jnp.exp(-pos[:, 0:1] * dec_ref[...])
    o_ref[...] = jnp.where(i < nblk, filt, 0.0)


def _hy_filters(pos, w1, b1, w2, b2, w3, freq, decay, L):
    nblk = L // F2N
    nh = w2.shape[0]
    no = w3.shape[1]
    const = lambda i: (0, 0)
    return pl.pallas_call(
        functools.partial(_hy_filter_kernel, nblk=nblk),
        grid=(nblk + 1,),
        in_specs=[
            pl.BlockSpec((F2N, HY_POS_PAD), lambda i: (jnp.minimum(i, nblk - 1), 0)),
            pl.BlockSpec((HY_POS_PAD, nh), const), pl.BlockSpec((1, nh), const),
            pl.BlockSpec((nh, nh), const), pl.BlockSpec((1, nh), const),
            pl.BlockSpec((3 * nh, no), const), pl.BlockSpec((1, nh), const), pl.BlockSpec((1, no), const),
        ],
        out_specs=pl.BlockSpec((F2N, no), lambda i: (i, 0)),
        out_shape=jax.ShapeDtypeStruct((L + F2N, no), jnp.float32),
        compiler_params=pltpu.CompilerParams(dimension_semantics=("arbitrary",), vmem_limit_bytes=VMEM_LIMIT),
        name="hyena_filters",
    )(pos, w1, b1, w2, b2, w3, freq, decay)


def _hy_constants(L):
    n = 2 * L
    n1f = n // F2N
    h1 = n1f // 2
    na = F2N // TW_B

    def cis(num, den):
        ang = (-2.0 * math.pi / den) * (num % den).astype(jnp.float32)
        return jnp.cos(ang), jnp.sin(ang)

    k1 = jnp.arange(n1f)
    n1 = jnp.arange(n1f)
    a = jnp.arange(na)
    f1r, f1i = cis(k1[:, None] * n1[None, :], n1f)
    tar, tai = cis(k1[None, :] * (TW_B * a)[:, None], n)
    mr, mi = _cmul(f1r[None], f1i[None], tar[:, :, None], tai[:, :, None])

    def blockform(r, i):
        return jnp.concatenate([jnp.concatenate([r, -i], axis=-1), jnp.concatenate([i, r], axis=-1)], axis=-2)

    m1 = blockform(mr[:, :, :h1], mi[:, :, :h1]).astype(jnp.bfloat16)
    m3r = jnp.swapaxes(mr[:, :, :h1], 1, 2) / n
    m3i = -jnp.swapaxes(mi[:, :, :h1], 1, 2) / n
    m3 = blockform(m3r, m3i).astype(jnp.bfloat16)
    mgr = jnp.concatenate([mr[:, :, :h1], mr[:, :, h1:][:, :, ::-1]], axis=-1)
    mgi = jnp.concatenate([mi[:, :, :h1], mi[:, :, h1:][:, :, ::-1]], axis=-1)
    mg = jnp.concatenate([mgr, mgi], axis=1).astype(jnp.bfloat16)
    b = jnp.arange(TW_B)
    tbr, tbi = cis(k1[None, :] * b[:, None], n)
    tbr = jnp.broadcast_to(tbr[:, :, None], (TW_B, n1f, TBL_LANES))
    tbi = jnp.broadcast_to(tbi[:, :, None], (TW_B, n1f, TBL_LANES))
    k2 = jnp.arange(F2N)
    f2r, f2i = cis(k2[:, None] * k2[None, :], F2N)
    f2 = jnp.stack([blockform(f2r, f2i), blockform(f2r, -f2i)]).astype(jnp.bfloat16)
    return m1, m3, mg, tbr, tbi, f2


def _hy_positions(L):
    t = jnp.arange(L, dtype=jnp.float32)
    bands = jnp.arange(1, 9, dtype=jnp.float32)
    ang = (2.0 * math.pi / L) * t[:, None] * bands[None, :]
    pos = jnp.concatenate([(t / L)[:, None], jnp.cos(ang), jnp.sin(ang)], axis=-1)
    return jnp.pad(pos, ((0, 0), (0, HY_POS_PAD - pos.shape[1])))


MAIN_COL_RANGES = ((0, 1536), (2048, 3072), (3600, 4624), (1536, 2048), (3072, 3584), (4624, 6160), (6176, 8224))
SMALL_COL_RANGES = ((3584, 3600), (6160, 6176))


def _take_cols(a, ranges):
    return jnp.concatenate([a[..., lo:hi] for lo, hi in ranges], axis=-1)


def _pad_cols(a, width, left=0):
    return jnp.pad(a, [(0, 0)] * (a.ndim - 1) + [(left, width - left - a.shape[-1])])


def kernel(x, norm_w, w_in, w_out, hy_conv_w, hy_conv_b, hy_w1, hy_b1, hy_w2, hy_b2, hy_w3, hy_freq, hy_decay,
           hy_skip, mb_conv_w, mb_conv_b, mb_dt_bias, mb_a_log, mb_d, mb_norm_w, ml_conv_w, ml_conv_b, ml_gate_b,
           ml_norm_w, na_qnorm_w, na_knorm_w, na_rpb):
    B, L, D = x.shape
    depth = w_in.shape[0]
    ncols = N_TILES * TILE_N
    f32 = jnp.float32
    pos = _hy_positions(L)
    m1, m3, mg, tbr, tbi, f2 = _hy_constants(L)
    gmean = jnp.kron(jnp.eye(NA_H, dtype=f32), jnp.full((NA_DH, NA_DH), 1.0 / NA_DH)).astype(jnp.bfloat16)

    def expansion(first_lane, heads, width):
        tgt = jnp.arange(heads * width) // width
        e = [(jnp.arange(SMALL_W)[:, None] == (first_lane + d * heads + tgt)[None, :]) for d in range(2)]
        e = jnp.stack(e).astype(jnp.bfloat16)
        return jnp.concatenate([e, e], axis=1)

    ee_mb = expansion(0, MB_H, MB_P)
    ee2_mb = expansion(0, MB_H, CHUNK)
    ee_ml = jnp.concatenate([expansion(ML_LANE, ML_H, ML_DH), expansion(ML_LANE, ML_H, CHUNK)], axis=2)
    for l in range(depth):
        w = _take_cols(w_in[l], MAIN_COL_RANGES).astype(jnp.bfloat16)
        wl = w_in[l]
        zeros = lambda n: jnp.zeros((D, n), wl.dtype)
        ws = jnp.concatenate([wl[:, 3584:3600], wl[:, 6160:6164], wl[:, 6168:6172], zeros(SMALL_W - 24),
                              zeros(ML_LANE), wl[:, 6164:6168], wl[:, 6172:6176], zeros(SMALL_W - 24)],
                             axis=1).astype(jnp.bfloat16)
        taps = _pad_cols(jnp.concatenate([hy_conv_w[l], mb_conv_w[l], ml_conv_w[l]], axis=-1), ncols)
        cbias = _pad_cols(jnp.concatenate([hy_conv_b[l], mb_conv_b[l], ml_conv_b[l]])[None], ncols)
        qk_scale = jnp.concatenate([jnp.tile(na_qnorm_w[l], NA_H) * (NA_DH ** -0.5 * math.log2(math.e)),
                                    jnp.tile(na_knorm_w[l], NA_H)])[None]
        cscale = _pad_cols(qk_scale, ncols, left=(T_NA_Q + N_HY_TILES) * TILE_N)
        proj, hy, small = _inproj(x, norm_w[l][None], w, ws, taps, cbias, cscale, gmean)

        w3_hi = hy_w3[l].astype(jnp.bfloat16)
        w3_lo = (hy_w3[l] - w3_hi.astype(f32)).astype(jnp.bfloat16)
        filt = _hy_filters(pos, _pad_cols(hy_w1[l].T, HY_POS_PAD).T, hy_b1[l][None], hy_w2[l], hy_b2[l][None],
                           jnp.concatenate([w3_hi, w3_hi, w3_lo]), hy_freq[l][None], hy_decay[l][None], L)
        spec = _hy_spectrum(filt, mg, tbr, tbi, f2[:1], L)
        hyp = hy.reshape(B, L // F2N, F2N, N_HY_TILES * GROUP_W).transpose(0, 2, 1, 3)
        y_hy = _hy_conv(hyp, spec, m1, m3, tbr, tbi, f2, hy_skip[l], L)
        y_hy = y_hy.transpose(0, 2, 1, 3).reshape(B, L, GROUP_W)

        a_log2 = -jnp.exp(mb_a_log[l].astype(f32)) * LOG2E
        dt_bias = jnp.concatenate([_pad_cols(mb_dt_bias[l, d][None], SMALL_W, left=d * MB_H) for d in range(2)])
        a_rows = jnp.concatenate([_pad_cols(a_log2[d][None], SMALL_W, left=d * MB_H) for d in range(2)])
        y_mf, y_mb = _ssd2(proj, small, dt_bias, a_rows, ee_mb, ee2_mb)

        bias_i = _pad_cols(ml_gate_b[l][:, 0, :].reshape(1, -1), SMALL_W, left=ML_LANE)
        bias_f = _pad_cols(ml_gate_b[l][:, 1, :].reshape(1, -1), SMALL_W, left=ML_LANE)
        h_f, h_b = _mlstm2(proj, small, bias_i, bias_f, ee_ml)

        y_na = _na(proj, _na_bias_table(na_rpb[l]))

        x = _outproj2(x, y_hy, proj, y_mf, y_mb, h_f, h_b, y_na, jnp.repeat(mb_d[l], MB_P)[None], mb_norm_w[l][None],
                      ml_norm_w[l][None], w_out[l].astype(jnp.bfloat16).reshape(4, GROUP_W, D))
    return x
```

```python
import functools
import math

import jax
import jax.numpy as jnp
from jax import lax
from jax.experimental import pallas as pl
from jax.experimental.pallas import tpu as pltpu

RMS_EPS = 1e-6
GROUP_W = 512
TILE_N = 512
SMALL_W = 128
SMALL_OUT = 2 * SMALL_W
HALO = 16
INPROJ_ROWS = 512
VMEM_LIMIT = 56 * 1024 * 1024

N_HY_TILES = 3
N_CONV_TILES = 7
N_TILES = 16
T_MB_X, T_MB_BC = 0, 1
T_ML_Q, T_ML_K = 2, 3
T_HY_G, T_MB_Z, T_ML_V, T_ML_O, T_ML_Z = 4, 5, 6, 7, 8
T_NA_Q, T_NA_K, T_NA_V, T_NA_G = 9, 10, 11, 12
N_MAIN_TILES = N_TILES - N_HY_TILES

_HI = lax.Precision.HIGHEST


def _silu(x):
    return x * jax.nn.sigmoid(x)


def _inproj_kernel(x_ref, xp_ref, xn_ref, nw_ref, w_ref, ws_ref, taps_ref, cb_ref, cs_ref, gm_ref,
                   o_ref, oh_ref, os_ref, h_ref, acc_ref, *, tm):
    i = pl.program_id(1)
    ni = pl.num_programs(1)

    def norm(xv):
        ms = jnp.mean(xv * xv, axis=-1, keepdims=True)
        return (xv * lax.rsqrt(ms + RMS_EPS) * nw_ref[...]).astype(jnp.bfloat16)

    h_ref[pl.ds(HALO, tm), :] = norm(x_ref[0])
    hp = norm(xp_ref[0])
    hn = norm(xn_ref[0])
    h_ref[pl.ds(0, HALO), :] = jnp.where(i == 0, jnp.zeros_like(hp), hp)
    h_ref[pl.ds(HALO + tm, HALO), :] = jnp.where(i == ni - 1, jnp.zeros_like(hn), hn)
    os_ref[0] = jnp.dot(h_ref[pl.ds(HALO, tm), :], ws_ref[...], preferred_element_type=jnp.float32)

    qk_tiles = (T_NA_Q + N_HY_TILES, T_NA_K + N_HY_TILES)

    def matmul(u):
        wt = w_ref[:, u * TILE_N:(u + 1) * TILE_N]
        if u < N_CONV_TILES:
            acc_ref[u % 2] = jnp.dot(h_ref[...], wt, preferred_element_type=jnp.float32)
        else:
            acc_ref[u % 2, pl.ds(0, tm), :] = jnp.dot(h_ref[pl.ds(HALO, tm), :], wt, preferred_element_type=jnp.float32)

    def epilogue(u):
        a = acc_ref.at[u % 2]
        cols = slice(u * TILE_N, (u + 1) * TILE_N)
        if u < N_CONV_TILES:
            t = taps_ref[:, cols]
            y = (a[pl.ds(HALO - 1, tm), :] * t[0:1] + a[pl.ds(HALO, tm), :] * t[1:2]
                 + a[pl.ds(HALO + 1, tm), :] * t[2:3] + cb_ref[:, cols])
            if u < N_HY_TILES:
                oh_ref[0, :, cols] = y.astype(oh_ref.dtype)
                return
            y = _silu(y)
        else:
            y = a[pl.ds(0, tm), :]
            if u in qk_tiles:
                ms = jnp.dot((y * y).astype(jnp.bfloat16), gm_ref[...], preferred_element_type=jnp.float32)
                y = y * lax.rsqrt(ms + RMS_EPS) * cs_ref[:, cols]
        o_ref[0, :, (u - N_HY_TILES) * TILE_N:(u - N_HY_TILES + 1) * TILE_N] = y.astype(o_ref.dtype)

    matmul(0)
    for u in range(N_TILES):
        if u + 1 < N_TILES:
            matmul(u + 1)
        epilogue(u)


def _inproj(x, nw, w, ws, taps, cbias, cscale, gmean):
    B, L, D = x.shape
    tm = min(INPROJ_ROWS, L)
    ni = L // tm
    hb = tm // HALO
    nlast = L // HALO - 1
    ncols = N_TILES * TILE_N
    const = lambda b, i: (0, 0)
    tok = lambda b, i: (b, i, 0)
    return pl.pallas_call(
        functools.partial(_inproj_kernel, tm=tm),
        grid=(B, ni),
        in_specs=[
            pl.BlockSpec((1, tm, D), tok),
            pl.BlockSpec((1, HALO, D), lambda b, i: (b, jnp.maximum(i * hb - 1, 0), 0)),
            pl.BlockSpec((1, HALO, D), lambda b, i: (b, jnp.minimum((i + 1) * hb, nlast), 0)),
            pl.BlockSpec((1, D), const),
            pl.BlockSpec((D, ncols), const, pipeline_mode=pl.Buffered(1)),
            pl.BlockSpec((D, SMALL_OUT), const),
            pl.BlockSpec((3, ncols), const),
            pl.BlockSpec((1, ncols), const),
            pl.BlockSpec((1, ncols), const),
            pl.BlockSpec((TILE_N, TILE_N), const),
        ],
        out_specs=[
            pl.BlockSpec((1, tm, N_MAIN_TILES * TILE_N), tok),
            pl.BlockSpec((1, tm, N_HY_TILES * TILE_N), tok),
            pl.BlockSpec((1, tm, SMALL_OUT), tok),
        ],
        out_shape=[
            jax.ShapeDtypeStruct((B, L, N_MAIN_TILES * TILE_N), jnp.bfloat16),
            jax.ShapeDtypeStruct((B, L, N_HY_TILES * TILE_N), jnp.bfloat16),
            jax.ShapeDtypeStruct((B, L, SMALL_OUT), jnp.float32),
        ],
        scratch_shapes=[
            pltpu.VMEM((tm + 2 * HALO, D), jnp.bfloat16),
            pltpu.VMEM((2, tm + 2 * HALO, TILE_N), jnp.float32),
        ],
        compiler_params=pltpu.CompilerParams(
            dimension_semantics=("arbitrary", "arbitrary"), vmem_limit_bytes=VMEM_LIMIT),
        name="inproj",
    )(x, x, x, nw, w, ws, taps, cbias, cscale, gmean)


CHUNK = 256
CHUNKS_PER_STEP = 4
STEP_TOKENS = CHUNKS_PER_STEP * CHUNK
NEG = -1e30


def _sub_chunk(ref, k):
    return ref.at[:, pl.ds(k * CHUNK, CHUNK), :]


def _dot_nt(a, b):
    return lax.dot_general(a, b, (((1,), (1,)), ((), ())), preferred_element_type=jnp.float32)


def _dot_tn(a, b):
    return lax.dot_general(a, b, (((0,), (0,)), ((), ())), preferred_element_type=jnp.float32)


def _chunk_mask(q, reverse):
    t = lax.broadcasted_iota(jnp.int32, (q, q), 0)
    s = lax.broadcasted_iota(jnp.int32, (q, q), 1)
    return (s >= t) if reverse else (s <= t)


MB_H = 8
MB_P = 64
MB_N = 128
MB_GW = 256
ML_H = 4
ML_DH = 128


LOG2E = math.log2(math.e)


def _split_cols(x):
    hi = x.astype(jnp.bfloat16)
    lo = (x - hi.astype(jnp.float32)).astype(jnp.bfloat16)
    return jnp.concatenate([hi, lo], axis=1)


def _split_rows(x):
    hi = x.astype(jnp.bfloat16)
    lo = (x - hi.astype(jnp.float32)).astype(jnp.bfloat16)
    return jnp.concatenate([hi, lo], axis=0)


def _tri2(mask):
    tri = mask.astype(jnp.bfloat16)
    return jnp.concatenate([tri, tri], axis=1)


def _ssd_direction(xs_ref, bc_ref, sm_ref, bias, a_row, ee, ee2, s_ref, o_ref, *, reverse, dcol):
    q = CHUNK
    mask = _chunk_mask(q, reverse)
    dt_s = jax.nn.softplus(sm_ref[0] + bias)
    a_s = dt_s * a_row
    c_s = jnp.dot(_tri2(mask), _split_rows(a_s), preferred_element_type=jnp.float32)
    c_t = c_s.T
    full = jnp.dot(_split_cols(jnp.concatenate([c_s, dt_s], axis=0)), ee, preferred_element_type=jnp.float32)
    c_full, dt_full = full[:q], full[q:]
    cc_all = jnp.dot(_split_cols(c_s), ee2, preferred_element_type=jnp.float32)
    far = 0 if reverse else q - 1
    tot_full = c_full[far:far + 1, :]
    x = xs_ref[0].astype(jnp.float32)
    dtx = dt_full * x
    lane = lax.broadcasted_iota(jnp.int32, (1, MB_GW), 1)
    ys = []
    yield
    for g in range(2):
        bg = bc_ref[0, :, g * MB_N:(g + 1) * MB_N]
        cg = bc_ref[0, :, MB_GW + g * MB_N:MB_GW + (g + 1) * MB_N]
        gs = slice(g * MB_GW, (g + 1) * MB_GW)
        gram = _dot_nt(cg, bg)
        dtx_g = dtx[:, gs]
        ms, xms = [], []
        for j in range(4):
            hd = g * 4 + j
            col = dcol + hd
            decay = jnp.exp2(jnp.where(mask, cc_all[:, hd * q:(hd + 1) * q] - c_t[col:col + 1, :], NEG))
            ms.append((gram * decay).astype(jnp.bfloat16))
            xms.append(jnp.where((lane >= j * MB_P) & (lane < (j + 1) * MB_P), dtx_g, 0.0).astype(jnp.bfloat16))
        yg = jnp.dot(jnp.concatenate(ms, axis=1), jnp.concatenate(xms, axis=0), preferred_element_type=jnp.float32)
        s_g = s_ref[:, gs]
        yg = yg + jnp.exp2(c_full[:, gs]) * jnp.dot(cg, s_g.astype(jnp.bfloat16), preferred_element_type=jnp.float32)
        w = jnp.exp2(tot_full[:, gs] - c_full[:, gs])
        s_ref[:, gs] = jnp.exp2(tot_full[:, gs]) * s_g + _dot_tn(bg, (w * dtx_g).astype(jnp.bfloat16))
        ys.append(yg)
        yield
    o_ref[0] = jnp.concatenate(ys, axis=-1).astype(o_ref.dtype)


def _interleave(*gens):
    live = list(gens)
    while live:
        for g in list(live):
            if next(g, StopIteration) is StopIteration:
                live.remove(g)


def _ssd2_kernel(xf_ref, bcf_ref, smf_ref, xb_ref, bcb_ref, smb_ref, bias_ref, a_ref, ee_ref, ee2_ref,
                 of_ref, ob_ref, s_ref):
    @pl.when(pl.program_id(1) == 0)
    def _():
        s_ref[...] = jnp.zeros_like(s_ref)

    for k in range(CHUNKS_PER_STEP):
        f = lambda r: _sub_chunk(r, k)
        g = lambda r: _sub_chunk(r, CHUNKS_PER_STEP - 1 - k)
        _interleave(
            _ssd_direction(f(xf_ref), f(bcf_ref), f(smf_ref), bias_ref[0:1], a_ref[0:1], ee_ref[0], ee2_ref[0],
                           s_ref.at[0], f(of_ref), reverse=False, dcol=0),
            _ssd_direction(g(xb_ref), g(bcb_ref), g(smb_ref), bias_ref[1:2], a_ref[1:2], ee_ref[1], ee2_ref[1],
                           s_ref.at[1], g(ob_ref), reverse=True, dcol=MB_H))


def _ssd2(proj, small, bias, a_row, ee, ee2):
    B, L, _ = proj.shape
    nc = L // STEP_TOKENS
    fw = lambda col: (lambda b, i: (b, i, col))
    bw = lambda col: (lambda b, i: (b, nc - 1 - i, col))
    blk = (1, STEP_TOKENS, GROUP_W)
    sblk = (1, STEP_TOKENS, SMALL_W)
    const2 = lambda b, i: (0, 0)
    const3 = lambda b, i: (0, 0, 0)
    out = jax.ShapeDtypeStruct((B, L, GROUP_W), jnp.bfloat16)
    return pl.pallas_call(
        _ssd2_kernel,
        grid=(B, nc),
        in_specs=[
            pl.BlockSpec(blk, fw(T_MB_X)), pl.BlockSpec(blk, fw(T_MB_BC)), pl.BlockSpec(sblk, fw(0)),
            pl.BlockSpec(blk, bw(T_MB_X)), pl.BlockSpec(blk, bw(T_MB_BC)), pl.BlockSpec(sblk, bw(0)),
            pl.BlockSpec(bias.shape, const2), pl.BlockSpec(a_row.shape, const2),
            pl.BlockSpec(ee.shape, const3), pl.BlockSpec(ee2.shape, const3),
        ],
        out_specs=[pl.BlockSpec(blk, fw(0)), pl.BlockSpec(blk, bw(0))],
        out_shape=[out, out],
        scratch_shapes=[pltpu.VMEM((2, MB_N, GROUP_W), jnp.float32)],
        compiler_params=pltpu.CompilerParams(
            dimension_semantics=("arbitrary", "arbitrary"), vmem_limit_bytes=VMEM_LIMIT),
        name="ssd",
    )(proj, proj, small, proj, proj, small, bias, a_row, ee, ee2)


ML_LANE = 16


def _cummax_rows(u, reverse):
    n = u.shape[0]
    row = lax.broadcasted_iota(jnp.int32, u.shape, 0)
    k = 1
    while k < n:
        if reverse:
            shifted = jnp.where(row < n - k, pltpu.roll(u, n - k, axis=0), NEG)
        else:
            shifted = jnp.where(row >= k, pltpu.roll(u, k, axis=0), NEG)
        u = jnp.maximum(u, shifted)
        k *= 2
    return u


def _mlstm_direction(q_ref, k_ref, v_ref, si_ref, sf_ref, bias_i, bias_f, ee, st_ref, m_ref, o_ref, *, reverse, lane0):
    q = CHUNK
    mask = _chunk_mask(q, reverse)
    gi = si_ref[0] + bias_i
    lf = jax.nn.log_sigmoid(sf_ref[0] + bias_f)
    b = jnp.dot(_tri2(mask), _split_rows(lf), preferred_element_type=jnp.float32)
    u = gi - b
    far = 0 if reverse else q - 1
    m_prev = m_ref[...]
    m_inter = b + m_prev
    m_t = jnp.maximum(m_inter, b + _cummax_rows(u, reverse))
    w_inter = jnp.exp(m_inter - m_t)
    em = jnp.exp(-m_t)
    b_tot = b[far:far + 1, :]
    g = b_tot - b + gi
    m_new = jnp.maximum(b_tot + m_prev, jnp.max(g, axis=0, keepdims=True))
    wk = jnp.exp(g - m_new) * (ML_DH ** -0.5)
    decay = jnp.exp(b_tot + m_prev - m_new)
    m_ref[...] = m_new
    arow = (b - m_t) * LOG2E + math.log2(ML_DH ** -0.5)
    ee_dh, ee_q = ee[:, :GROUP_W], ee[:, GROUP_W:]
    stack = jnp.concatenate([w_inter, em, wk], axis=0)
    full = jnp.dot(_split_cols(stack), ee_dh, preferred_element_type=jnp.float32)
    w_full, em_full, wk_full = full[:q], full[q:2 * q], full[2 * q:]
    a_all = jnp.dot(_split_cols(arow), ee_q, preferred_element_type=jnp.float32)
    dec_full = jnp.dot(_split_cols(jnp.broadcast_to(decay, (8, SMALL_W))), ee_dh,
                       preferred_element_type=jnp.float32)[0:1]
    u_t = (u * LOG2E).T
    ones = jnp.ones((q, ML_DH), jnp.bfloat16)
    hs = []
    yield
    for h in range(ML_H):
        hsl = slice(h * ML_DH, (h + 1) * ML_DH)
        p = jnp.exp2(jnp.where(mask, a_all[:, h * q:(h + 1) * q] + u_t[lane0 + h:lane0 + h + 1, :], NEG))
        qh = q_ref[0, :, hsl]
        kh = k_ref[0, :, hsl]
        v1 = jnp.concatenate([v_ref[0, :, hsl], ones], axis=1)
        s = (_dot_nt(qh, kh) * p).astype(jnp.bfloat16)
        st = st_ref[h]
        inter = jnp.dot(qh, st.astype(jnp.bfloat16), preferred_element_type=jnp.float32)
        both = w_full[:, hsl] * inter[:, :ML_DH], w_full[:, hsl] * inter[:, ML_DH:]
        intra = jnp.dot(s, v1, preferred_element_type=jnp.float32)
        num = both[0] + intra[:, :ML_DH]
        den = both[1] + intra[:, ML_DH:]
        hs.append(num / jnp.maximum(jnp.abs(den), em_full[:, hsl]))
        kw = (kh.astype(jnp.float32) * wk_full[:, hsl]).astype(jnp.bfloat16)
        dh = dec_full[:, hsl]
        st_ref[h] = jnp.concatenate([dh, dh], axis=1) * st + _dot_tn(kw, v1)
        yield
    o_ref[0] = jnp.concatenate(hs, axis=-1).astype(o_ref.dtype)


def _mlstm2_kernel(qf_ref, kf_ref, vf_ref, sif_ref, sff_ref, qb_ref, kb_ref, vb_ref, sib_ref, sfb_ref,
                   bi_ref, bf_ref, ee_ref, of_ref, ob_ref, st_ref, m_ref):
    @pl.when(pl.program_id(1) == 0)
    def _():
        st_ref[...] = jnp.zeros_like(st_ref)
        m_ref[...] = jnp.zeros_like(m_ref)

    for k in range(CHUNKS_PER_STEP):
        f = lambda r: _sub_chunk(r, k)
        g = lambda r: _sub_chunk(r, CHUNKS_PER_STEP - 1 - k)
        _interleave(
            _mlstm_direction(f(qf_ref), f(kf_ref), f(vf_ref), f(sif_ref), f(sff_ref), bi_ref[...], bf_ref[...],
                             ee_ref[0], st_ref.at[0], m_ref.at[0], f(of_ref), reverse=False, lane0=ML_LANE),
            _mlstm_direction(g(qb_ref), g(kb_ref), g(vb_ref), g(sib_ref), g(sfb_ref), bi_ref[...], bf_ref[...],
                             ee_ref[1], st_ref.at[1], m_ref.at[1], g(ob_ref), reverse=True, lane0=ML_LANE + ML_H))


def _mlstm2(proj, small, bias_i, bias_f, ee):
    B, L, _ = proj.shape
    nc = L // STEP_TOKENS
    fw = lambda col: (lambda b, i: (b, i, col))
    bw = lambda col: (lambda b, i: (b, nc - 1 - i, col))
    blk = (1, STEP_TOKENS, GROUP_W)
    sblk = (1, STEP_TOKENS, SMALL_W)
    const2 = lambda b, i: (0, 0)
    out = jax.ShapeDtypeStruct((B, L, GROUP_W), jnp.bfloat16)
    specs = lambda m: [pl.BlockSpec(blk, m(T_ML_Q)), pl.BlockSpec(blk, m(T_ML_K)), pl.BlockSpec(blk, m(T_ML_V)),
                       pl.BlockSpec(sblk, m(0)), pl.BlockSpec(sblk, m(1))]
    return pl.pallas_call(
        _mlstm2_kernel,
        grid=(B, nc),
        in_specs=specs(fw) + specs(bw) + [
            pl.BlockSpec(bias_i.shape, const2), pl.BlockSpec(bias_f.shape, const2),
            pl.BlockSpec(ee.shape, lambda b, i: (0, 0, 0)),
        ],
        out_specs=[pl.BlockSpec(blk, fw(0)), pl.BlockSpec(blk, bw(0))],
        out_shape=[out, out],
        scratch_shapes=[
            pltpu.VMEM((2, ML_H, ML_DH, 2 * ML_DH), jnp.float32),
            pltpu.VMEM((2, 1, SMALL_W), jnp.float32),
        ],
        compiler_params=pltpu.CompilerParams(
            dimension_semantics=("arbitrary", "arbitrary"), vmem_limit_bytes=VMEM_LIMIT),
        name="mlstm",
    )(proj, proj, proj, small, small, proj, proj, proj, small, small, bias_i, bias_f, ee)


def _group_rmsnorm(y, width):
    outs = []
    for g in range(y.shape[-1] // width):
        yg = y[:, g * width:(g + 1) * width]
        ms = jnp.mean(yg * yg, axis=-1, keepdims=True)
        outs.append(yg * lax.rsqrt(ms + RMS_EPS))
    return jnp.concatenate(outs, axis=-1)


def _outproj2_kernel(x_ref, yh_ref, hg_ref, mf_ref, mb_ref, mx_ref, mz_ref, lf_ref, lb_ref, lo_ref, lz_ref, yn_ref,
                     dsk_ref, mnw_ref, lnw_ref, w_ref, o_ref):
    f32 = jnp.float32
    up = lambda r: r[0].astype(f32)
    yh = (up(yh_ref) * _silu(up(hg_ref))).astype(jnp.bfloat16)
    acc = jnp.dot(yh, w_ref[0], preferred_element_type=f32)
    ym = (up(mf_ref) + up(mb_ref) + up(mx_ref) * dsk_ref[...]) * _silu(up(mz_ref))
    ym = (_group_rmsnorm(ym, MB_GW) * mnw_ref[...]).astype(jnp.bfloat16)
    acc += jnp.dot(ym, w_ref[1], preferred_element_type=f32)
    yl = (up(lf_ref) + up(lb_ref)) * jax.nn.sigmoid(up(lo_ref))
    yl = (_group_rmsnorm(yl, ML_DH) * lnw_ref[...] * _silu(up(lz_ref))).astype(jnp.bfloat16)
    acc += jnp.dot(yl, w_ref[2], preferred_element_type=f32)
    acc += jnp.dot(yn_ref[0], w_ref[3], preferred_element_type=f32)
    o_ref[0] = x_ref[0] + acc


def _outproj2(x, y_hy, proj, y_mf, y_mb, h_f, h_b, y_na, dskip, mb_nw, ml_nw, w_out):
    B, L, D = x.shape
    tm = min(512, L)
    tok = lambda b, i: (b, i, 0)
    col = lambda c: (lambda b, i: (b, i, c))
    blk = (1, tm, GROUP_W)
    vec = pl.BlockSpec((1, GROUP_W), lambda b, i: (0, 0))
    return pl.pallas_call(
        _outproj2_kernel,
        grid=(B, L // tm),
        in_specs=[
            pl.BlockSpec((1, tm, D), tok),
            pl.BlockSpec(blk, tok), pl.BlockSpec(blk, col(T_HY_G)),
            pl.BlockSpec(blk, tok), pl.BlockSpec(blk, tok), pl.BlockSpec(blk, col(T_MB_X)), pl.BlockSpec(blk, col(T_MB_Z)),
            pl.BlockSpec(blk, tok), pl.BlockSpec(blk, tok), pl.BlockSpec(blk, col(T_ML_O)), pl.BlockSpec(blk, col(T_ML_Z)),
            pl.BlockSpec(blk, tok),
            vec, vec, vec,
            pl.BlockSpec((4, GROUP_W, D), lambda b, i: (0, 0, 0)),
        ],
        out_specs=pl.BlockSpec((1, tm, D), tok),
        out_shape=jax.ShapeDtypeStruct((B, L, D), jnp.float32),
        compiler_params=pltpu.CompilerParams(
            dimension_semantics=("arbitrary", "arbitrary"), vmem_limit_bytes=VMEM_LIMIT),
        name="outproj",
    )(x, y_hy, proj, y_mf, y_mb, proj, proj, h_f, h_b, proj, proj, y_na, dskip, mb_nw, ml_nw, w_out)


GRID_W = 64
NA_KR = 8
NA_KC = 16
NA_H = 8
NA_DH = 64
ROWS_PER_BLOCK = 16
BLOCK_TOK = ROWS_PER_BLOCK * GRID_W


def _na_kernel(q_ref, kp_ref, kc_ref, kn_ref, vp_ref, vc_ref, vn_ref, g_ref, tbl_ref, o_ref, k_s, v_s, *, rows):
    mblk = pl.program_id(1)
    npair = NA_H // 2
    pw = 2 * NA_DH
    ones = jnp.ones((BLOCK_TOK, pw), jnp.bfloat16)
    for t, (kr, vr) in enumerate(((kp_ref, vp_ref), (kc_ref, vc_ref), (kn_ref, vn_ref))):
        k_s[pl.ds(t * BLOCK_TOK, BLOCK_TOK), :] = kr[0]
        for hp in range(npair):
            v_s[pl.ds(t * BLOCK_TOK, BLOCK_TOK), 2 * hp * pw:(2 * hp + 1) * pw] = vr[0, :, hp * pw:(hp + 1) * pw]
            v_s[pl.ds(t * BLOCK_TOK, BLOCK_TOK), (2 * hp + 1) * pw:(2 * hp + 2) * pw] = ones
    lane = lax.broadcasted_iota(jnp.int32, (GRID_W, pw), 1)
    first = lane < NA_DH
    win = NA_KR * GRID_W
    for j in range(ROWS_PER_BLOCK):
        r = mblk * ROWS_PER_BLOCK + j
        rs = jnp.clip(r - NA_KR // 2, 0, rows - NA_KR)
        didx = r - rs
        off = pl.multiple_of((rs - (mblk - 1) * ROWS_PER_BLOCK) * GRID_W, GRID_W)
        ss = []
        for hp in range(npair):
            ls = slice(hp * pw, (hp + 1) * pw)
            qp = q_ref[0, j * GRID_W:(j + 1) * GRID_W, ls]
            zero = jnp.zeros_like(qp)
            q2 = jnp.concatenate([jnp.where(first, qp, zero), jnp.where(first, zero, qp)], axis=0)
            ss.append(_dot_nt(q2, k_s[pl.ds(off, win), ls]))
        s = jnp.concatenate(ss, axis=0) + tbl_ref[didx].astype(jnp.float32)
        e = jnp.exp2(s - jnp.max(s, axis=-1, keepdims=True)).astype(jnp.bfloat16)
        outs = []
        for hp in range(npair):
            ov = jnp.dot(e[hp * pw:(hp + 1) * pw], v_s[pl.ds(off, win), 2 * hp * pw:(2 * hp + 2) * pw],
                         preferred_element_type=jnp.float32)
            o2 = ov[:, :pw] / ov[:, pw:]
            outs.append(jnp.where(first, o2[:GRID_W], o2[GRID_W:]))
        o = jnp.concatenate(outs, axis=-1)
        gate = _silu(g_ref[0, j * GRID_W:(j + 1) * GRID_W, :].astype(jnp.float32))
        o_ref[0, j * GRID_W:(j + 1) * GRID_W, :] = (o * gate).astype(o_ref.dtype)


def _na(proj, tbl):
    B, L, _ = proj.shape
    rows = L // GRID_W
    nb = rows // ROWS_PER_BLOCK
    prev = lambda col: (lambda b, m: (b, jnp.maximum(m - 1, 0), col))
    cur = lambda col: (lambda b, m: (b, m, col))
    nxt = lambda col: (lambda b, m: (b, jnp.minimum(m + 1, nb - 1), col))
    blk = (1, BLOCK_TOK, GROUP_W)
    return pl.pallas_call(
        functools.partial(_na_kernel, rows=rows),
        grid=(B, nb),
        in_specs=[
            pl.BlockSpec(blk, cur(T_NA_Q)),
            pl.BlockSpec(blk, prev(T_NA_K)), pl.BlockSpec(blk, cur(T_NA_K)), pl.BlockSpec(blk, nxt(T_NA_K)),
            pl.BlockSpec(blk, prev(T_NA_V)), pl.BlockSpec(blk, cur(T_NA_V)), pl.BlockSpec(blk, nxt(T_NA_V)),
            pl.BlockSpec(blk, cur(T_NA_G)),
            pl.BlockSpec(tbl.shape, lambda b, m: (0, 0, 0)),
        ],
        out_specs=pl.BlockSpec(blk, cur(0)),
        out_shape=jax.ShapeDtypeStruct((B, L, GROUP_W), jnp.bfloat16),
        scratch_shapes=[
            pltpu.VMEM((3 * BLOCK_TOK, GROUP_W), jnp.bfloat16),
            pltpu.VMEM((3 * BLOCK_TOK, 2 * GROUP_W), jnp.bfloat16),
        ],
        compiler_params=pltpu.CompilerParams(
            dimension_semantics=("arbitrary", "arbitrary"), vmem_limit_bytes=VMEM_LIMIT),
        name="nbr_attn",
    )(proj, proj, proj, proj, proj, proj, proj, proj, tbl)


def _na_bias_table(rpb):
    nco = 2 * NA_KC - 1
    rows_d = jnp.stack([rpb[:, NA_KR - 1 - d:2 * NA_KR - 1 - d, :] for d in range(NA_KR)])
    lpad = GRID_W - NA_KC
    ext = jnp.pad(rows_d * math.log2(math.e), ((0, 0), (0, 0), (0, 0), (lpad, 2 * GRID_W - lpad - nco)))
    lead = ext.shape[:3]
    skew = jnp.broadcast_to(ext[..., None, :], lead + (GRID_W, 2 * GRID_W)).reshape(lead + (2 * GRID_W * GRID_W,))
    skew = skew[..., :GRID_W * (2 * GRID_W - 1)].reshape(lead + (GRID_W, 2 * GRID_W - 1))[..., GRID_W - 1:]
    w = jnp.arange(GRID_W)
    c = jnp.arange(GRID_W)
    cstart = jnp.clip(w - NA_KC // 2, 0, GRID_W - NA_KC)
    inwin = (c[None, :] >= cstart[:, None]) & (c[None, :] < cstart[:, None] + NA_KC)
    t = jnp.where(inwin, skew, NEG)
    t = t.transpose(0, 1, 3, 2, 4).reshape(NA_KR, NA_H * GRID_W, NA_KR * GRID_W)
    return t.astype(jnp.bfloat16)


HY_LANES = 256
TBL_LANES = 128
F2N = 128
TW_B = 16
PAD_ROWS = 8
HY_POS_PAD = 128


def _cmul(ar, ai, br, bi):
    return ar * br - ai * bi, ar * bi + ai * br


def _pack_c(re, im):
    half = jnp.uint32(0x8000)
    rb = lax.bitcast_convert_type(re, jnp.uint32) + half
    ib = lax.bitcast_convert_type(im, jnp.uint32) + half
    return (rb & jnp.uint32(0xFFFF0000)) | (ib >> 16)


def _unpack_c(w):
    re = lax.bitcast_convert_type(w & jnp.uint32(0xFFFF0000), jnp.float32)
    im = lax.bitcast_convert_type(w << 16, jnp.float32)
    return re, im


def _tile_lanes(t):
    return jnp.concatenate([t] * (HY_LANES // TBL_LANES), axis=1)


LANE_PARTS = HY_LANES // TBL_LANES


def _rows_load(refs_or_ref, idx):
    parts = [refs_or_ref[p, idx, :] for p in range(LANE_PARTS)] if not isinstance(refs_or_ref, (list, tuple)) \
        else [r[idx, :] for r in refs_or_ref]
    return jnp.concatenate(parts, axis=1)


def _rows_store(buf_ref, idx, val):
    for p in range(LANE_PARTS):
        buf_ref[p, idx, :] = val[:, p * TBL_LANES:(p + 1) * TBL_LANES]


def _stage_a_store(buf_ref, a, n2, tr, ti, n1f, rs):
    ar, ai = _cmul(a[:n1f], a[n1f:], _tile_lanes(tr), _tile_lanes(ti))
    _rows_store(buf_ref, pl.ds(pl.multiple_of(n2 * rs, 8), n1f), _pack_c(ar, ai))


def _stage_b_load(buf_ref, k1, rs):
    yr, yi = _unpack_c(_rows_load(buf_ref, pl.ds(k1, F2N, stride=rs)))
    return jnp.concatenate([yr, yi], axis=0).astype(jnp.bfloat16)


def _hy_spec_kernel(*refs, n1f, h1, rs, ns):
    ff_refs, fb_refs = refs[:LANE_PARTS], refs[LANE_PARTS:2 * LANE_PARTS]
    mg_ref, tbr_ref, tbi_ref, f2_ref, o_ref, buf_ref = refs[2 * LANE_PARTS:]
    s = pl.program_id(2)

    @pl.when(s < ns)
    def _():
        for b in range(TW_B):
            n2 = s * TW_B + b
            hf = _rows_load(ff_refs, pl.ds(n2, h1, stride=F2N))
            hb = _rows_load(fb_refs, pl.ds(F2N - n2, h1, stride=F2N))
            xs = jnp.concatenate([hf, hb], axis=0).astype(jnp.bfloat16)
            a = jnp.dot(mg_ref[0], xs, preferred_element_type=jnp.float32)
            _stage_a_store(buf_ref, a, n2, tbr_ref[b], tbi_ref[b], n1f, rs)

    @pl.when(s >= ns)
    def _():
        for kk in range(TW_B):
            k1 = (s - ns) * TW_B + kk
            o_ref[0, kk] = jnp.dot(f2_ref[0], _stage_b_load(buf_ref, k1, rs), preferred_element_type=jnp.float32)


def _hy_spectrum(filt, mg, tbr, tbi, f2, L):
    n1f = 2 * L // F2N
    h1 = n1f // 2
    rs = n1f + PAD_ROWS
    ns = F2N // TW_B
    nk = n1f // TW_B
    ncb = GROUP_W // HY_LANES
    a_idx = lambda o, c, s: (jnp.minimum(s, ns - 1), 0, 0)
    once = pl.Buffered(1)
    nparts = GROUP_W // TBL_LANES

    def filt_spec(direction, part):
        return pl.BlockSpec(
            (L + F2N, TBL_LANES),
            lambda o, c, s: (0, (o * 2 + direction) * nparts + c * LANE_PARTS + part), pipeline_mode=once)

    return pl.pallas_call(
        functools.partial(_hy_spec_kernel, n1f=n1f, h1=h1, rs=rs, ns=ns),
        grid=(2, ncb, ns + nk),
        in_specs=[filt_spec(0, p) for p in range(LANE_PARTS)] + [filt_spec(1, p) for p in range(LANE_PARTS)] + [
            pl.BlockSpec((1, 2 * n1f, n1f), a_idx),
            pl.BlockSpec(tbr.shape, lambda o, c, s: (0, 0, 0)),
            pl.BlockSpec(tbi.shape, lambda o, c, s: (0, 0, 0)),
            pl.BlockSpec((1, 2 * F2N, 2 * F2N), lambda o, c, s: (0, 0, 0)),
        ],
        out_specs=pl.BlockSpec((1, TW_B, 2 * F2N, HY_LANES), lambda o, c, s: (o, jnp.maximum(s - ns, 0), 0, c)),
        out_shape=jax.ShapeDtypeStruct((2, n1f, 2 * F2N, GROUP_W), jnp.float32),
        scratch_shapes=[pltpu.VMEM((LANE_PARTS, F2N * rs, TBL_LANES), jnp.uint32)],
        compiler_params=pltpu.CompilerParams(
            dimension_semantics=("arbitrary", "arbitrary", "arbitrary"), vmem_limit_bytes=VMEM_LIMIT),
        name="hyena_spectrum",
    )(*([filt] * (2 * LANE_PARTS)), mg, tbr, tbi, f2)


def _hy_conv_kernel(v_ref, x1_ref, x2_ref, g_ref, m1_ref, m3_ref, tbr_ref, tbi_ref, f2_ref, skip_ref,
                    o_ref, buf_ref, z_ref, *, n1f, h1, rs, ns, nk):
    s = pl.program_id(2)
    p1 = ns
    p2 = p1 + nk
    p3 = p2 + ns
    p4 = p3 + nk

    def stage_a(xs, n2, b):
        a = jnp.dot(m1_ref[0], xs, preferred_element_type=jnp.float32)
        _stage_a_store(buf_ref, a, n2, tbr_ref[b], tbi_ref[b], n1f, rs)

    def stage_b(order, kb):
        for kk in range(TW_B):
            k1 = kb * TW_B + kk
            y = jnp.dot(f2_ref[0], _stage_b_load(buf_ref, k1, rs), preferred_element_type=jnp.float32)
            g = g_ref[0, kk]
            yr, yi = _cmul(y[:F2N], y[F2N:], g[:F2N], g[F2N:])
            ys = jnp.concatenate([yr, yi], axis=0).astype(jnp.bfloat16)
            z = jnp.dot(f2_ref[1], ys, preferred_element_type=jnp.float32)
            _rows_store(buf_ref, pl.ds(k1, F2N, stride=rs), _pack_c(z[:F2N], z[F2N:]))

    def stage_c(n2, b):
        zr, zi = _unpack_c(_rows_load(buf_ref, pl.ds(pl.multiple_of(n2 * rs, 8), n1f)))
        wr, wi = _cmul(zr, zi, _tile_lanes(tbr_ref[b]), -_tile_lanes(tbi_ref[b]))
        ws = jnp.concatenate([wr, wi], axis=0).astype(jnp.bfloat16)
        return jnp.dot(m3_ref[0], ws, preferred_element_type=jnp.float32)

    @pl.when(s < p1)
    def _():
        for b in range(TW_B):
            xs = jnp.concatenate([v_ref[0, b], v_ref[1, b]], axis=0)
            stage_a(xs, s * TW_B + b, b)

    @pl.when((s >= p1) & (s < p2))
    def _():
        stage_b(0, s - p1)

    @pl.when((s >= p2) & (s < p3))
    def _():
        for b in range(TW_B):
            n2 = (s - p2) * TW_B + b
            c = stage_c(n2, b)
            zs = []
            for r in range(2):
                vv = v_ref[r, b].astype(jnp.float32)
                z = x1_ref[r, b].astype(jnp.float32) * (c[r * h1:(r + 1) * h1] + skip_ref[0:1, :] * vv)
                zb = z.astype(jnp.bfloat16)
                z_ref[n2, r] = zb
                zs.append(zb)
            stage_a(jnp.concatenate(zs, axis=0), n2, b)

    @pl.when((s >= p3) & (s < p4))
    def _():
        stage_b(1, s - p3)

    @pl.when(s >= p4)
    def _():
        for b in range(TW_B):
            n2 = (s - p4) * TW_B + b
            c = stage_c(n2, b)
            for r in range(2):
                zz = z_ref[n2, r].astype(jnp.float32)
                y = x2_ref[r, b].astype(jnp.float32) * (c[r * h1:(r + 1) * h1] + skip_ref[1:2, :] * zz)
                o_ref[r, b] = y.astype(o_ref.dtype)


def _hy_conv(hyp, spec, m1, m3, tbr, tbi, f2, skip, L):
    B = hyp.shape[0]
    n1f = 2 * L // F2N
    h1 = n1f // 2
    rs = n1f + PAD_ROWS
    ns = F2N // TW_B
    nk = n1f // TW_B
    ncb = GROUP_W // HY_LANES
    p1, p2, p3, p4 = ns, ns + nk, 2 * ns + nk, 2 * ns + 2 * nk
    clip = lambda v, hi: jnp.clip(v, 0, hi)
    tblk = (2, TW_B, h1, HY_LANES)
    v_idx = lambda c, p, s: (p, jnp.where(s < p1, s, clip(s - p2, ns - 1)), 0, c)
    x1_idx = lambda c, p, s: (p, clip(s - p2, ns - 1), 0, ncb + c)
    x2_idx = lambda c, p, s: (p, clip(s - p4, ns - 1), 0, 2 * ncb + c)
    g_idx = lambda c, p, s: (jnp.where(s < p3, 0, 1), jnp.where(s < p3, clip(s - p1, nk - 1), clip(s - p3, nk - 1)), 0, c)
    m1_idx = lambda c, p, s: (jnp.where(s < p1, s, clip(s - p2, ns - 1)), 0, 0)
    m3_idx = lambda c, p, s: (jnp.where(s < p4, clip(s - p2, ns - 1), s - p4), 0, 0)
    return pl.pallas_call(
        functools.partial(_hy_conv_kernel, n1f=n1f, h1=h1, rs=rs, ns=ns, nk=nk),
        grid=(ncb, B // 2, p4 + ns),
        in_specs=[
            pl.BlockSpec(tblk, v_idx),
            pl.BlockSpec(tblk, x1_idx),
            pl.BlockSpec(tblk, x2_idx),
            pl.BlockSpec((1, TW_B, 2 * F2N, HY_LANES), g_idx),
            pl.BlockSpec((1, 2 * n1f, 2 * h1), m1_idx),
            pl.BlockSpec((1, 2 * h1, 2 * n1f), m3_idx),
            pl.BlockSpec(tbr.shape, lambda c, p, s: (0, 0, 0)),
            pl.BlockSpec(tbi.shape, lambda c, p, s: (0, 0, 0)),
            pl.BlockSpec((2, 2 * F2N, 2 * F2N), lambda c, p, s: (0, 0, 0)),
            pl.BlockSpec((2, HY_LANES), lambda c, p, s: (0, c)),
        ],
        out_specs=pl.BlockSpec(tblk, lambda c, p, s: (p, clip(s - p4, ns - 1), 0, c)),
        out_shape=jax.ShapeDtypeStruct((B, F2N, h1, GROUP_W), jnp.bfloat16),
        scratch_shapes=[
            pltpu.VMEM((LANE_PARTS, F2N * rs, TBL_LANES), jnp.uint32),
            pltpu.VMEM((F2N, 2, h1, HY_LANES), jnp.bfloat16),
        ],
        compiler_params=pltpu.CompilerParams(
            dimension_semantics=("arbitrary", "arbitrary", "arbitrary"), vmem_limit_bytes=VMEM_LIMIT),
        name="hyena_conv",
    )(hyp, hyp, hyp, spec, m1, m3, tbr, tbi, f2, skip)


def _hy_filter_kernel(pos_ref, w1_ref, b1_ref, w2_ref, b2_ref, w3_ref, fr_ref, dec_ref, o_ref, *, nblk):
    i = pl.program_id(0)
    pos = pos_ref[...]
    fr = fr_ref[...]
    hid = jnp.sin(fr * (jnp.dot(pos, w1_ref[...], precision=_HI, preferred_element_type=jnp.float32) + b1_ref[...]))
    hid = jnp.sin(fr * (jnp.dot(hid, w2_ref[...], precision=_HI, preferred_element_type=jnp.float32) + b2_ref[...]))
    hi = hid.astype(jnp.bfloat16)
    lo = (hid - hi.astype(jnp.float32)).astype(jnp.bfloat16)
    filt = jnp.dot(jnp.concatenate([hi, lo, hi], axis=1), w3_ref[...], preferred_element_type=jnp.float32)
    filt = filt * jnp.exp(-pos[:, 0:1] * dec_ref[...])
    o_ref[...] = jnp.where(i < nblk, filt, 0.0)


def _hy_filters(pos, w1, b1, w2, b2, w3, freq, decay, L):
    nblk = L // F2N
    nh = w2.shape[0]
    no = w3.shape[1]
    const = lambda i: (0, 0)
    return pl.pallas_call(
        functools.partial(_hy_filter_kernel, nblk=nblk),
        grid=(nblk + 1,),
        in_specs=[
            pl.BlockSpec((F2N, HY_POS_PAD), lambda i: (jnp.minimum(i, nblk - 1), 0)),
            pl.BlockSpec((HY_POS_PAD, nh), const), pl.BlockSpec((1, nh), const),
            pl.BlockSpec((nh, nh), const), pl.BlockSpec((1, nh), const),
            pl.BlockSpec((3 * nh, no), const), pl.BlockSpec((1, nh), const), pl.BlockSpec((1, no), const),
        ],
        out_specs=pl.BlockSpec((F2N, no), lambda i: (i, 0)),
        out_shape=jax.ShapeDtypeStruct((L + F2N, no), jnp.float32),
        compiler_params=pltpu.CompilerParams(dimension_semantics=("arbitrary",), vmem_limit_bytes=VMEM_LIMIT),
        name="hyena_filters",
    )(pos, w1, b1, w2, b2, w3, freq, decay)


def _hy_constants(L):
    n = 2 * L
    n1f = n // F2N
    h1 = n1f // 2
    na = F2N // TW_B

    def cis(num, den):
        ang = (-2.0 * math.pi / den) * (num % den).astype(jnp.float32)
        return jnp.cos(ang), jnp.sin(ang)

    k1 = jnp.arange(n1f)
    n1 = jnp.arange(n1f)
    a = jnp.arange(na)
    f1r, f1i = cis(k1[:, None] * n1[None, :], n1f)
    tar, tai = cis(k1[None, :] * (TW_B * a)[:, None], n)
    mr, mi = _cmul(f1r[None], f1i[None], tar[:, :, None], tai[:, :, None])

    def blockform(r, i):
        return jnp.concatenate([jnp.concatenate([r, -i], axis=-1), jnp.concatenate([i, r], axis=-1)], axis=-2)

    m1 = blockform(mr[:, :, :h1], mi[:, :, :h1]).astype(jnp.bfloat16)
    m3r = jnp.swapaxes(mr[:, :, :h1], 1, 2) / n
    m3i = -jnp.swapaxes(mi[:, :, :h1], 1, 2) / n
    m3 = blockform(m3r, m3i).astype(jnp.bfloat16)
    mgr = jnp.concatenate([mr[:, :, :h1], mr[:, :, h1:][:, :, ::-1]], axis=-1)
    mgi = jnp.concatenate([mi[:, :, :h1], mi[:, :, h1:][:, :, ::-1]], axis=-1)
    mg = jnp.concatenate([mgr, mgi], axis=1).astype(jnp.bfloat16)
    b = jnp.arange(TW_B)
    tbr, tbi = cis(k1[None, :] * b[:, None], n)
    tbr = jnp.broadcast_to(tbr[:, :, None], (TW_B, n1f, TBL_LANES))
    tbi = jnp.broadcast_to(tbi[:, :, None], (TW_B, n1f, TBL_LANES))
    k2 = jnp.arange(F2N)
    f2r, f2i = cis(k2[:, None] * k2[None, :], F2N)
    f2 = jnp.stack([blockform(f2r, f2i), blockform(f2r, -f2i)]).astype(jnp.bfloat16)
    return m1, m3, mg, tbr, tbi, f2


def _hy_positions(L):
    t = jnp.arange(L, dtype=jnp.float32)
    bands = jnp.arange(1, 9, dtype=jnp.float32)
    ang = (2.0 * math.pi / L) * t[:, None] * bands[None, :]
    pos = jnp.concatenate([(t / L)[:, None], jnp.cos(ang), jnp.sin(ang)], axis=-1)
    return jnp.pad(pos, ((0, 0), (0, HY_POS_PAD - pos.shape[1])))


MAIN_COL_RANGES = ((0, 1536), (2048, 3072), (3600, 4624), (1536, 2048), (3072, 3584), (4624, 6160), (6176, 8224))
SMALL_COL_RANGES = ((3584, 3600), (6160, 6176))


def _take_cols(a, ranges):
    return jnp.concatenate([a[..., lo:hi] for lo, hi in ranges], axis=-1)


def _pad_cols(a, width, left=0):
    return jnp.pad(a, [(0, 0)] * (a.ndim - 1) + [(left, width - left - a.shape[-1])])


def kernel(x, norm_w, w_in, w_out, hy_conv_w, hy_conv_b, hy_w1, hy_b1, hy_w2, hy_b2, hy_w3, hy_freq, hy_decay,
           hy_skip, mb_conv_w, mb_conv_b, mb_dt_bias, mb_a_log, mb_d, mb_norm_w, ml_conv_w, ml_conv_b, ml_gate_b,
           ml_norm_w, na_qnorm_w, na_knorm_w, na_rpb):
    B, L, D = x.shape
    depth = w_in.shape[0]
    ncols = N_TILES * TILE_N
    f32 = jnp.float32
    pos = _hy_positions(L)
    m1, m3, mg, tbr, tbi, f2 = _hy_constants(L)
    gmean = jnp.kron(jnp.eye(NA_H, dtype=f32), jnp.full((NA_DH, NA_DH), 1.0 / NA_DH)).astype(jnp.bfloat16)

    def expansion(first_lane, heads, width):
        tgt = jnp.arange(heads * width) // width
        e = [(jnp.arange(SMALL_W)[:, None] == (first_lane + d * heads + tgt)[None, :]) for d in range(2)]
        e = jnp.stack(e).astype(jnp.bfloat16)
        return jnp.concatenate([e, e], axis=1)

    ee_mb = expansion(0, MB_H, MB_P)
    ee2_mb = expansion(0, MB_H, CHUNK)
    ee_ml = jnp.concatenate([expansion(ML_LANE, ML_H, ML_DH), expansion(ML_LANE, ML_H, CHUNK)], axis=2)
    for l in range(depth):
        w = _take_cols(w_in[l], MAIN_COL_RANGES).astype(jnp.bfloat16)
        wl = w_in[l]
        zeros = lambda n: jnp.zeros((D, n), wl.dtype)
        ws = jnp.concatenate([wl[:, 3584:3600], wl[:, 6160:6164], wl[:, 6168:6172], zeros(SMALL_W - 24),
                              zeros(ML_LANE), wl[:, 6164:6168], wl[:, 6172:6176], zeros(SMALL_W - 24)],
                             axis=1).astype(jnp.bfloat16)
        taps = _pad_cols(jnp.concatenate([hy_conv_w[l], mb_conv_w[l], ml_conv_w[l]], axis=-1), ncols)
        cbias = _pad_cols(jnp.concatenate([hy_conv_b[l], mb_conv_b[l], ml_conv_b[l]])[None], ncols)
        qk_scale = jnp.concatenate([jnp.tile(na_qnorm_w[l], NA_H) * (NA_DH ** -0.5 * math.log2(math.e)),
                                    jnp.tile(na_knorm_w[l], NA_H)])[None]
        cscale = _pad_cols(qk_scale, ncols, left=(T_NA_Q + N_HY_TILES) * TILE_N)
        proj, hy, small = _inproj(x, norm_w[l][None], w, ws, taps, cbias, cscale, gmean)

        w3_hi = hy_w3[l].astype(jnp.bfloat16)
        w3_lo = (hy_w3[l] - w3_hi.astype(f32)).astype(jnp.bfloat16)
        filt = _hy_filters(pos, _pad_cols(hy_w1[l].T, HY_POS_PAD).T, hy_b1[l][None], hy_w2[l], hy_b2[l][None],
                           jnp.concatenate([w3_hi, w3_hi, w3_lo]), hy_freq[l][None], hy_decay[l][None], L)
        spec = _hy_spectrum(filt, mg, tbr, tbi, f2[:1], L)
        hyp = hy.reshape(B, L // F2N, F2N, N_HY_TILES * GROUP_W).transpose(0, 2, 1, 3)
        y_hy = _hy_conv(hyp, spec, m1, m3, tbr, tbi, f2, hy_skip[l], L)
        y_hy = y_hy.transpose(0, 2, 1, 3).reshape(B, L, GROUP_W)

        a_log2 = -jnp.exp(mb_a_log[l].astype(f32)) * LOG2E
        dt_bias = jnp.concatenate([_pad_cols(mb_dt_bias[l, d][None], SMALL_W, left=d * MB_H) for d in range(2)])
        a_rows = jnp.concatenate([_pad_cols(a_log2[d][None], SMALL_W, left=d * MB_H) for d in range(2)])
        y_mf, y_mb = _ssd2(proj, small, dt_bias, a_rows, ee_mb, ee2_mb)

        bias_i = _pad_cols(ml_gate_b[l][:, 0, :].reshape(1, -1), SMALL_W, left=ML_LANE)
        bias_f = _pad_cols(ml_gate_b[l][:, 1, :].reshape(1, -1), SMALL_W, left=ML_LANE)
        h_f, h_b = _mlstm2(proj, small, bias_i, bias_f, ee_ml)

        y_na = _na(proj, _na_bias_table(na_rpb[l]))

        x = _outproj2(x, y_hy, proj, y_mf, y_mb, h_f, h_b, y_na, jnp.repeat(mb_d[l], MB_P)[None], mb_norm_w[l][None],
                      ml_norm_w[l][None], w_out[l].astype(jnp.bfloat16).reshape(4, GROUP_W, D))
    return x
```

```python
import functools
import math

import jax
import jax.numpy as jnp
from jax import lax
from jax.experimental import pallas as pl
from jax.experimental.pallas import tpu as pltpu

RMS_EPS = 1e-6
GROUP_W = 512
TILE_N = 512
SMALL_W = 128
SMALL_OUT = 2 * SMALL_W
HALO = 16
INPROJ_ROWS = 512
OUTPROJ_ROWS = 1024
OUTPROJ_SUB = 512
VMEM_LIMIT = 56 * 1024 * 1024

N_HY_TILES = 3
N_CONV_TILES = 7
N_TILES = 16
T_MB_X, T_MB_BC = 0, 1
T_ML_Q, T_ML_K = 2, 3
T_HY_G, T_MB_Z, T_ML_V, T_ML_O, T_ML_Z = 4, 5, 6, 7, 8
T_NA_Q, T_NA_K, T_NA_V, T_NA_G = 9, 10, 11, 12
N_MAIN_TILES = N_TILES - N_HY_TILES

_HI = lax.Precision.HIGHEST


def _silu(x):
    return x * jax.nn.sigmoid(x)


def _inproj_kernel(x_ref, xp_ref, xn_ref, nw_ref, w_ref, ws_ref, taps_ref, cb_ref, cs_ref, gm_ref,
                   o_ref, oh_ref, os_ref, h_ref, acc_ref, *, tm):
    i = pl.program_id(1)
    ni = pl.num_programs(1)

    def norm(xv):
        ms = jnp.mean(xv * xv, axis=-1, keepdims=True)
        return (xv * lax.rsqrt(ms + RMS_EPS) * nw_ref[...]).astype(jnp.bfloat16)

    h_ref[pl.ds(HALO, tm), :] = norm(x_ref[0])
    hp = norm(xp_ref[0])
    hn = norm(xn_ref[0])
    h_ref[pl.ds(0, HALO), :] = jnp.where(i == 0, jnp.zeros_like(hp), hp)
    h_ref[pl.ds(HALO + tm, HALO), :] = jnp.where(i == ni - 1, jnp.zeros_like(hn), hn)
    os_ref[0] = jnp.dot(h_ref[pl.ds(HALO, tm), :], ws_ref[...], preferred_element_type=jnp.float32)

    qk_tiles = (T_NA_Q + N_HY_TILES, T_NA_K + N_HY_TILES)

    def matmul(u):
        wt = w_ref[:, u * TILE_N:(u + 1) * TILE_N]
        if u < N_CONV_TILES:
            acc_ref[u % 2] = jnp.dot(h_ref[...], wt, preferred_element_type=jnp.float32)
        else:
            acc_ref[u % 2, pl.ds(0, tm), :] = jnp.dot(h_ref[pl.ds(HALO, tm), :], wt, preferred_element_type=jnp.float32)

    def epilogue(u):
        a = acc_ref.at[u % 2]
        cols = slice(u * TILE_N, (u + 1) * TILE_N)
        if u < N_CONV_TILES:
            t = taps_ref[:, cols]
            y = (a[pl.ds(HALO - 1, tm), :] * t[0:1] + a[pl.ds(HALO, tm), :] * t[1:2]
                 + a[pl.ds(HALO + 1, tm), :] * t[2:3] + cb_ref[:, cols])
            if u < N_HY_TILES:
                oh_ref[0, :, cols] = y.astype(oh_ref.dtype)
                return
            y = _silu(y)
        else:
            y = a[pl.ds(0, tm), :]
            if u in qk_tiles:
                ms = jnp.dot((y * y).astype(jnp.bfloat16), gm_ref[...], preferred_element_type=jnp.float32)
                y = y * lax.rsqrt(ms + RMS_EPS) * cs_ref[:, cols]
        o_ref[0, :, (u - N_HY_TILES) * TILE_N:(u - N_HY_TILES + 1) * TILE_N] = y.astype(o_ref.dtype)

    matmul(0)
    for u in range(N_TILES):
        if u + 1 < N_TILES:
            matmul(u + 1)
        epilogue(u)


def _inproj(x, nw, w, ws, taps, cbias, cscale, gmean):
    B, L, D = x.shape
    tm = min(INPROJ_ROWS, L)
    ni = L // tm
    hb = tm // HALO
    nlast = L // HALO - 1
    ncols = N_TILES * TILE_N
    const = lambda b, i: (0, 0)
    tok = lambda b, i: (b, i, 0)
    return pl.pallas_call(
        functools.partial(_inproj_kernel, tm=tm),
        grid=(B, ni),
        in_specs=[
            pl.BlockSpec((1, tm, D), tok),
            pl.BlockSpec((1, HALO, D), lambda b, i: (b, jnp.maximum(i * hb - 1, 0), 0)),
            pl.BlockSpec((1, HALO, D), lambda b, i: (b, jnp.minimum((i + 1) * hb, nlast), 0)),
            pl.BlockSpec((1, D), const),
            pl.BlockSpec((D, ncols), const, pipeline_mode=pl.Buffered(1)),
            pl.BlockSpec((D, SMALL_OUT), const),
            pl.BlockSpec((3, ncols), const),
            pl.BlockSpec((1, ncols), const),
            pl.BlockSpec((1, ncols), const),
            pl.BlockSpec((TILE_N, TILE_N), const),
        ],
        out_specs=[
            pl.BlockSpec((1, tm, N_MAIN_TILES * TILE_N), tok),
            pl.BlockSpec((1, tm, N_HY_TILES * TILE_N), tok),
            pl.BlockSpec((1, tm, SMALL_OUT), tok),
        ],
        out_shape=[
            jax.ShapeDtypeStruct((B, L, N_MAIN_TILES * TILE_N), jnp.bfloat16),
            jax.ShapeDtypeStruct((B, L, N_HY_TILES * TILE_N), jnp.bfloat16),
            jax.ShapeDtypeStruct((B, L, SMALL_OUT), jnp.float32),
        ],
        scratch_shapes=[
            pltpu.VMEM((tm + 2 * HALO, D), jnp.bfloat16),
            pltpu.VMEM((2, tm + 2 * HALO, TILE_N), jnp.float32),
        ],
        compiler_params=pltpu.CompilerParams(
            dimension_semantics=("arbitrary", "arbitrary"), vmem_limit_bytes=VMEM_LIMIT),
        name="inproj",
    )(x, x, x, nw, w, ws, taps, cbias, cscale, gmean)


CHUNK = 256
CHUNKS_PER_STEP = 4
STEP_TOKENS = CHUNKS_PER_STEP * CHUNK
NEG = -1e30


def _sub_chunk(ref, k):
    return ref.at[:, pl.ds(k * CHUNK, CHUNK), :]


def _dot_nt(a, b):
    return lax.dot_general(a, b, (((1,), (1,)), ((), ())), preferred_element_type=jnp.float32)


def _dot_tn(a, b):
    return lax.dot_general(a, b, (((0,), (0,)), ((), ())), preferred_element_type=jnp.float32)


def _chunk_mask(q, reverse):
    t = lax.broadcasted_iota(jnp.int32, (q, q), 0)
    s = lax.broadcasted_iota(jnp.int32, (q, q), 1)
    return (s >= t) if reverse else (s <= t)


MB_H = 8
MB_P = 64
MB_N = 128
MB_GW = 256
ML_H = 4
ML_DH = 128


LOG2E = math.log2(math.e)


def _split_cols(x):
    hi = x.astype(jnp.bfloat16)
    lo = (x - hi.astype(jnp.float32)).astype(jnp.bfloat16)
    return jnp.concatenate([hi, lo], axis=1)


def _split_rows(x):
    hi = x.astype(jnp.bfloat16)
    lo = (x - hi.astype(jnp.float32)).astype(jnp.bfloat16)
    return jnp.concatenate([hi, lo], axis=0)


def _tri2(mask):
    tri = mask.astype(jnp.bfloat16)
    return jnp.concatenate([tri, tri], axis=1)


def _ssd_direction(xs_ref, bc_ref, sm_ref, bias, a_row, ee, ee2, s_ref, o_ref, *, reverse, dcol):
    q = CHUNK
    mask = _chunk_mask(q, reverse)
    dt_s = jax.nn.softplus(sm_ref[0] + bias)
    a_s = dt_s * a_row
    c_s = jnp.dot(_tri2(mask), _split_rows(a_s), preferred_element_type=jnp.float32)
    c_t = c_s.T
    full = jnp.dot(_split_cols(jnp.concatenate([c_s, dt_s], axis=0)), ee, preferred_element_type=jnp.float32)
    c_full, dt_full = full[:q], full[q:]
    cc_all = jnp.dot(_split_cols(c_s), ee2, preferred_element_type=jnp.float32)
    far = 0 if reverse else q - 1
    tot_full = c_full[far:far + 1, :]
    x = xs_ref[0].astype(jnp.float32)
    dtx = dt_full * x
    lane = lax.broadcasted_iota(jnp.int32, (1, MB_GW), 1)
    ys = []
    yield
    for g in range(2):
        bg = bc_ref[0, :, g * MB_N:(g + 1) * MB_N]
        cg = bc_ref[0, :, MB_GW + g * MB_N:MB_GW + (g + 1) * MB_N]
        gs = slice(g * MB_GW, (g + 1) * MB_GW)
        gram = _dot_nt(cg, bg)
        dtx_g = dtx[:, gs]
        ms, xms = [], []
        for j in range(4):
            hd = g * 4 + j
            col = dcol + hd
            decay = jnp.exp2(jnp.where(mask, cc_all[:, hd * q:(hd + 1) * q] - c_t[col:col + 1, :], NEG))
            ms.append((gram * decay).astype(jnp.bfloat16))
            xms.append(jnp.where((lane >= j * MB_P) & (lane < (j + 1) * MB_P), dtx_g, 0.0).astype(jnp.bfloat16))
        yg = jnp.dot(jnp.concatenate(ms, axis=1), jnp.concatenate(xms, axis=0), preferred_element_type=jnp.float32)
        s_g = s_ref[:, gs]
        yg = yg + jnp.exp2(c_full[:, gs]) * jnp.dot(cg, s_g.astype(jnp.bfloat16), preferred_element_type=jnp.float32)
        w = jnp.exp2(tot_full[:, gs] - c_full[:, gs])
        s_ref[:, gs] = jnp.exp2(tot_full[:, gs]) * s_g + _dot_tn(bg, (w * dtx_g).astype(jnp.bfloat16))
        ys.append(yg)
        yield
    o_ref[0] = jnp.concatenate(ys, axis=-1).astype(o_ref.dtype)


def _interleave(*gens):
    live = list(gens)
    while live:
        for g in list(live):
            if next(g, StopIteration) is StopIteration:
                live.remove(g)


def _ssd2_kernel(xf_ref, bcf_ref, smf_ref, xb_ref, bcb_ref, smb_ref, bias_ref, a_ref, ee_ref, ee2_ref,
                 of_ref, ob_ref, s_ref):
    @pl.when(pl.program_id(1) == 0)
    def _():
        s_ref[...] = jnp.zeros_like(s_ref)

    for k in range(CHUNKS_PER_STEP):
        f = lambda r: _sub_chunk(r, k)
        g = lambda r: _sub_chunk(r, CHUNKS_PER_STEP - 1 - k)
        _interleave(
            _ssd_direction(f(xf_ref), f(bcf_ref), f(smf_ref), bias_ref[0:1], a_ref[0:1], ee_ref[0], ee2_ref[0],
                           s_ref.at[0], f(of_ref), reverse=False, dcol=0),
            _ssd_direction(g(xb_ref), g(bcb_ref), g(smb_ref), bias_ref[1:2], a_ref[1:2], ee_ref[1], ee2_ref[1],
                           s_ref.at[1], g(ob_ref), reverse=True, dcol=MB_H))


def _ssd2(proj, small, bias, a_row, ee, ee2):
    B, L, _ = proj.shape
    nc = L // STEP_TOKENS
    fw = lambda col: (lambda b, i: (b, i, col))
    bw = lambda col: (lambda b, i: (b, nc - 1 - i, col))
    blk = (1, STEP_TOKENS, GROUP_W)
    sblk = (1, STEP_TOKENS, SMALL_W)
    const2 = lambda b, i: (0, 0)
    const3 = lambda b, i: (0, 0, 0)
    out = jax.ShapeDtypeStruct((B, L, GROUP_W), jnp.bfloat16)
    return pl.pallas_call(
        _ssd2_kernel,
        grid=(B, nc),
        in_specs=[
            pl.BlockSpec(blk, fw(T_MB_X)), pl.BlockSpec(blk, fw(T_MB_BC)), pl.BlockSpec(sblk, fw(0)),
            pl.BlockSpec(blk, bw(T_MB_X)), pl.BlockSpec(blk, bw(T_MB_BC)), pl.BlockSpec(sblk, bw(0)),
            pl.BlockSpec(bias.shape, const2), pl.BlockSpec(a_row.shape, const2),
            pl.BlockSpec(ee.shape, const3), pl.BlockSpec(ee2.shape, const3),
        ],
        out_specs=[pl.BlockSpec(blk, fw(0)), pl.BlockSpec(blk, bw(0))],
        out_shape=[out, out],
        scratch_shapes=[pltpu.VMEM((2, MB_N, GROUP_W), jnp.float32)],
        compiler_params=pltpu.CompilerParams(
            dimension_semantics=("arbitrary", "arbitrary"), vmem_limit_bytes=VMEM_LIMIT),
        name="ssd",
    )(proj, proj, small, proj, proj, small, bias, a_row, ee, ee2)


ML_LANE = 16


def _cummax_rows(u, reverse):
    n = u.shape[0]
    row = lax.broadcasted_iota(jnp.int32, u.shape, 0)
    k = 1
    while k < n:
        if reverse:
            shifted = jnp.where(row < n - k, pltpu.roll(u, n - k, axis=0), NEG)
        else:
            shifted = jnp.where(row >= k, pltpu.roll(u, k, axis=0), NEG)
        u = jnp.maximum(u, shifted)
        k *= 2
    return u


def _mlstm_direction(q_ref, k_ref, v_ref, si_ref, sf_ref, bias_i, bias_f, ee, st_ref, m_ref, o_ref, *, reverse, lane0):
    q = CHUNK
    mask = _chunk_mask(q, reverse)
    gi = si_ref[0] + bias_i
    lf = jax.nn.log_sigmoid(sf_ref[0] + bias_f)
    b = jnp.dot(_tri2(mask), _split_rows(lf), preferred_element_type=jnp.float32)
    u = gi - b
    far = 0 if reverse else q - 1
    m_prev = m_ref[...]
    m_inter = b + m_prev
    m_t = jnp.maximum(m_inter, b + _cummax_rows(u, reverse))
    w_inter = jnp.exp(m_inter - m_t)
    em = jnp.exp(-m_t)
    b_tot = b[far:far + 1, :]
    g = b_tot - b + gi
    m_new = jnp.maximum(b_tot + m_prev, jnp.max(g, axis=0, keepdims=True))
    wk = jnp.exp(g - m_new) * (ML_DH ** -0.5)
    decay = jnp.exp(b_tot + m_prev - m_new)
    m_ref[...] = m_new
    arow = (b - m_t) * LOG2E + math.log2(ML_DH ** -0.5)
    ee_dh, ee_q = ee[:, :GROUP_W], ee[:, GROUP_W:]
    stack = jnp.concatenate([w_inter, em, wk], axis=0)
    full = jnp.dot(_split_cols(stack), ee_dh, preferred_element_type=jnp.float32)
    w_full, em_full, wk_full = full[:q], full[q:2 * q], full[2 * q:]
    a_all = jnp.dot(_split_cols(arow), ee_q, preferred_element_type=jnp.float32)
    dec_full = jnp.dot(_split_cols(jnp.broadcast_to(decay, (8, SMALL_W))), ee_dh,
                       preferred_element_type=jnp.float32)[0:1]
    u_t = (u * LOG2E).T
    ones = jnp.ones((q, ML_DH), jnp.bfloat16)
    hs = []
    yield
    for h in range(ML_H):
        hsl = slice(h * ML_DH, (h + 1) * ML_DH)
        p = jnp.exp2(jnp.where(mask, a_all[:, h * q:(h + 1) * q] + u_t[lane0 + h:lane0 + h + 1, :], NEG))
        qh = q_ref[0, :, hsl]
        kh = k_ref[0, :, hsl]
        v1 = jnp.concatenate([v_ref[0, :, hsl], ones], axis=1)
        s = (_dot_nt(qh, kh) * p).astype(jnp.bfloat16)
        st = st_ref[h]
        inter = jnp.dot(qh, st.astype(jnp.bfloat16), preferred_element_type=jnp.float32)
        both = w_full[:, hsl] * inter[:, :ML_DH], w_full[:, hsl] * inter[:, ML_DH:]
        intra = jnp.dot(s, v1, preferred_element_type=jnp.float32)
        num = both[0] + intra[:, :ML_DH]
        den = both[1] + intra[:, ML_DH:]
        hs.append(num / jnp.maximum(jnp.abs(den), em_full[:, hsl]))
        kw = (kh.astype(jnp.float32) * wk_full[:, hsl]).astype(jnp.bfloat16)
        dh = dec_full[:, hsl]
        st_ref[h] = jnp.concatenate([dh, dh], axis=1) * st + _dot_tn(kw, v1)
        yield
    o_ref[0] = jnp.concatenate(hs, axis=-1).astype(o_ref.dtype)


def _mlstm2_kernel(qf_ref, kf_ref, vf_ref, sif_ref, sff_ref, qb_ref, kb_ref, vb_ref, sib_ref, sfb_ref,
                   bi_ref, bf_ref, ee_ref, of_ref, ob_ref, st_ref, m_ref):
    @pl.when(pl.program_id(1) == 0)
    def _():
        st_ref[...] = jnp.zeros_like(st_ref)
        m_ref[...] = jnp.zeros_like(m_ref)

    for k in range(CHUNKS_PER_STEP):
        f = lambda r: _sub_chunk(r, k)
        g = lambda r: _sub_chunk(r, CHUNKS_PER_STEP - 1 - k)
        _interleave(
            _mlstm_direction(f(qf_ref), f(kf_ref), f(vf_ref), f(sif_ref), f(sff_ref), bi_ref[...], bf_ref[...],
                             ee_ref[0], st_ref.at[0], m_ref.at[0], f(of_ref), reverse=False, lane0=ML_LANE),
            _mlstm_direction(g(qb_ref), g(kb_ref), g(vb_ref), g(sib_ref), g(sfb_ref), bi_ref[...], bf_ref[...],
                             ee_ref[1], st_ref.at[1], m_ref.at[1], g(ob_ref), reverse=True, lane0=ML_LANE + ML_H))


def _mlstm2(proj, small, bias_i, bias_f, ee):
    B, L, _ = proj.shape
    nc = L // STEP_TOKENS
    fw = lambda col: (lambda b, i: (b, i, col))
    bw = lambda col: (lambda b, i: (b, nc - 1 - i, col))
    blk = (1, STEP_TOKENS, GROUP_W)
    sblk = (1, STEP_TOKENS, SMALL_W)
    const2 = lambda b, i: (0, 0)
    out = jax.ShapeDtypeStruct((B, L, GROUP_W), jnp.bfloat16)
    specs = lambda m: [pl.BlockSpec(blk, m(T_ML_Q)), pl.BlockSpec(blk, m(T_ML_K)), pl.BlockSpec(blk, m(T_ML_V)),
                       pl.BlockSpec(sblk, m(0)), pl.BlockSpec(sblk, m(1))]
    return pl.pallas_call(
        _mlstm2_kernel,
        grid=(B, nc),
        in_specs=specs(fw) + specs(bw) + [
            pl.BlockSpec(bias_i.shape, const2), pl.BlockSpec(bias_f.shape, const2),
            pl.BlockSpec(ee.shape, lambda b, i: (0, 0, 0)),
        ],
        out_specs=[pl.BlockSpec(blk, fw(0)), pl.BlockSpec(blk, bw(0))],
        out_shape=[out, out],
        scratch_shapes=[
            pltpu.VMEM((2, ML_H, ML_DH, 2 * ML_DH), jnp.float32),
            pltpu.VMEM((2, 1, SMALL_W), jnp.float32),
        ],
        compiler_params=pltpu.CompilerParams(
            dimension_semantics=("arbitrary", "arbitrary"), vmem_limit_bytes=VMEM_LIMIT),
        name="mlstm",
    )(proj, proj, proj, small, small, proj, proj, proj, small, small, bias_i, bias_f, ee)


def _group_rmsnorm(y, width):
    outs = []
    for g in range(y.shape[-1] // width):
        yg = y[:, g * width:(g + 1) * width]
        ms = jnp.mean(yg * yg, axis=-1, keepdims=True)
        outs.append(yg * lax.rsqrt(ms + RMS_EPS))
    return jnp.concatenate(outs, axis=-1)


def _outproj2_kernel(x_ref, yh_ref, hg_ref, mf_ref, mb_ref, mx_ref, mz_ref, lf_ref, lb_ref, lo_ref, lz_ref, yn_ref,
                     dsk_ref, mnw_ref, lnw_ref, w_ref, o_ref):
    f32 = jnp.float32
    for hh in range(x_ref.shape[1] // OUTPROJ_SUB):
        rows = pl.ds(hh * OUTPROJ_SUB, OUTPROJ_SUB)
        up = lambda r: r[0, rows, :].astype(f32)
        yh = (up(yh_ref) * _silu(up(hg_ref))).astype(jnp.bfloat16)
        acc = jnp.dot(yh, w_ref[0], preferred_element_type=f32)
        ym = (up(mf_ref) + up(mb_ref) + up(mx_ref) * dsk_ref[...]) * _silu(up(mz_ref))
        ym = (_group_rmsnorm(ym, MB_GW) * mnw_ref[...]).astype(jnp.bfloat16)
        acc += jnp.dot(ym, w_ref[1], preferred_element_type=f32)
        yl = (up(lf_ref) + up(lb_ref)) * jax.nn.sigmoid(up(lo_ref))
        yl = (_group_rmsnorm(yl, ML_DH) * lnw_ref[...] * _silu(up(lz_ref))).astype(jnp.bfloat16)
        acc += jnp.dot(yl, w_ref[2], preferred_element_type=f32)
        acc += jnp.dot(yn_ref[0, rows, :], w_ref[3], preferred_element_type=f32)
        o_ref[0, rows, :] = x_ref[0, rows, :] + acc


def _outproj2(x, y_hy, proj, y_mf, y_mb, h_f, h_b, y_na, dskip, mb_nw, ml_nw, w_out):
    B, L, D = x.shape
    tm = min(OUTPROJ_ROWS, L)
    tok = lambda b, i: (b, i, 0)
    col = lambda c: (lambda b, i: (b, i, c))
    blk = (1, tm, GROUP_W)
    vec = pl.BlockSpec((1, GROUP_W), lambda b, i: (0, 0))
    return pl.pallas_call(
        _outproj2_kernel,
        grid=(B, L // tm),
        in_specs=[
            pl.BlockSpec((1, tm, D), tok),
            pl.BlockSpec(blk, tok), pl.BlockSpec(blk, col(T_HY_G)),
            pl.BlockSpec(blk, tok), pl.BlockSpec(blk, tok), pl.BlockSpec(blk, col(T_MB_X)), pl.BlockSpec(blk, col(T_MB_Z)),
            pl.BlockSpec(blk, tok), pl.BlockSpec(blk, tok), pl.BlockSpec(blk, col(T_ML_O)), pl.BlockSpec(blk, col(T_ML_Z)),
            pl.BlockSpec(blk, tok),
            vec, vec, vec,
            pl.BlockSpec((4, GROUP_W, D), lambda b, i: (0, 0, 0)),
        ],
        out_specs=pl.BlockSpec((1, tm, D), tok),
        out_shape=jax.ShapeDtypeStruct((B, L, D), jnp.float32),
        compiler_params=pltpu.CompilerParams(
            dimension_semantics=("arbitrary", "arbitrary"), vmem_limit_bytes=VMEM_LIMIT),
        name="outproj",
    )(x, y_hy, proj, y_mf, y_mb, proj, proj, h_f, h_b, proj, proj, y_na, dskip, mb_nw, ml_nw, w_out)


GRID_W = 64
NA_KR = 8
NA_KC = 16
NA_H = 8
NA_DH = 64
ROWS_PER_BLOCK = 8
BLOCK_TOK = ROWS_PER_BLOCK * GRID_W


def _na_kernel(q_ref, kp_ref, kc_ref, kn_ref, vp_ref, vc_ref, vn_ref, g_ref, tbl_ref, o_ref, k_s, v_s, *, rows):
    mblk = pl.program_id(1)
    npair = NA_H // 2
    pw = 2 * NA_DH
    ones = jnp.ones((BLOCK_TOK, pw), jnp.bfloat16)
    for t, (kr, vr) in enumerate(((kp_ref, vp_ref), (kc_ref, vc_ref), (kn_ref, vn_ref))):
        k_s[pl.ds(t * BLOCK_TOK, BLOCK_TOK), :] = kr[0]
        for hp in range(npair):
            v_s[pl.ds(t * BLOCK_TOK, BLOCK_TOK), 2 * hp * pw:(2 * hp + 1) * pw] = vr[0, :, hp * pw:(hp + 1) * pw]
            v_s[pl.ds(t * BLOCK_TOK, BLOCK_TOK), (2 * hp + 1) * pw:(2 * hp + 2) * pw] = ones
    lane = lax.broadcasted_iota(jnp.int32, (GRID_W, pw), 1)
    first = lane < NA_DH
    win = NA_KR * GRID_W
    for j in range(ROWS_PER_BLOCK):
        r = mblk * ROWS_PER_BLOCK + j
        rs = jnp.clip(r - NA_KR // 2, 0, rows - NA_KR)
        didx = r - rs
        off = pl.multiple_of((rs - (mblk - 1) * ROWS_PER_BLOCK) * GRID_W, GRID_W)
        ss = []
        for hp in range(npair):
            ls = slice(hp * pw, (hp + 1) * pw)
            qp = q_ref[0, j * GRID_W:(j + 1) * GRID_W, ls]
            zero = jnp.zeros_like(qp)
            q2 = jnp.concatenate([jnp.where(first, qp, zero), jnp.where(first, zero, qp)], axis=0)
            ss.append(_dot_nt(q2, k_s[pl.ds(off, win), ls]))
        s = jnp.concatenate(ss, axis=0) + tbl_ref[didx].astype(jnp.float32)
        e = jnp.exp2(s - jnp.max(s, axis=-1, keepdims=True)).astype(jnp.bfloat16)
        outs = []
        for hp in range(npair):
            ov = jnp.dot(e[hp * pw:(hp + 1) * pw], v_s[pl.ds(off, win), 2 * hp * pw:(2 * hp + 2) * pw],
                         preferred_element_type=jnp.float32)
            o2 = ov[:, :pw] / ov[:, pw:]
            outs.append(jnp.where(first, o2[:GRID_W], o2[GRID_W:]))
        o = jnp.concatenate(outs, axis=-1)
        gate = _silu(g_ref[0, j * GRID_W:(j + 1) * GRID_W, :].astype(jnp.float32))
        o_ref[0, j * GRID_W:(j + 1) * GRID_W, :] = (o * gate).astype(o_ref.dtype)


def _na(proj, tbl):
    B, L, _ = proj.shape
    rows = L // GRID_W
    nb = rows // ROWS_PER_BLOCK
    prev = lambda col: (lambda b, m: (b, jnp.maximum(m - 1, 0), col))
    cur = lambda col: (lambda b, m: (b, m, col))
    nxt = lambda col: (lambda b, m: (b, jnp.minimum(m + 1, nb - 1), col))
    blk = (1, BLOCK_TOK, GROUP_W)
    return pl.pallas_call(
        functools.partial(_na_kernel, rows=rows),
        grid=(B, nb),
        in_specs=[
            pl.BlockSpec(blk, cur(T_NA_Q)),
            pl.BlockSpec(blk, prev(T_NA_K)), pl.BlockSpec(blk, cur(T_NA_K)), pl.BlockSpec(blk, nxt(T_NA_K)),
            pl.BlockSpec(blk, prev(T_NA_V)), pl.BlockSpec(blk, cur(T_NA_V)), pl.BlockSpec(blk, nxt(T_NA_V)),
            pl.BlockSpec(blk, cur(T_NA_G)),
            pl.BlockSpec(tbl.shape, lambda b, m: (0, 0, 0)),
        ],
        out_specs=pl.BlockSpec(blk, cur(0)),
        out_shape=jax.ShapeDtypeStruct((B, L, GROUP_W), jnp.bfloat16),
        scratch_shapes=[
            pltpu.VMEM((3 * BLOCK_TOK, GROUP_W), jnp.bfloat16),
            pltpu.VMEM((3 * BLOCK_TOK, 2 * GROUP_W), jnp.bfloat16),
        ],
        compiler_params=pltpu.CompilerParams(
            dimension_semantics=("arbitrary", "arbitrary"), vmem_limit_bytes=VMEM_LIMIT),
        name="nbr_attn",
    )(proj, proj, proj, proj, proj, proj, proj, proj, tbl)


def _na_bias_table(rpb):
    nco = 2 * NA_KC - 1
    rows_d = jnp.stack([rpb[:, NA_KR - 1 - d:2 * NA_KR - 1 - d, :] for d in range(NA_KR)])
    lpad = GRID_W - NA_KC
    ext = jnp.pad(rows_d * math.log2(math.e), ((0, 0), (0, 0), (0, 0), (lpad, 2 * GRID_W - lpad - nco)))
    lead = ext.shape[:3]
    skew = jnp.broadcast_to(ext[..., None, :], lead + (GRID_W, 2 * GRID_W)).reshape(lead + (2 * GRID_W * GRID_W,))
    skew = skew[..., :GRID_W * (2 * GRID_W - 1)].reshape(lead + (GRID_W, 2 * GRID_W - 1))[..., GRID_W - 1:]
    w = jnp.arange(GRID_W)
    c = jnp.arange(GRID_W)
    cstart = jnp.clip(w - NA_KC // 2, 0, GRID_W - NA_KC)
    inwin = (c[None, :] >= cstart[:, None]) & (c[None, :] < cstart[:, None] + NA_KC)
    t = jnp.where(inwin, skew, NEG)
    t = t.transpose(0, 1, 3, 2, 4).reshape(NA_KR, NA_H * GRID_W, NA_KR * GRID_W)
    return t.astype(jnp.bfloat16)


HY_LANES = 256
TBL_LANES = 128
F2N = 128
TW_B = 16
PAD_ROWS = 8
HY_POS_PAD = 128


def _cmul(ar, ai, br, bi):
    return ar * br - ai * bi, ar * bi + ai * br


def _pack_c(re, im):
    half = jnp.uint32(0x8000)
    rb = lax.bitcast_convert_type(re, jnp.uint32) + half
    ib = lax.bitcast_convert_type(im, jnp.uint32) + half
    return (rb & jnp.uint32(0xFFFF0000)) | (ib >> 16)


def _unpack_c(w):
    re = lax.bitcast_convert_type(w & jnp.uint32(0xFFFF0000), jnp.float32)
    im = lax.bitcast_convert_type(w << 16, jnp.float32)
    return re, im


def _tile_lanes(t):
    return jnp.concatenate([t] * (HY_LANES // TBL_LANES), axis=1)


LANE_PARTS = HY_LANES // TBL_LANES


def _rows_load(refs_or_ref, idx):
    parts = [refs_or_ref[p, idx, :] for p in range(LANE_PARTS)] if not isinstance(refs_or_ref, (list, tuple)) \
        else [r[idx, :] for r in refs_or_ref]
    return jnp.concatenate(parts, axis=1)


def _rows_store(buf_ref, idx, val):
    for p in range(LANE_PARTS):
        buf_ref[p, idx, :] = val[:, p * TBL_LANES:(p + 1) * TBL_LANES]


def _stage_a_store(buf_ref, a, n2, tr, ti, n1f, rs):
    ar, ai = _cmul(a[:n1f], a[n1f:], _tile_lanes(tr), _tile_lanes(ti))
    _rows_store(buf_ref, pl.ds(pl.multiple_of(n2 * rs, 8), n1f), _pack_c(ar, ai))


def _stage_b_load(buf_ref, k1, rs):
    yr, yi = _unpack_c(_rows_load(buf_ref, pl.ds(k1, F2N, stride=rs)))
    return jnp.concatenate([yr, yi], axis=0).astype(jnp.bfloat16)


def _hy_spec_kernel(*refs, n1f, h1, rs, ns):
    ff_refs, fb_refs = refs[:LANE_PARTS], refs[LANE_PARTS:2 * LANE_PARTS]
    mg_ref, tbr_ref, tbi_ref, f2_ref, o_ref, buf_ref = refs[2 * LANE_PARTS:]
    s = pl.program_id(2)

    @pl.when(s < ns)
    def _():
        for b in range(TW_B):
            n2 = s * TW_B + b
            hf = _rows_load(ff_refs, pl.ds(n2, h1, stride=F2N))
            hb = _rows_load(fb_refs, pl.ds(F2N - n2, h1, stride=F2N))
            xs = jnp.concatenate([hf, hb], axis=0).astype(jnp.bfloat16)
            a = jnp.dot(mg_ref[0], xs, preferred_element_type=jnp.float32)
            _stage_a_store(buf_ref, a, n2, tbr_ref[b], tbi_ref[b], n1f, rs)

    @pl.when(s >= ns)
    def _():
        for kk in range(TW_B):
            k1 = (s - ns) * TW_B + kk
            o_ref[0, kk] = jnp.dot(f2_ref[0], _stage_b_load(buf_ref, k1, rs), preferred_element_type=jnp.float32)


def _hy_spectrum(filt, mg, tbr, tbi, f2, L):
    n1f = 2 * L // F2N
    h1 = n1f // 2
    rs = n1f + PAD_ROWS
    ns = F2N // TW_B
    nk = n1f // TW_B
    ncb = GROUP_W // HY_LANES
    a_idx = lambda o, c, s: (jnp.minimum(s, ns - 1), 0, 0)
    once = pl.Buffered(1)
    nparts = GROUP_W // TBL_LANES

    def filt_spec(direction, part):
        return pl.BlockSpec(
            (L + F2N, TBL_LANES),
            lambda o, c, s: (0, (o * 2 + direction) * nparts + c * LANE_PARTS + part), pipeline_mode=once)

    return pl.pallas_call(
        functools.partial(_hy_spec_kernel, n1f=n1f, h1=h1, rs=rs, ns=ns),
        grid=(2, ncb, ns + nk),
        in_specs=[filt_spec(0, p) for p in range(LANE_PARTS)] + [filt_spec(1, p) for p in range(LANE_PARTS)] + [
            pl.BlockSpec((1, 2 * n1f, n1f), a_idx),
            pl.BlockSpec(tbr.shape, lambda o, c, s: (0, 0, 0)),
            pl.BlockSpec(tbi.shape, lambda o, c, s: (0, 0, 0)),
            pl.BlockSpec((1, 2 * F2N, 2 * F2N), lambda o, c, s: (0, 0, 0)),
        ],
        out_specs=pl.BlockSpec((1, TW_B, 2 * F2N, HY_LANES), lambda o, c, s: (o, jnp.maximum(s - ns, 0), 0, c)),
        out_shape=jax.ShapeDtypeStruct((2, n1f, 2 * F2N, GROUP_W), jnp.float32),
        scratch_shapes=[pltpu.VMEM((LANE_PARTS, F2N * rs, TBL_LANES), jnp.uint32)],
        compiler_params=pltpu.CompilerParams(
            dimension_semantics=("arbitrary", "arbitrary", "arbitrary"), vmem_limit_bytes=VMEM_LIMIT),
        name="hyena_spectrum",
    )(*([filt] * (2 * LANE_PARTS)), mg, tbr, tbi, f2)


def _hy_conv_kernel(v_ref, x1_ref, x2_ref, g_ref, m1_ref, m3_ref, tbr_ref, tbi_ref, f2_ref, skip_ref,
                    o_ref, buf_ref, z_ref, *, n1f, h1, rs, ns, nk):
    s = pl.program_id(2)
    p1 = ns
    p2 = p1 + nk
    p3 = p2 + ns
    p4 = p3 + nk

    def stage_a(xs, n2, b):
        a = jnp.dot(m1_ref[0], xs, preferred_element_type=jnp.float32)
        _stage_a_store(buf_ref, a, n2, tbr_ref[b], tbi_ref[b], n1f, rs)

    def stage_b(order, kb):
        for kk in range(TW_B):
            k1 = kb * TW_B + kk
            y = jnp.dot(f2_ref[0], _stage_b_load(buf_ref, k1, rs), preferred_element_type=jnp.float32)
            g = g_ref[0, kk]
            yr, yi = _cmul(y[:F2N], y[F2N:], g[:F2N], g[F2N:])
            ys = jnp.concatenate([yr, yi], axis=0).astype(jnp.bfloat16)
            z = jnp.dot(f2_ref[1], ys, preferred_element_type=jnp.float32)
            _rows_store(buf_ref, pl.ds(k1, F2N, stride=rs), _pack_c(z[:F2N], z[F2N:]))

    def stage_c(n2, b):
        zr, zi = _unpack_c(_rows_load(buf_ref, pl.ds(pl.multiple_of(n2 * rs, 8), n1f)))
        wr, wi = _cmul(zr, zi, _tile_lanes(tbr_ref[b]), -_tile_lanes(tbi_ref[b]))
        ws = jnp.concatenate([wr, wi], axis=0).astype(jnp.bfloat16)
        return jnp.dot(m3_ref[0], ws, preferred_element_type=jnp.float32)

    @pl.when(s < p1)
    def _():
        for b in range(TW_B):
            xs = jnp.concatenate([v_ref[0, b], v_ref[1, b]], axis=0)
            stage_a(xs, s * TW_B + b, b)

    @pl.when((s >= p1) & (s < p2))
    def _():
        stage_b(0, s - p1)

    @pl.when((s >= p2) & (s < p3))
    def _():
        for b in range(TW_B):
            n2 = (s - p2) * TW_B + b
            c = stage_c(n2, b)
            zs = []
            for r in range(2):
                vv = v_ref[r, b].astype(jnp.float32)
                z = x1_ref[r, b].astype(jnp.float32) * (c[r * h1:(r + 1) * h1] + skip_ref[0:1, :] * vv)
                zb = z.astype(jnp.bfloat16)
                z_ref[n2, r] = zb
                zs.append(zb)
            stage_a(jnp.concatenate(zs, axis=0), n2, b)

    @pl.when((s >= p3) & (s < p4))
    def _():
        stage_b(1, s - p3)

    @pl.when(s >= p4)
    def _():
        for b in range(TW_B):
            n2 = (s - p4) * TW_B + b
            c = stage_c(n2, b)
            for r in range(2):
                zz = z_ref[n2, r].astype(jnp.float32)
                y = x2_ref[r, b].astype(jnp.float32) * (c[r * h1:(r + 1) * h1] + skip_ref[1:2, :] * zz)
                o_ref[r, b] = y.astype(o_ref.dtype)


def _hy_conv(hyp, spec, m1, m3, tbr, tbi, f2, skip, L):
    B = hyp.shape[0]
    n1f = 2 * L // F2N
    h1 = n1f // 2
    rs = n1f + PAD_ROWS
    ns = F2N // TW_B
    nk = n1f // TW_B
    ncb = GROUP_W // HY_LANES
    p1, p2, p3, p4 = ns, ns + nk, 2 * ns + nk, 2 * ns + 2 * nk
    clip = lambda v, hi: jnp.clip(v, 0, hi)
    tblk = (2, TW_B, h1, HY_LANES)
    v_idx = lambda c, p, s: (p, jnp.where(s < p1, s, clip(s - p2, ns - 1)), 0, c)
    x1_idx = lambda c, p, s: (p, clip(s - p2, ns - 1), 0, ncb + c)
    x2_idx = lambda c, p, s: (p, clip(s - p4, ns - 1), 0, 2 * ncb + c)
    g_idx = lambda c, p, s: (jnp.where(s < p3, 0, 1), jnp.where(s < p3, clip(s - p1, nk - 1), clip(s - p3, nk - 1)), 0, c)
    m1_idx = lambda c, p, s: (jnp.where(s < p1, s, clip(s - p2, ns - 1)), 0, 0)
    m3_idx = lambda c, p, s: (jnp.where(s < p4, clip(s - p2, ns - 1), s - p4), 0, 0)
    return pl.pallas_call(
        functools.partial(_hy_conv_kernel, n1f=n1f, h1=h1, rs=rs, ns=ns, nk=nk),
        grid=(ncb, B // 2, p4 + ns),
        in_specs=[
            pl.BlockSpec(tblk, v_idx),
            pl.BlockSpec(tblk, x1_idx),
            pl.BlockSpec(tblk, x2_idx),
            pl.BlockSpec((1, TW_B, 2 * F2N, HY_LANES), g_idx),
            pl.BlockSpec((1, 2 * n1f, 2 * h1), m1_idx),
            pl.BlockSpec((1, 2 * h1, 2 * n1f), m3_idx),
            pl.BlockSpec(tbr.shape, lambda c, p, s: (0, 0, 0)),
            pl.BlockSpec(tbi.shape, lambda c, p, s: (0, 0, 0)),
            pl.BlockSpec((2, 2 * F2N, 2 * F2N), lambda c, p, s: (0, 0, 0)),
            pl.BlockSpec((2, HY_LANES), lambda c, p, s: (0, c)),
        ],
        out_specs=pl.BlockSpec(tblk, lambda c, p, s: (p, clip(s - p4, ns - 1), 0, c)),
        out_shape=jax.ShapeDtypeStruct((B, F2N, h1, GROUP_W), jnp.bfloat16),
        scratch_shapes=[
            pltpu.VMEM((LANE_PARTS, F2N * rs, TBL_LANES), jnp.uint32),
            pltpu.VMEM((F2N, 2, h1, HY_LANES), jnp.bfloat16),
        ],
        compiler_params=pltpu.CompilerParams(
            dimension_semantics=("arbitrary", "arbitrary", "arbitrary"), vmem_limit_bytes=VMEM_LIMIT),
        name="hyena_conv",
    )(hyp, hyp, hyp, spec, m1, m3, tbr, tbi, f2, skip)


def _hy_filter_kernel(pos_ref, w1_ref, b1_ref, w2_ref, b2_ref, w3_ref, fr_ref, dec_ref, o_ref, *, nblk):
    i = pl.program_id(0)
    pos = pos_ref[...]
    fr = fr_ref[...]
    hid = jnp.sin(fr * (jnp.dot(pos, w1_ref[...], precision=_HI, preferred_element_type=jnp.float32) + b1_ref[...]))
    hid = jnp.sin(fr * (jnp.dot(hid, w2_ref[...], precision=_HI, preferred_element_type=jnp.float32) + b2_ref[...]))
    hi = hid.astype(jnp.bfloat16)
    lo = (hid - hi.astype(jnp.float32)).astype(jnp.bfloat16)
    filt = jnp.dot(jnp.concatenate([hi, lo, hi], axis=1), w3_ref[...], preferred_element_type=jnp.float32)
    filt = filt * jnp.exp(-pos[:, 0:1] * dec_ref[...])
    o_ref[...] = jnp.where(i < nblk, filt, 0.0)


def _hy_filters(pos, w1, b1, w2, b2, w3, freq, decay, L):
    nblk = L // F2N
    nh = w2.shape[0]
    no = w3.shape[1]
    const = lambda i: (0, 0)
    return pl.pallas_call(
        functools.partial(_hy_filter_kernel, nblk=nblk),
        grid=(nblk + 1,),
        in_specs=[
            pl.BlockSpec((F2N, HY_POS_PAD), lambda i: (jnp.minimum(i, nblk - 1), 0)),
            pl.BlockSpec((HY_POS_PAD, nh), const), pl.BlockSpec((1, nh), const),
            pl.BlockSpec((nh, nh), const), pl.BlockSpec((1, nh), const),
            pl.BlockSpec((3 * nh, no), const), pl.BlockSpec((1, nh), const), pl.BlockSpec((1, no), const),
        ],
        out_specs=pl.BlockSpec((F2N, no), lambda i: (i, 0)),
        out_shape=jax.ShapeDtypeStruct((L + F2N, no), jnp.float32),
        compiler_params=pltpu.CompilerParams(dimension_semantics=("arbitrary",), vmem_limit_bytes=VMEM_LIMIT),
        name="hyena_filters",
    )(pos, w1, b1, w2, b2, w3, freq, decay)


def _hy_constants(L):
    n = 2 * L
    n1f = n // F2N
    h1 = n1f // 2
    na = F2N // TW_B

    def cis(num, den):
        ang = (-2.0 * math.pi / den) * (num % den).astype(jnp.float32)
        return jnp.cos(ang), jnp.sin(ang)

    k1 = jnp.arange(n1f)
    n1 = jnp.arange(n1f)
    a = jnp.arange(na)
    f1r, f1i = cis(k1[:, None] * n1[None, :], n1f)
    tar, tai = cis(k1[None, :] * (TW_B * a)[:, None], n)
    mr, mi = _cmul(f1r[None], f1i[None], tar[:, :, None], tai[:, :, None])

    def blockform(r, i):
        return jnp.concatenate([jnp.concatenate([r, -i], axis=-1), jnp.concatenate([i, r], axis=-1)], axis=-2)

    m1 = blockform(mr[:, :, :h1], mi[:, :, :h1]).astype(jnp.bfloat16)
    m3r = jnp.swapaxes(mr[:, :, :h1], 1, 2) / n
    m3i = -jnp.swapaxes(mi[:, :, :h1], 1, 2) / n
    m3 = blockform(m3r, m3i).astype(jnp.bfloat16)
    mgr = jnp.concatenate([mr[:, :, :h1], mr[:, :, h1:][:, :, ::-1]], axis=-1)
    mgi = jnp.concatenate([mi[:, :, :h1], mi[:, :, h1:][:, :, ::-1]], axis=-1)
    mg = jnp.concatenate([mgr, mgi], axis=1).astype(jnp.bfloat16)
    b = jnp.arange(TW_B)
    tbr, tbi = cis(k1[None, :] * b[:, None], n)
    tbr = jnp.broadcast_to(tbr[:, :, None], (TW_B, n1f, TBL_LANES))
    tbi = jnp.broadcast_to(tbi[:, :, None], (TW_B, n1f, TBL_LANES))
    k2 = jnp.arange(F2N)
    f2r, f2i = cis(k2[:, None] * k2[None, :], F2N)
    f2 = jnp.stack([blockform(f2r, f2i), blockform(f2r, -f2i)]).astype(jnp.bfloat16)
    return m1, m3, mg, tbr, tbi, f2


def _hy_positions(L):
    t = jnp.arange(L, dtype=jnp.float32)
    bands = jnp.arange(1, 9, dtype=jnp.float32)
    ang = (2.0 * math.pi / L) * t[:, None] * bands[None, :]
    pos = jnp.concatenate([(t / L)[:, None], jnp.cos(ang), jnp.sin(ang)], axis=-1)
    return jnp.pad(pos, ((0, 0), (0, HY_POS_PAD - pos.shape[1])))


MAIN_COL_RANGES = ((0, 1536), (2048, 3072), (3600, 4624), (1536, 2048), (3072, 3584), (4624, 6160), (6176, 8224))
SMALL_COL_RANGES = ((3584, 3600), (6160, 6176))


def _take_cols(a, ranges):
    return jnp.concatenate([a[..., lo:hi] for lo, hi in ranges], axis=-1)


def _pad_cols(a, width, left=0):
    return jnp.pad(a, [(0, 0)] * (a.ndim - 1) + [(left, width - left - a.shape[-1])])


def kernel(x, norm_w, w_in, w_out, hy_conv_w, hy_conv_b, hy_w1, hy_b1, hy_w2, hy_b2, hy_w3, hy_freq, hy_decay,
           hy_skip, mb_conv_w, mb_conv_b, mb_dt_bias, mb_a_log, mb_d, mb_norm_w, ml_conv_w, ml_conv_b, ml_gate_b,
           ml_norm_w, na_qnorm_w, na_knorm_w, na_rpb):
    B, L, D = x.shape
    depth = w_in.shape[0]
    ncols = N_TILES * TILE_N
    f32 = jnp.float32
    pos = _hy_positions(L)
    m1, m3, mg, tbr, tbi, f2 = _hy_constants(L)
    gmean = jnp.kron(jnp.eye(NA_H, dtype=f32), jnp.full((NA_DH, NA_DH), 1.0 / NA_DH)).astype(jnp.bfloat16)

    def expansion(first_lane, heads, width):
        tgt = jnp.arange(heads * width) // width
        e = [(jnp.arange(SMALL_W)[:, None] == (first_lane + d * heads + tgt)[None, :]) for d in range(2)]
        e = jnp.stack(e).astype(jnp.bfloat16)
        return jnp.concatenate([e, e], axis=1)

    ee_mb = expansion(0, MB_H, MB_P)
    ee2_mb = expansion(0, MB_H, CHUNK)
    ee_ml = jnp.concatenate([expansion(ML_LANE, ML_H, ML_DH), expansion(ML_LANE, ML_H, CHUNK)], axis=2)
    for l in range(depth):
        w = _take_cols(w_in[l], MAIN_COL_RANGES).astype(jnp.bfloat16)
        wl = w_in[l]
        zeros = lambda n: jnp.zeros((D, n), wl.dtype)
        ws = jnp.concatenate([wl[:, 3584:3600], wl[:, 6160:6164], wl[:, 6168:6172], zeros(SMALL_W - 24),
                              zeros(ML_LANE), wl[:, 6164:6168], wl[:, 6172:6176], zeros(SMALL_W - 24)],
                             axis=1).astype(jnp.bfloat16)
        taps = _pad_cols(jnp.concatenate([hy_conv_w[l], mb_conv_w[l], ml_conv_w[l]], axis=-1), ncols)
        cbias = _pad_cols(jnp.concatenate([hy_conv_b[l], mb_conv_b[l], ml_conv_b[l]])[None], ncols)
        qk_scale = jnp.concatenate([jnp.tile(na_qnorm_w[l], NA_H) * (NA_DH ** -0.5 * math.log2(math.e)),
                                    jnp.tile(na_knorm_w[l], NA_H)])[None]
        cscale = _pad_cols(qk_scale, ncols, left=(T_NA_Q + N_HY_TILES) * TILE_N)
        proj, hy, small = _inproj(x, norm_w[l][None], w, ws, taps, cbias, cscale, gmean)

        w3_hi = hy_w3[l].astype(jnp.bfloat16)
        w3_lo = (hy_w3[l] - w3_hi.astype(f32)).astype(jnp.bfloat16)
        filt = _hy_filters(pos, _pad_cols(hy_w1[l].T, HY_POS_PAD).T, hy_b1[l][None], hy_w2[l], hy_b2[l][None],
                           jnp.concatenate([w3_hi, w3_hi, w3_lo]), hy_freq[l][None], hy_decay[l][None], L)
        spec = _hy_spectrum(filt, mg, tbr, tbi, f2[:1], L)
        hyp = hy.reshape(B, L // F2N, F2N, N_HY_TILES * GROUP_W).transpose(0, 2, 1, 3)
        y_hy = _hy_conv(hyp, spec, m1, m3, tbr, tbi, f2, hy_skip[l], L)
        y_hy = y_hy.transpose(0, 2, 1, 3).reshape(B, L, GROUP_W)

        a_log2 = -jnp.exp(mb_a_log[l].astype(f32)) * LOG2E
        dt_bias = jnp.concatenate([_pad_cols(mb_dt_bias[l, d][None], SMALL_W, left=d * MB_H) for d in range(2)])
        a_rows = jnp.concatenate([_pad_cols(a_log2[d][None], SMALL_W, left=d * MB_H) for d in range(2)])
        y_mf, y_mb = _ssd2(proj, small, dt_bias, a_rows, ee_mb, ee2_mb)

        bias_i = _pad_cols(ml_gate_b[l][:, 0, :].reshape(1, -1), SMALL_W, left=ML_LANE)
        bias_f = _pad_cols(ml_gate_b[l][:, 1, :].reshape(1, -1), SMALL_W, left=ML_LANE)
        h_f, h_b = _mlstm2(proj, small, bias_i, bias_f, ee_ml)

        y_na = _na(proj, _na_bias_table(na_rpb[l]))

        x = _outproj2(x, y_hy, proj, y_mf, y_mb, h_f, h_b, y_na, jnp.repeat(mb_d[l], MB_P)[None], mb_norm_w[l][None],
                      ml_norm_w[l][None], w_out[l].astype(jnp.bfloat16).reshape(4, GROUP_W, D))
    return x
```
